```python
import math
import jax
import jax.numpy as jnp
from jax import lax
import numpy as np

D_MODEL = 1024
BATCH = 8
SEQ = 2048
DEPTH = 2
DEC_BATCH = 128
DEC_SEQ = 1
PAST_LEN = 16384
PAGE_SIZE = 128

GLA_HEADS = 4
GLA_DV = D_MODEL // GLA_HEADS
GLA_DK = GLA_DV // 2
GLA_GATE_RANK = 16
GLA_TAU = 16.0
GLA_QK = GLA_HEADS * GLA_DK
GLA_V = GLA_HEADS * GLA_DV
ML_HEADS = 4
ML_DH = D_MODEL // ML_HEADS
ML_W = ML_HEADS * ML_DH
CONV_W = 4
CHUNK = 64
PEER_HEADS = 8
PEER_DQ = 256
N_KEYS = 128
N_EXPERTS = N_KEYS * N_KEYS
PEER_TOPK = 16
TOKEN_BLOCK = 256
EPS = 1e-6

SPLITS = (GLA_QK, GLA_QK, GLA_V, GLA_V, GLA_GATE_RANK,
          ML_W, ML_W, ML_W, ML_HEADS, ML_HEADS, ML_W,
          D_MODEL, D_MODEL)
N_IN = (2 * GLA_QK + 2 * GLA_V + GLA_GATE_RANK + 4 * ML_W + 2 * ML_HEADS + 2 * D_MODEL)

kernel_name = "gla_mlstm_peer_hybrid_step"


def _rmsnorm(x, g):
    xf = x.astype(jnp.float32)
    y = xf * lax.rsqrt(jnp.mean(xf * xf, axis=-1, keepdims=True) + EPS)
    return (y * g.astype(jnp.float32)).astype(x.dtype)


def _headnorm(x, g, n_heads):
    shp = x.shape
    xf = x.astype(jnp.float32).reshape(shp[:-1] + (n_heads, shp[-1] // n_heads))
    y = xf * lax.rsqrt(jnp.mean(xf * xf, axis=-1, keepdims=True) + EPS)
    return y.reshape(shp) * g.astype(jnp.float32)


def _chunk_len(t):
    return CHUNK if t % CHUNK == 0 else t


def _to_chunks(x, n, l):
    b = x.shape[0]
    x = x.reshape((b, n, l) + x.shape[2:])
    return jnp.moveaxis(x, [1, 3], [0, 2])


def _from_chunks(o):
    o = jnp.moveaxis(o, [0, 2], [1, 3])
    return o.reshape((o.shape[0], o.shape[1] * o.shape[2]) + o.shape[3:])


def _gla(q, k, v, log_a, s0):
    t = q.shape[1]
    l = _chunk_len(t)
    n = t // l
    f32 = jnp.float32
    qc = _to_chunks(q.astype(f32), n, l)
    kc = _to_chunks(k.astype(f32), n, l)
    vc = _to_chunks(v.astype(f32), n, l)
    ac = _to_chunks(log_a.astype(f32), n, l)
    mask = jnp.tril(jnp.ones((l, l), dtype=bool))

    def step(s, inp):
        qb, kb, vb, ab = inp
        bc = jnp.cumsum(ab, axis=2)
        diff = bc[:, :, :, None, :] - bc[:, :, None, :, :]
        dec = jnp.exp(jnp.where(mask[:, :, None], diff, -jnp.inf))
        att = jnp.einsum('bhld,bhsd,bhlsd->bhls', qb, kb, dec)
        o = (jnp.einsum('bhld,bhdv->bhlv', qb * jnp.exp(bc), s)
             + jnp.einsum('bhls,bhsv->bhlv', att, vb))
        b_last = bc[:, :, -1, :]
        s = (s * jnp.exp(b_last)[..., None]
             + jnp.einsum('bhsd,bhsv->bhdv', kb * jnp.exp(b_last[:, :, None, :] - bc), vb))
        return s, o

    s, o = lax.scan(step, s0.astype(f32), (qc, kc, vc, ac))
    return _from_chunks(o), s


def _mlstm(q, k, v, i_pre, f_pre, c0, n0, m0):
    t = q.shape[1]
    l = _chunk_len(t)
    n = t // l
    f32 = jnp.float32
    qc = _to_chunks(q.astype(f32), n, l)
    kc = _to_chunks(k.astype(f32), n, l)
    vc = _to_chunks(v.astype(f32), n, l)
    lic = _to_chunks(i_pre.astype(f32), n, l)
    lfc = _to_chunks(jax.nn.log_sigmoid(f_pre.astype(f32)), n, l)
    mask = jnp.tril(jnp.ones((l, l), dtype=bool))

    def step(carry, inp):
        c, nv, m = carry
        qb, kb, vb, lib, lfb = inp
        b = jnp.cumsum(lfb, axis=-1)
        a_inter = b + m[..., None]
        dmat = jnp.where(mask, b[..., :, None] - b[..., None, :] + lib[..., None, :], -jnp.inf)
        m_t = jnp.maximum(a_inter, jnp.max(dmat, axis=-1))
        w_inter = jnp.exp(a_inter - m_t)
        s = jnp.einsum('bhld,bhsd->bhls', qb, kb) * jnp.exp(dmat - m_t[..., None])
        num = (w_inter[..., None] * jnp.einsum('bhld,bhdv->bhlv', qb, c)
               + jnp.einsum('bhls,bhsv->bhlv', s, vb))
        den = w_inter * jnp.einsum('bhld,bhd->bhl', qb, nv) + jnp.sum(s, axis=-1)
        h = num / jnp.maximum(jnp.abs(den), jnp.exp(-m_t))[..., None]
        b_last = b[..., -1]
        g_s = b_last[..., None] - b + lib
        m_new = jnp.maximum(b_last + m, jnp.max(g_s, axis=-1))
        f_c = jnp.exp(b_last + m - m_new)
        w_s = jnp.exp(g_s - m_new[..., None])
        c = f_c[..., None, None] * c + jnp.einsum('bhs,bhsd,bhsv->bhdv', w_s, kb, vb)
        nv = f_c[..., None] * nv + jnp.einsum('bhs,bhsd->bhd', w_s, kb)
        return (c, nv, m_new), h

    (c, nv, m), h = lax.scan(step, (c0.astype(f32), n0.astype(f32), m0.astype(f32)),
                             (qc, kc, vc, lic, lfc))
    return _from_chunks(h), c, nv, m


def _causal_conv(x, buf, w, b):
    t = x.shape[1]
    xp = jnp.concatenate([buf.astype(x.dtype), x], axis=1)
    y = b
    for j in range(CONV_W):
        y = y + w[j] * xp[:, j:j + t]
    return y, xp[:, xp.shape[1] - (CONV_W - 1):]


def _mixer(h, s_gla, s_c, s_n, s_m, s_conv, w_in, gla_w2, gla_b2, gla_norm_g,
           conv_w, conv_b, ml_i_b, ml_f_b, ml_norm_g, w_out):
    bsz, t, _ = h.shape
    proj = h @ w_in
    points = np.cumsum(np.array(SPLITS))[:-1].tolist()
    (aq, ak, av, ag, alr, bq, bk, bv, bi, bf, bo, gate_a, gate_b) = jnp.split(proj, points, axis=-1)

    log_a = jax.nn.log_sigmoid((alr @ gla_w2 + gla_b2).astype(jnp.float32)) / GLA_TAU
    q_a = aq.reshape(bsz, t, GLA_HEADS, GLA_DK) * (GLA_DK ** -0.5)
    k_a = ak.reshape(bsz, t, GLA_HEADS, GLA_DK)
    v_a = av.reshape(bsz, t, GLA_HEADS, GLA_DV)
    o_a, s_gla_new = _gla(q_a, k_a, v_a, log_a.reshape(bsz, t, GLA_HEADS, GLA_DK), s_gla)
    o_a = (_headnorm(o_a.reshape(bsz, t, GLA_V), gla_norm_g, GLA_HEADS).astype(h.dtype)
           * jax.nn.silu(ag))

    qk, conv_new = _causal_conv(jnp.concatenate([bq, bk], axis=-1), s_conv, conv_w, conv_b)
    qk = jax.nn.silu(qk)
    q_b, k_b = jnp.split(qk, 2, axis=-1)
    o_b, c_new, n_new, m_new = _mlstm(
        q_b.reshape(bsz, t, ML_HEADS, ML_DH),
        k_b.reshape(bsz, t, ML_HEADS, ML_DH) * (ML_DH ** -0.5),
        bv.reshape(bsz, t, ML_HEADS, ML_DH),
        bi + ml_i_b, bf + ml_f_b, s_c, s_n, s_m)
    o_b = (_headnorm(o_b.reshape(bsz, t, ML_W), ml_norm_g, ML_HEADS).astype(h.dtype)
           * jax.nn.sigmoid(bo))

    mix = (jax.nn.sigmoid(gate_a) * o_a + jax.nn.sigmoid(gate_b) * o_b) @ w_out
    return mix, s_gla_new, c_new, n_new, m_new, conv_new


def _peer(h, wq, subkeys, u_tab, v_tab):
    bsz, t, d = h.shape
    n_tok = bsz * t
    pad = (-n_tok) % TOKEN_BLOCK
    xf = jnp.pad(h.reshape(n_tok, d), ((0, pad), (0, 0)))
    blocks = xf.reshape(-1, TOKEN_BLOCK, d)

    def one(xb):
        q = (xb @ wq).reshape(TOKEN_BLOCK, PEER_HEADS, 2, PEER_DQ // 2)
        s = jnp.einsum('thcd,ckd->thck', q, subkeys)
        sv, si = lax.top_k(s, PEER_TOPK)
        cand = (sv[:, :, 0, :, None] + sv[:, :, 1, None, :]).reshape(TOKEN_BLOCK, PEER_HEADS, -1)
        cidx = (si[:, :, 0, :, None] * N_KEYS + si[:, :, 1, None, :]).reshape(TOKEN_BLOCK, PEER_HEADS, -1)
        top_v, top_pos = lax.top_k(cand, PEER_TOPK)
        eidx = jnp.take_along_axis(cidx, top_pos, axis=-1)
        g = jax.nn.softmax(top_v.astype(jnp.float32), axis=-1).astype(xb.dtype)
        u = u_tab[eidx]
        a = jax.nn.gelu(jnp.einsum('td,thkd->thk', xb, u), approximate=False)
        v = v_tab[eidx]
        return jnp.einsum('thk,thkd->td', g * a, v)

    out = lax.map(one, blocks).reshape(-1, d)[:n_tok]
    return out.reshape(bsz, t, d)


def _trunk(x, st_gla, st_c, st_n, st_m, st_conv,
           norm1_g, w_in, gla_w2, gla_b2, gla_norm_g, conv_w, conv_b, ml_i_b, ml_f_b,
           ml_norm_g, w_out, norm2_g, peer_wq, peer_subkeys, peer_u, peer_v, final_g):
    new_gla, new_c, new_n, new_m, new_conv = [], [], [], [], []
    for li in range(DEPTH):
        h = _rmsnorm(x, norm1_g[li])
        mix, sg, sc, sn, sm, sv = _mixer(
            h, st_gla[li], st_c[li], st_n[li], st_m[li], st_conv[li],
            w_in[li], gla_w2[li], gla_b2[li], gla_norm_g[li], conv_w[li], conv_b[li],
            ml_i_b[li], ml_f_b[li], ml_norm_g[li], w_out[li])
        x = x + mix
        x = x + _peer(_rmsnorm(x, norm2_g[li]), peer_wq[li], peer_subkeys[li], peer_u[li], peer_v[li])
        new_gla.append(sg.astype(x.dtype))
        new_c.append(sc.astype(x.dtype))
        new_n.append(sn.astype(x.dtype))
        new_m.append(sm.astype(x.dtype))
        new_conv.append(sv.astype(x.dtype))
    y = _rmsnorm(x, final_g)
    return (y, jnp.stack(new_gla), jnp.stack(new_c), jnp.stack(new_n),
            jnp.stack(new_m), jnp.stack(new_conv))


def setup_inputs(seed: int = 0) -> dict:
    key = jax.random.key(seed)
    ks = jax.random.split(key, 24)
    nrm = jax.random.normal
    f32 = jnp.float32
    d = D_MODEL
    inp = {}
    inp['x_prompt'] = nrm(ks[0], (BATCH, SEQ, d), f32)
    inp['x_sample'] = nrm(ks[1], (DEC_BATCH, DEC_SEQ, d), f32)
    inp['state_gla'] = 0.3 * nrm(ks[2], (DEPTH, DEC_BATCH, GLA_HEADS, GLA_DK, GLA_DV), f32)
    inp['state_mlstm_c'] = 0.3 * nrm(ks[3], (DEPTH, DEC_BATCH, ML_HEADS, ML_DH, ML_DH), f32)
    inp['state_mlstm_n'] = 0.3 * nrm(ks[4], (DEPTH, DEC_BATCH, ML_HEADS, ML_DH), f32)
    inp['state_mlstm_m'] = nrm(ks[5], (DEPTH, DEC_BATCH, ML_HEADS), f32)
    inp['state_conv'] = nrm(ks[6], (DEPTH, DEC_BATCH, CONV_W - 1, 2 * ML_W), f32)
    inp['norm1_g'] = 1.0 + 0.02 * nrm(ks[7], (DEPTH, d), f32)
    inp['w_in'] = nrm(ks[8], (DEPTH, d, N_IN), f32) * d ** -0.5
    inp['gla_w2'] = nrm(ks[9], (DEPTH, GLA_GATE_RANK, GLA_QK), f32) * GLA_GATE_RANK ** -0.5
    inp['gla_b2'] = 0.1 * nrm(ks[10], (DEPTH, GLA_QK), f32)
    inp['gla_norm_g'] = 1.0 + 0.02 * nrm(ks[11], (DEPTH, GLA_V), f32)
    inp['conv_w'] = nrm(ks[12], (DEPTH, CONV_W, 2 * ML_W), f32) * CONV_W ** -0.5
    inp['conv_b'] = 0.02 * nrm(ks[13], (DEPTH, 2 * ML_W), f32)
    inp['ml_i_b'] = 0.1 * nrm(ks[14], (DEPTH, ML_HEADS), f32)
    inp['ml_f_b'] = (jnp.linspace(3.0, 6.0, ML_HEADS, dtype=f32)[None, :]
                     + 0.1 * nrm(ks[15], (DEPTH, ML_HEADS), f32))
    inp['ml_norm_g'] = 1.0 + 0.02 * nrm(ks[16], (DEPTH, ML_W), f32)
    inp['w_out'] = nrm(ks[17], (DEPTH, d, d), f32) * d ** -0.5
    inp['norm2_g'] = 1.0 + 0.02 * nrm(ks[18], (DEPTH, d), f32)
    inp['peer_wq'] = nrm(ks[19], (DEPTH, d, PEER_HEADS * PEER_DQ), f32) * d ** -0.5
    inp['peer_subkeys'] = nrm(ks[20], (DEPTH, 2, N_KEYS, PEER_DQ // 2), f32) * (PEER_DQ // 2) ** -0.5
    inp['peer_u'] = nrm(ks[21], (DEPTH, N_EXPERTS, d), f32) * d ** -0.5
    inp['peer_v'] = nrm(ks[22], (DEPTH, N_EXPERTS, d), f32) * PEER_HEADS ** -0.5
    inp['final_g'] = 1.0 + 0.02 * nrm(ks[23], (d,), f32)
    return inp


def reference(x_prompt, x_sample, state_gla, state_mlstm_c, state_mlstm_n, state_mlstm_m,
              state_conv, norm1_g, w_in, gla_w2, gla_b2, gla_norm_g, conv_w, conv_b,
              ml_i_b, ml_f_b, ml_norm_g, w_out, norm2_g, peer_wq, peer_subkeys,
              peer_u, peer_v, final_g):
    bp = x_prompt.shape[0]
    dt = x_prompt.dtype
    z_gla = jnp.zeros((DEPTH, bp, GLA_HEADS, GLA_DK, GLA_DV), dt)
    z_c = jnp.zeros((DEPTH, bp, ML_HEADS, ML_DH, ML_DH), dt)
    z_n = jnp.zeros((DEPTH, bp, ML_HEADS, ML_DH), dt)
    z_m = jnp.zeros((DEPTH, bp, ML_HEADS), dt)
    z_conv = jnp.zeros((DEPTH, bp, CONV_W - 1, 2 * ML_W), dt)
    y_prompt, p_gla, p_c, p_n, p_m, p_conv = _trunk(
        x_prompt, z_gla, z_c, z_n, z_m, z_conv,
        norm1_g, w_in, gla_w2, gla_b2, gla_norm_g, conv_w, conv_b, ml_i_b, ml_f_b,
        ml_norm_g, w_out, norm2_g, peer_wq, peer_subkeys, peer_u, peer_v, final_g)
    y_sample, s_gla, s_c, s_n, s_m, s_conv = _trunk(
        x_sample, state_gla, state_mlstm_c, state_mlstm_n, state_mlstm_m, state_conv,
        norm1_g, w_in, gla_w2, gla_b2, gla_norm_g, conv_w, conv_b, ml_i_b, ml_f_b,
        ml_norm_g, w_out, norm2_g, peer_wq, peer_subkeys, peer_u, peer_v, final_g)
    return (y_prompt, y_sample, p_gla, p_c, p_n, p_m, p_conv, s_gla, s_c, s_n, s_m, s_conv)
```

```python
import functools

import jax
import jax.numpy as jnp
from jax import lax
from jax.experimental import pallas as pl
from jax.experimental.pallas import tpu as pltpu

F32 = jnp.float32
BF16 = jnp.bfloat16
HIGHEST = lax.Precision.HIGHEST

D_MODEL = 1024
GLA_HEADS = 4
GLA_DK = 128
GLA_DV = 256
GLA_GATE_RANK = 16
GLA_TAU = 16.0
GLA_QK = GLA_HEADS * GLA_DK
ML_HEADS = 4
ML_DH = 256
ML_W = ML_HEADS * ML_DH
CONV_W = 4
CHUNK = 64
PEER_HEADS = 8
PEER_DQ = 256
N_KEYS = 128
N_EXPERTS = N_KEYS * N_KEYS
PEER_TOPK = 16
EPS = 1e-6

LANES = 128
BIG_COLS = 9 * D_MODEL
SMALL_COLS = 3 * LANES
VMEM_LIMIT_BYTES = 56 * 1024 * 1024

NT_DIMS = (((1,), (1,)), ((), ()))
TN_DIMS = (((0,), (0,)), ((), ()))


def _params(sem):
    return pltpu.CompilerParams(dimension_semantics=sem, vmem_limit_bytes=VMEM_LIMIT_BYTES)


def _log_sigmoid(x):
    return jnp.minimum(x, 0.0) - jnp.log1p(jnp.exp(-jnp.abs(x)))


def _sigmoid(x):
    return 1.0 / (1.0 + jnp.exp(-x))


def _silu(x):
    return x * _sigmoid(x)


def _rms(x, g):
    return x * lax.rsqrt(jnp.mean(x * x, axis=-1, keepdims=True) + EPS) * g


def _lane_col(a, j):
    lane = lax.broadcasted_iota(jnp.int32, a.shape, 1)
    return jnp.sum(jnp.where(lane == j, a, 0.0), axis=1, keepdims=True)


def _inproj_kernel(x_ref, g_ref, wbig_ref, wsm_ref, wgt_ref, big_ref, sm_ref, gt_ref, h_scr):
    @pl.when(pl.program_id(1) == 0)
    def _():
        hb = _rms(x_ref[...], g_ref[...]).astype(BF16)
        h_scr[...] = hb
        sm_ref[...] = jnp.dot(hb, wsm_ref[...], preferred_element_type=F32)
        gt_ref[...] = lax.dot_general(wgt_ref[...], hb, NT_DIMS, preferred_element_type=F32)

    big_ref[...] = jnp.dot(h_scr[...], wbig_ref[...], preferred_element_type=F32).astype(BF16)


def _inproj(x, g, w_big, w_small, w_gt, tm):
    t = x.shape[0]
    tn = D_MODEL
    return pl.pallas_call(
        _inproj_kernel,
        grid=(t // tm, BIG_COLS // tn),
        in_specs=[
            pl.BlockSpec((tm, D_MODEL), lambda i, j: (i, 0)),
            pl.BlockSpec((1, D_MODEL), lambda i, j: (0, 0)),
            pl.BlockSpec((D_MODEL, tn), lambda i, j: (0, j)),
            pl.BlockSpec((D_MODEL, SMALL_COLS), lambda i, j: (0, 0)),
            pl.BlockSpec((8, D_MODEL), lambda i, j: (0, 0)),
        ],
        out_specs=[
            pl.BlockSpec((tm, tn), lambda i, j: (i, j)),
            pl.BlockSpec((tm, SMALL_COLS), lambda i, j: (i, 0)),
            pl.BlockSpec((8, tm), lambda i, j: (0, i)),
        ],
        out_shape=[
            jax.ShapeDtypeStruct((t, BIG_COLS), BF16),
            jax.ShapeDtypeStruct((t, SMALL_COLS), F32),
            jax.ShapeDtypeStruct((8, t), F32),
        ],
        scratch_shapes=[pltpu.VMEM((tm, D_MODEL), BF16)],
        compiler_params=_params(("arbitrary", "arbitrary")),
        name="inproj",
    )(x, g, w_big, w_small, w_gt)


def _headnorm(o, g_row):
    return o * lax.rsqrt(jnp.mean(o * o, axis=-1, keepdims=True) + EPS) * g_row


def _mixer_prompt_kernel(x_ref, aq_ref, ak_ref, av_ref, ag_ref, bq_ref, bk_ref, bv_ref, bo_ref,
                         ga_ref, gb_ref, sm_ref, gt_ref, w2_ref, b2_ref, gng_ref, cw_ref, cb_ref,
                         gbr_ref, gbc_ref, mng_ref, wout_ref,
                         x1_ref, gla_ref, c_ref, n_ref, m_ref, conv_ref,
                         st_scr, xp_scr, qk_scr, mix_scr, *, tc, nblk):
    i = pl.program_id(1)
    L = CHUNK

    @pl.when(i == 0)
    def _init():
        st_scr[...] = jnp.zeros_like(st_scr)
        c_ref[...] = jnp.zeros_like(c_ref)
        n_ref[...] = jnp.zeros_like(n_ref)
        m_ref[...] = jnp.zeros_like(m_ref)
        xp_scr[0:8, :] = jnp.zeros((8, 2 * ML_W), F32)

    xp_scr[8:8 + tc, 0:ML_W] = bq_ref[...].astype(F32)
    xp_scr[8:8 + tc, ML_W:2 * ML_W] = bk_ref[...].astype(F32)
    y = cb_ref[...]
    for j in range(CONV_W):
        y = y + cw_ref[j:j + 1, :] * xp_scr[5 + j:5 + j + tc, :]
    qk = _silu(y)
    qk_scr[:, 0:ML_W] = qk[:, 0:ML_W].astype(BF16)
    qk_scr[:, ML_W:2 * ML_W] = (qk[:, ML_W:2 * ML_W] * (ML_DH ** -0.5)).astype(BF16)
    last3 = xp_scr[tc + 5:tc + 8, :]
    xp_scr[5:8, :] = last3

    @pl.when(i == nblk - 1)
    def _():
        conv_ref[0] = last3

    row = lax.broadcasted_iota(jnp.int32, (L, L), 0)
    col = lax.broadcasted_iota(jnp.int32, (L, L), 1)
    causal = col <= row
    tril = causal.astype(F32)
    triu = (row <= col).astype(F32)
    lane_row = lax.broadcasted_iota(jnp.int32, (1, LANES), 1)

    def chunk(c, carry):
        r0 = pl.multiple_of(c * L, L)
        rows = pl.ds(r0, L)
        sm = sm_ref[rows, :]
        alr = sm[:, 0:LANES]
        z = jnp.dot(alr, w2_ref[...], precision=HIGHEST, preferred_element_type=F32) + b2_ref[...]
        log_a = _log_sigmoid(z) * (1.0 / GLA_TAU)
        bc = jnp.dot(tril, log_a, precision=HIGHEST, preferred_element_type=F32)

        li_c_all = sm[:, LANES:2 * LANES] + gbr_ref[:, 0:LANES]
        lf_c_all = _log_sigmoid(sm[:, 2 * LANES:3 * LANES] + gbr_ref[:, LANES:2 * LANES])
        bcum_c_all = jnp.dot(tril, lf_c_all, precision=HIGHEST, preferred_element_type=F32)
        gtb = gt_ref[c] + gbc_ref[...]
        bcum_r_all = jnp.dot(_log_sigmoid(gtb), triu, precision=HIGHEST, preferred_element_type=F32)

        for h in range(GLA_HEADS):
            sl = slice(h * GLA_DK, (h + 1) * GLA_DK)
            vl = slice(h * GLA_DV, (h + 1) * GLA_DV)
            b = bc[:, sl]
            mid = b[L // 2 - 1:L // 2, :]
            bl = b[L - 1:L, :]
            q = aq_ref[rows, sl].astype(F32) * (GLA_DK ** -0.5)
            k = ak_ref[rows, sl].astype(F32)
            v = av_ref[rows, vl]
            q_in = (q * jnp.exp(b)).astype(BF16)
            q_at = (q * jnp.exp(b - mid)).astype(BF16)
            k_at = (k * jnp.exp(mid - b)).astype(BF16)
            k_out = (k * jnp.exp(bl - b)).astype(BF16)
            att = lax.dot_general(q_at, k_at, NT_DIMS, preferred_element_type=F32)
            att = jnp.where(causal, att, 0.0)
            st = st_scr[h]
            o = lax.dot_general(q_in, st.astype(BF16), NT_DIMS, preferred_element_type=F32)
            o = o + jnp.dot(att.astype(BF16), v, preferred_element_type=F32)
            st_scr[h] = st * jnp.exp(bl) + lax.dot_general(v, k_out, TN_DIMS, preferred_element_type=F32)
            o_a = _headnorm(o, gng_ref[:, vl]) * _silu(ag_ref[rows, vl].astype(F32))

            li_c = _lane_col(li_c_all, h)
            b_c = _lane_col(bcum_c_all, h)
            li_r = gtb[h:h + 1, :]
            b_r = bcum_r_all[ML_HEADS + h:ML_HEADS + h + 1, :]
            m_all = m_ref[0]
            m_prev = jnp.sum(jnp.where(lane_row == h, m_all, 0.0), axis=1, keepdims=True)
            a_int = b_c + m_prev
            dm = jnp.where(causal, b_c - b_r + li_r, -jnp.inf)
            m_t = jnp.maximum(a_int, jnp.max(dm, axis=1, keepdims=True))
            w_int = jnp.exp(a_int - m_t)
            dexp = jnp.exp(dm - m_t)
            qb = qk_scr[rows, vl]
            kb = qk_scr[rows, ML_W + h * ML_DH:ML_W + (h + 1) * ML_DH]
            vb = bv_ref[rows, vl]
            s = lax.dot_general(qb, kb, NT_DIMS, preferred_element_type=F32) * dexp
            cst = c_ref[0, h]
            num = (w_int * jnp.dot(qb, cst.astype(BF16), preferred_element_type=F32)
                   + jnp.dot(s.astype(BF16), vb, preferred_element_type=F32))
            nrow = n_ref[0, h:h + 1, :]
            den = (w_int * jnp.sum(qb.astype(F32) * nrow, axis=1, keepdims=True)
                   + jnp.sum(s, axis=1, keepdims=True))
            hh = num / jnp.maximum(jnp.abs(den), jnp.exp(-m_t))
            b_last = b_c[L - 1:L, :]
            g_c = b_last - b_c + li_c
            m_new = jnp.maximum(b_last + m_prev, jnp.max(g_c, axis=0, keepdims=True))
            f_c = jnp.exp(b_last + m_prev - m_new)
            w_s = jnp.exp(g_c - m_new)
            kw = kb.astype(F32) * w_s
            c_ref[0, h] = f_c * cst + lax.dot_general(kw.astype(BF16), vb, TN_DIMS,
                                                      preferred_element_type=F32)
            n_ref[0, h:h + 1, :] = f_c * nrow + jnp.sum(kw, axis=0, keepdims=True)
            m_ref[0] = jnp.where(lane_row == h, m_new, m_all)
            o_b = _headnorm(hh, mng_ref[:, vl]) * _sigmoid(bo_ref[rows, vl].astype(F32))

            mix = (_sigmoid(ga_ref[rows, vl].astype(F32)) * o_a
                   + _sigmoid(gb_ref[rows, vl].astype(F32)) * o_b)
            mix_scr[rows, vl] = mix.astype(BF16)
        return carry

    lax.fori_loop(0, tc // L, chunk, 0)

    x1_ref[...] = x_ref[...] + jnp.dot(mix_scr[...], wout_ref[...], preferred_element_type=F32)

    @pl.when(i == nblk - 1)
    def _fin():
        for h in range(GLA_HEADS):
            gla_ref[0, h] = st_scr[h].T


def _mixer_prompt(x, big, sm, gtc, lw, bsz, seq, tc):
    nblk = seq // tc
    rb = lambda b, i: (b * nblk + i, 0)

    def seg(width, idx):
        return pl.BlockSpec((tc, width), lambda b, i: (b * nblk + i, idx))

    const = lambda shape: pl.BlockSpec(shape, lambda b, i: tuple(0 for _ in shape))
    in_specs = [
        pl.BlockSpec((tc, D_MODEL), rb),
        seg(GLA_QK, 0), seg(GLA_QK, 1),
        seg(D_MODEL, 1), seg(D_MODEL, 2),
        seg(D_MODEL, 3), seg(D_MODEL, 4), seg(D_MODEL, 5),
        seg(D_MODEL, 6), seg(D_MODEL, 7), seg(D_MODEL, 8),
        pl.BlockSpec((tc, SMALL_COLS), rb),
        pl.BlockSpec((tc // CHUNK, 8, CHUNK), lambda b, i: (b * nblk + i, 0, 0)),
        const((LANES, GLA_QK)), const((1, GLA_QK)), const((1, D_MODEL)),
        const((CONV_W, 2 * ML_W)), const((1, 2 * ML_W)),
        const((1, 2 * LANES)), const((8, 1)), const((1, D_MODEL)),
        const((D_MODEL, D_MODEL)),
    ]
    out_specs = [
        pl.BlockSpec((tc, D_MODEL), rb),
        pl.BlockSpec((1, GLA_HEADS, GLA_DK, GLA_DV), lambda b, i: (b, 0, 0, 0)),
        pl.BlockSpec((1, ML_HEADS, ML_DH, ML_DH), lambda b, i: (b, 0, 0, 0)),
        pl.BlockSpec((1, ML_HEADS, ML_DH), lambda b, i: (b, 0, 0)),
        pl.BlockSpec((1, 1, LANES), lambda b, i: (b, 0, 0)),
        pl.BlockSpec((1, CONV_W - 1, 2 * ML_W), lambda b, i: (b, 0, 0)),
    ]
    out_shape = [
        jax.ShapeDtypeStruct((bsz * seq, D_MODEL), F32),
        jax.ShapeDtypeStruct((bsz, GLA_HEADS, GLA_DK, GLA_DV), F32),
        jax.ShapeDtypeStruct((bsz, ML_HEADS, ML_DH, ML_DH), F32),
        jax.ShapeDtypeStruct((bsz, ML_HEADS, ML_DH), F32),
        jax.ShapeDtypeStruct((bsz, 1, LANES), F32),
        jax.ShapeDtypeStruct((bsz, CONV_W - 1, 2 * ML_W), F32),
    ]
    scratch = [
        pltpu.VMEM((GLA_HEADS, GLA_DV, GLA_DK), F32),
        pltpu.VMEM((tc + 8, 2 * ML_W), F32),
        pltpu.VMEM((tc, 2 * ML_W), BF16),
        pltpu.VMEM((tc, D_MODEL), BF16),
    ]
    return pl.pallas_call(
        functools.partial(_mixer_prompt_kernel, tc=tc, nblk=nblk),
        grid=(bsz, nblk),
        in_specs=in_specs, out_specs=out_specs, out_shape=out_shape,
        scratch_shapes=scratch,
        compiler_params=_params(("arbitrary", "arbitrary")),
        name="mixer_prompt",
    )(x, big, big, big, big, big, big, big, big, big, big, sm, gtc,
      lw["w2p"], lw["b2"], lw["gng"], lw["cw"], lw["cb"], lw["gbr"], lw["gbc"], lw["mng"], lw["wout"])


def _col_bcast(row):
    return jnp.broadcast_to(row, (LANES, LANES)).T


def _mixer_sample_kernel(x_ref, big_ref, sm_ref, gla_ref, c_ref, n_ref, m_ref, conv_ref,
                         w2_ref, b2_ref, gng_ref, cw_ref, cb_ref, gbr_ref, mng_ref, wout_ref,
                         x1_ref, gla_o, c_o, n_o, m_o, conv_o,
                         a_s, qa_s, ka_s, va_s, kw_s, qb_s, vb_s, fc_s, den_s, oa_s, num_s, *, nb):
    b = pl.program_id(0)
    seg = lambda idx: slice(idx * D_MODEL, (idx + 1) * D_MODEL)
    lane = lax.broadcasted_iota(jnp.int32, (nb, LANES), 1)

    @pl.when(b == 0)
    def _prep():
        sm = sm_ref[...]
        z = jnp.dot(sm[:, 0:LANES], w2_ref[...], precision=HIGHEST, preferred_element_type=F32) + b2_ref[...]
        a_s[...] = jnp.exp(_log_sigmoid(z) * (1.0 / GLA_TAU))
        qa_s[...] = big_ref[:, 0:GLA_QK].astype(F32) * (GLA_DK ** -0.5)
        ka_s[...] = big_ref[:, GLA_QK:2 * GLA_QK].astype(F32)
        va_s[...] = big_ref[:, seg(1)].astype(F32)
        vb_s[...] = big_ref[:, seg(5)].astype(F32)
        xq = big_ref[:, seg(3)].astype(F32)
        xk = big_ref[:, seg(4)].astype(F32)
        w = 2 * ML_W
        y = cb_ref[...]
        for j in range(CONV_W - 1):
            y = y + cw_ref[j:j + 1, :] * conv_ref[:, j * w:(j + 1) * w]
        y = y + cw_ref[CONV_W - 1:CONV_W, :] * jnp.concatenate([xq, xk], axis=1)
        qk = _silu(y)
        qb = qk[:, 0:ML_W]
        kb = qk[:, ML_W:w] * (ML_DH ** -0.5)
        qb_s[...] = qb
        conv_o[:, 0:2 * w] = conv_ref[:, w:3 * w]
        conv_o[:, 2 * w:2 * w + ML_W] = xq
        conv_o[:, 2 * w + ML_W:3 * w] = xk

        li = sm[:, LANES:2 * LANES] + gbr_ref[:, 0:LANES]
        lf = _log_sigmoid(sm[:, 2 * LANES:3 * LANES] + gbr_ref[:, LANES:2 * LANES])
        m_old = m_ref[...]
        m_new = jnp.maximum(lf + m_old, li)
        f_c = jnp.exp(lf + m_old - m_new)
        w_s = jnp.exp(li - m_new)
        m_o[...] = m_new
        fc_s[...] = f_c
        den = jnp.zeros((nb, LANES), F32)
        for h in range(ML_HEADS):
            vl = slice(h * ML_DH, (h + 1) * ML_DH)
            kw = kb[:, vl] * _lane_col(w_s, h)
            kw_s[:, vl] = kw
            n_new = _lane_col(f_c, h) * n_ref[:, vl] + kw
            n_o[:, vl] = n_new
            den = jnp.where(lane == h, jnp.sum(qb[:, vl] * n_new, axis=1, keepdims=True), den)
        den_s[...] = den
        oa_s[...] = jnp.zeros_like(oa_s)
        num_s[...] = jnp.zeros_like(num_s)

    tile = pl.ds(pl.multiple_of(lax.shift_right_logical(b, 3) * 8, 8), 8)
    sub = jnp.bitwise_and(b, 7)

    def row_get(ref, cols):
        t = ref[tile, cols]
        pick = lax.broadcasted_iota(jnp.int32, t.shape, 0) == sub
        return jnp.sum(jnp.where(pick, t, 0.0), axis=0, keepdims=True)

    def row_set(ref, cols, val):
        t = ref[tile, cols]
        pick = lax.broadcasted_iota(jnp.int32, t.shape, 0) == sub
        ref[tile, cols] = jnp.where(pick, val, t)

    lane_row = lax.broadcasted_iota(jnp.int32, (1, LANES), 1)
    fc_row = row_get(fc_s, slice(0, LANES))
    for h in range(GLA_HEADS):
        sl = slice(h * GLA_DK, (h + 1) * GLA_DK)
        vl = slice(h * GLA_DV, (h + 1) * GLA_DV)
        a_c = _col_bcast(row_get(a_s, sl))
        k_c = _col_bcast(row_get(ka_s, sl))
        q_c = _col_bcast(row_get(qa_s, sl))
        v_row = row_get(va_s, vl)
        s_old = gla_ref[0, h]
        halves = []
        for p in range(GLA_DV // LANES):
            ls = slice(p * LANES, (p + 1) * LANES)
            s_new = s_old[:, ls] * a_c + k_c * v_row[:, ls]
            gla_o[0, h, :, ls] = s_new
            halves.append(jnp.sum(q_c * s_new, axis=0, keepdims=True))
        row_set(oa_s, vl, jnp.concatenate(halves, axis=1))

        f_c = jnp.sum(jnp.where(lane_row == h, fc_row, 0.0), axis=1, keepdims=True)
        vb_row = row_get(vb_s, vl)
        acc = [jnp.zeros((1, LANES), F32) for _ in range(ML_DH // LANES)]
        for r in range(ML_DH // LANES):
            rs = slice(r * LANES, (r + 1) * LANES)
            ks = slice(h * ML_DH + r * LANES, h * ML_DH + (r + 1) * LANES)
            kw_c = _col_bcast(row_get(kw_s, ks))
            qb_c = _col_bcast(row_get(qb_s, ks))
            for p in range(ML_DH // LANES):
                ls = slice(p * LANES, (p + 1) * LANES)
                c_new = f_c * c_ref[0, h, rs, ls] + kw_c * vb_row[:, ls]
                c_o[0, h, rs, ls] = c_new
                acc[p] = acc[p] + jnp.sum(qb_c * c_new, axis=0, keepdims=True)
        row_set(num_s, vl, jnp.concatenate(acc, axis=1))

    @pl.when(b == nb - 1)
    def _post():
        m_new = m_o[...]
        den = den_s[...]
        for h in range(GLA_HEADS):
            vl = slice(h * GLA_DV, (h + 1) * GLA_DV)
            o_a = _headnorm(oa_s[:, vl], gng_ref[:, vl]) * _silu(big_ref[:, D_MODEL * 2 + h * GLA_DV:
                                                                       D_MODEL * 2 + (h + 1) * GLA_DV].astype(F32))
            dn = jnp.maximum(jnp.abs(_lane_col(den, h)), jnp.exp(-_lane_col(m_new, h)))
            hh = num_s[:, vl] / dn
            o_b = _headnorm(hh, mng_ref[:, vl]) * _sigmoid(
                big_ref[:, 6 * D_MODEL + h * ML_DH:6 * D_MODEL + (h + 1) * ML_DH].astype(F32))
            ga = big_ref[:, 7 * D_MODEL + h * ML_DH:7 * D_MODEL + (h + 1) * ML_DH].astype(F32)
            gb = big_ref[:, 8 * D_MODEL + h * ML_DH:8 * D_MODEL + (h + 1) * ML_DH].astype(F32)
            oa_s[:, vl] = _sigmoid(ga) * o_a + _sigmoid(gb) * o_b
        x1_ref[...] = x_ref[...] + jnp.dot(oa_s[...].astype(BF16), wout_ref[...], preferred_element_type=F32)


def _mixer_sample(x, big, sm, st_gla, st_c, st_n, st_m, st_conv, lw):
    nb = x.shape[0]
    full = lambda shape: pl.BlockSpec(shape, lambda b: tuple(0 for _ in shape))
    in_specs = [
        full((nb, D_MODEL)), full((nb, BIG_COLS)), full((nb, SMALL_COLS)),
        pl.BlockSpec((1, GLA_HEADS, GLA_DK, GLA_DV), lambda b: (b, 0, 0, 0)),
        pl.BlockSpec((1, ML_HEADS, ML_DH, ML_DH), lambda b: (b, 0, 0, 0)),
        full((nb, ML_W)), full((nb, LANES)), full((nb, 3 * 2 * ML_W)),
        full((LANES, GLA_QK)), full((1, GLA_QK)), full((1, D_MODEL)),
        full((CONV_W, 2 * ML_W)), full((1, 2 * ML_W)), full((1, 2 * LANES)), full((1, D_MODEL)),
        full((D_MODEL, D_MODEL)),
    ]
    out_specs = [
        full((nb, D_MODEL)),
        pl.BlockSpec((1, GLA_HEADS, GLA_DK, GLA_DV), lambda b: (b, 0, 0, 0)),
        pl.BlockSpec((1, ML_HEADS, ML_DH, ML_DH), lambda b: (b, 0, 0, 0)),
        full((nb, ML_W)), full((nb, LANES)), full((nb, 3 * 2 * ML_W)),
    ]
    out_shape = [
        jax.ShapeDtypeStruct((nb, D_MODEL), F32),
        jax.ShapeDtypeStruct(st_gla.shape, F32),
        jax.ShapeDtypeStruct(st_c.shape, F32),
        jax.ShapeDtypeStruct((nb, ML_W), F32),
        jax.ShapeDtypeStruct((nb, LANES), F32),
        jax.ShapeDtypeStruct((nb, 3 * 2 * ML_W), F32),
    ]
    vm = lambda cols: pltpu.VMEM((nb, cols), F32)
    scratch = [vm(GLA_QK), vm(GLA_QK), vm(GLA_QK), vm(D_MODEL), vm(ML_W), vm(ML_W), vm(ML_W),
               vm(LANES), vm(LANES), vm(D_MODEL), vm(ML_W)]
    return pl.pallas_call(
        functools.partial(_mixer_sample_kernel, nb=nb),
        grid=(nb,),
        in_specs=in_specs, out_specs=out_specs, out_shape=out_shape,
        scratch_shapes=scratch,
        compiler_params=_params(("arbitrary",)),
        name="mixer_sample",
    )(x, big, sm, st_gla, st_c, st_n, st_m, st_conv,
      lw["w2p"], lw["b2"], lw["gng"], lw["cw"], lw["cb"], lw["gbr"], lw["mng"], lw["wout"])


def _peer_select_kernel(x_ref, g_ref, wq_ref, sk_ref, h2_ref, rank_ref, p1_ref, cnt_ref, p0_ref,
                        sc_scr, *, tq):
    nch = tq // LANES
    hb = _rms(x_ref[...], g_ref[...]).astype(BF16)
    h2_ref[...] = hb
    q = jnp.dot(hb, wq_ref[...], preferred_element_type=F32).astype(BF16)
    for hc in range(2 * PEER_HEADS):
        for ch in range(nch):
            qs = q[ch * LANES:(ch + 1) * LANES, hc * LANES:(hc + 1) * LANES]
            sc_scr[hc, ch] = lax.dot_general(sk_ref[hc % 2], qs, NT_DIMS, preferred_element_type=F32)

    K = PEER_TOPK
    neg = -jnp.inf
    iota_k = lax.broadcasted_iota(jnp.int32, (K, LANES), 0)
    iota_kf = iota_k.astype(F32)

    def body(it, carry):
        h = it // nch
        ch = it % nch
        s0 = sc_scr[2 * h, ch]
        s1 = sc_scr[2 * h + 1, ch]

        sv0 = jnp.zeros((K, LANES), F32)
        prev = jnp.full((1, LANES), jnp.inf, F32)
        for j in range(K):
            prev = jnp.max(jnp.where(s0 < prev, s0, neg), axis=0, keepdims=True)
            sv0 = jnp.where(iota_k == j, prev, sv0)
        sv1 = jnp.zeros((K, LANES), F32)
        rank1 = jnp.full((N_KEYS, LANES), float(K), F32)
        work = s1
        for j in range(K):
            mx = jnp.max(work, axis=0, keepdims=True)
            hit = work == mx
            rank1 = jnp.where(hit, float(j), rank1)
            work = jnp.where(hit, neg, work)
            sv1 = jnp.where(iota_k == j, mx, sv1)

        top = sv0[0:1, :] + sv1[0:1, :]
        front = sv0 + sv1[0:1, :]
        cnt = jnp.zeros((K, LANES), F32)
        zsum = jnp.zeros((1, LANES), F32)
        for _ in range(K):
            fm = jnp.max(front, axis=0, keepdims=True)
            j1 = jnp.min(jnp.where(front == fm, iota_kf, float(K)), axis=0, keepdims=True)
            sel = iota_kf == j1
            zsum = zsum + jnp.exp(fm - top)
            nxt = jnp.sum(jnp.where(sel, cnt, 0.0), axis=0, keepdims=True) + 1.0
            cnt = jnp.where(sel, cnt + 1.0, cnt)
            sv1n = jnp.sum(jnp.where(iota_kf == nxt, sv1, 0.0), axis=0, keepdims=True)
            sv1n = jnp.where(nxt >= float(K), neg, sv1n)
            sv0s = jnp.sum(jnp.where(sel, sv0, 0.0), axis=0, keepdims=True)
            front = jnp.where(sel, sv0s + sv1n, front)

        cntp = jnp.zeros((N_KEYS, LANES), F32)
        for j in range(K):
            cntp = cntp + jnp.where(s0 == sv0[j:j + 1, :], cnt[j:j + 1, :], 0.0)

        rank_ref[h, ch] = rank1.astype(BF16)
        p1_ref[h, ch] = jnp.exp(s1 - sv1[0:1, :]).astype(BF16)
        cnt_ref[h, ch] = cntp
        p0_ref[h, ch] = jnp.exp(s0 - sv0[0:1, :]) / zsum
        return carry

    lax.fori_loop(0, PEER_HEADS * nch, body, 0)


def _peer_select(x1, g2, wq, sk, tq):
    t = x1.shape[0]
    nch = tq // LANES
    meta = lambda dt: jax.ShapeDtypeStruct((PEER_HEADS, t // LANES, N_KEYS, LANES), dt)
    mspec = pl.BlockSpec((PEER_HEADS, nch, N_KEYS, LANES), lambda i: (0, i, 0, 0))
    return pl.pallas_call(
        functools.partial(_peer_select_kernel, tq=tq),
        grid=(t // tq,),
        in_specs=[
            pl.BlockSpec((tq, D_MODEL), lambda i: (i, 0)),
            pl.BlockSpec((1, D_MODEL), lambda i: (0, 0)),
            pl.BlockSpec((D_MODEL, PEER_HEADS * PEER_DQ), lambda i: (0, 0)),
            pl.BlockSpec((2, N_KEYS, PEER_DQ // 2), lambda i: (0, 0, 0)),
        ],
        out_specs=[pl.BlockSpec((tq, D_MODEL), lambda i: (i, 0)), mspec, mspec, mspec, mspec],
        out_shape=[jax.ShapeDtypeStruct((t, D_MODEL), BF16), meta(BF16), meta(BF16), meta(F32), meta(F32)],
        scratch_shapes=[pltpu.VMEM((2 * PEER_HEADS, nch, N_KEYS, LANES), F32)],
        compiler_params=_params(("arbitrary",)),
        name="peer_select",
    )(x1, g2, wq, sk)


def _peer_dense_kernel(x_ref, h2_ref, rank_ref, p1_ref, cnt_ref, p0_ref, u_ref, vt_ref, fg_ref,
                       out_ref, acc_scr, g_scr, *, tb, eb, nblk_e, final_norm):
    j = pl.program_id(1)
    nch = tb // LANES

    @pl.when(j == 0)
    def _():
        acc_scr[...] = jnp.zeros_like(acc_scr)

    ht = lax.dot_general(u_ref[...], h2_ref[...], NT_DIMS, preferred_element_type=F32)
    for k in range(eb // N_KEYS):
        i1 = j * (eb // N_KEYS) + k
        ks = slice(k * N_KEYS, (k + 1) * N_KEYS)
        for ch in range(nch):
            cs = slice(ch * LANES, (ch + 1) * LANES)
            w = jnp.zeros((N_KEYS, LANES), BF16)
            for h in range(PEER_HEADS):
                cnt_row = jnp.broadcast_to(cnt_ref[h, ch, pl.ds(i1, 1), :], (N_KEYS, LANES)).astype(BF16)
                p0_row = jnp.broadcast_to(p0_ref[h, ch, pl.ds(i1, 1), :], (N_KEYS, LANES)).astype(BF16)
                w = w + jnp.where(rank_ref[h, ch] < cnt_row, p1_ref[h, ch] * p0_row, jnp.zeros_like(w))
            z = ht[ks, cs]
            act = 0.5 * z * (1.0 + lax.erf(z * 0.7071067811865476))
            g_scr[ks, cs] = w * act.astype(BF16)
    acc_scr[...] += jnp.dot(vt_ref[...], g_scr[...], preferred_element_type=F32)

    @pl.when(j == nblk_e - 1)
    def _():
        y = x_ref[...] + acc_scr[...].T
        if final_norm:
            y = _rms(y, fg_ref[...])
        out_ref[...] = y


def _peer_dense(x1, h2, rank, p1, cnt, p0, u_bf, vt_bf, fg, tb, eb, final_norm):
    t = x1.shape[0]
    nch = tb // LANES
    nblk_e = N_EXPERTS // eb
    mspec = pl.BlockSpec((PEER_HEADS, nch, N_KEYS, LANES), lambda i, j: (0, i, 0, 0))
    return pl.pallas_call(
        functools.partial(_peer_dense_kernel, tb=tb, eb=eb, nblk_e=nblk_e, final_norm=final_norm),
        grid=(t // tb, nblk_e),
        in_specs=[
            pl.BlockSpec((tb, D_MODEL), lambda i, j: (i, 0)),
            pl.BlockSpec((tb, D_MODEL), lambda i, j: (i, 0)),
            mspec, mspec, mspec, mspec,
            pl.BlockSpec((eb, D_MODEL), lambda i, j: (j, 0)),
            pl.BlockSpec((D_MODEL, eb), lambda i, j: (0, j)),
            pl.BlockSpec((1, D_MODEL), lambda i, j: (0, 0)),
        ],
        out_specs=pl.BlockSpec((tb, D_MODEL), lambda i, j: (i, 0)),
        out_shape=jax.ShapeDtypeStruct((t, D_MODEL), F32),
        scratch_shapes=[pltpu.VMEM((D_MODEL, tb), F32), pltpu.VMEM((eb, tb), BF16)],
        compiler_params=_params(("arbitrary", "arbitrary")),
        name="peer_dense",
    )(x1, h2, rank, p1, cnt, p0, u_bf, vt_bf, fg)


def _layer_weights(li, w_in, gla_w2, gla_b2, gla_norm_g, conv_w, conv_b, ml_i_b, ml_f_b, ml_norm_g,
                   w_out, peer_wq, peer_subkeys, peer_u, peer_v):
    w = w_in[li]
    o = 0
    segs = {}
    for name, width in (("aq", GLA_QK), ("ak", GLA_QK), ("av", D_MODEL), ("ag", D_MODEL),
                        ("alr", GLA_GATE_RANK), ("bq", ML_W), ("bk", ML_W), ("bv", ML_W),
                        ("bi", ML_HEADS), ("bf", ML_HEADS), ("bo", ML_W),
                        ("ga", D_MODEL), ("gb", D_MODEL)):
        segs[name] = w[:, o:o + width]
        o += width
    w_big = jnp.concatenate([segs[n] for n in ("aq", "ak", "av", "ag", "bq", "bk", "bv", "bo", "ga", "gb")],
                            axis=1).astype(BF16)
    pad = lambda a: jnp.pad(a, ((0, 0), (0, LANES - a.shape[1])))
    w_small = jnp.concatenate([pad(segs["alr"]), pad(segs["bi"]), pad(segs["bf"])], axis=1).astype(BF16)
    w_gt = jnp.concatenate([segs["bi"], segs["bf"]], axis=1).T.astype(BF16)
    gate_b = jnp.concatenate([ml_i_b[li], ml_f_b[li]])
    return dict(
        w_big=w_big, w_small=w_small, w_gt=w_gt,
        w2p=jnp.pad(gla_w2[li], ((0, LANES - GLA_GATE_RANK), (0, 0))),
        b2=gla_b2[li][None, :], gng=gla_norm_g[li][None, :],
        cw=conv_w[li], cb=conv_b[li][None, :],
        gbr=jnp.concatenate([jnp.pad(ml_i_b[li], (0, LANES - ML_HEADS)),
                             jnp.pad(ml_f_b[li], (0, LANES - ML_HEADS))])[None, :],
        gbc=gate_b[:, None], mng=ml_norm_g[li][None, :],
        wout=w_out[li].astype(BF16),
        wq=peer_wq[li].astype(BF16), sk=peer_subkeys[li].astype(BF16),
        u=peer_u[li].astype(BF16), vt=peer_v[li].T.astype(BF16),
    )


def _pick(n, cands):
    for c in cands:
        if n % c == 0:
            return c
    return n


def _peer(x1, g2, lw, fg, final_norm):
    t = x1.shape[0]
    tq = _pick(t, (512, 256, 128))
    tb = _pick(t, (512, 256, 128))
    h2, rank, p1, cnt, p0 = _peer_select(x1, g2, lw["wq"], lw["sk"], tq)
    return _peer_dense(x1, h2, rank, p1, cnt, p0, lw["u"], lw["vt"], fg, tb, 512, final_norm)


def kernel(x_prompt, x_sample, state_gla, state_mlstm_c, state_mlstm_n, state_mlstm_m, state_conv,
           norm1_g, w_in, gla_w2, gla_b2, gla_norm_g, conv_w, conv_b, ml_i_b, ml_f_b, ml_norm_g,
           w_out, norm2_g, peer_wq, peer_subkeys, peer_u, peer_v, final_g):
    depth = w_in.shape[0]
    bsz, seq, _ = x_prompt.shape
    nb = x_sample.shape[0]
    assert seq % CHUNK == 0 and x_sample.shape[1] == 1 and nb % LANES == 0
    tp = bsz * seq
    tm = _pick(tp, (1024, 512, 256, 128))
    tc = _pick(seq, (256, 128, 64))
    fg = final_g[None, :]

    xp = x_prompt.reshape(tp, D_MODEL)
    xs = x_sample.reshape(nb, D_MODEL)
    p_out = [[] for _ in range(5)]
    s_out = [[] for _ in range(5)]
    for li in range(depth):
        lw = _layer_weights(li, w_in, gla_w2, gla_b2, gla_norm_g, conv_w, conv_b, ml_i_b, ml_f_b,
                            ml_norm_g, w_out, peer_wq, peer_subkeys, peer_u, peer_v)
        g1 = norm1_g[li][None, :]
        g2 = norm2_g[li][None, :]
        last = li == depth - 1

        big, sm, gt = _inproj(xp, g1, lw["w_big"], lw["w_small"], lw["w_gt"], tm)
        gtc = gt.reshape(8, tp // CHUNK, CHUNK).transpose(1, 0, 2)
        x1, gla, c, n, m, conv = _mixer_prompt(xp, big, sm, gtc, lw, bsz, seq, tc)
        xp = _peer(x1, g2, lw, fg, last)
        for lst, val in zip(p_out, (gla, c, n, m[:, 0, :ML_HEADS], conv)):
            lst.append(val)

        big, sm, _ = _inproj(xs, g1, lw["w_big"], lw["w_small"], lw["w_gt"], nb)
        x1, gla, c, n, m, conv = _mixer_sample(
            xs, big, sm, state_gla[li], state_mlstm_c[li],
            state_mlstm_n[li].reshape(nb, ML_W),
            jnp.pad(state_mlstm_m[li], ((0, 0), (0, LANES - ML_HEADS))),
            state_conv[li].reshape(nb, (CONV_W - 1) * 2 * ML_W), lw)
        xs = _peer(x1, g2, lw, fg, last)
        for lst, val in zip(s_out, (gla, c, n.reshape(nb, ML_HEADS, ML_DH), m[:, :ML_HEADS],
                                    conv.reshape(nb, CONV_W - 1, 2 * ML_W))):
            lst.append(val)

    y_prompt = xp.reshape(bsz, seq, D_MODEL)
    y_sample = xs.reshape(nb, 1, D_MODEL)
    return (y_prompt, y_sample, *[jnp.stack(v) for v in p_out], *[jnp.stack(v) for v in s_out])
```

```python
import functools

import jax
import jax.numpy as jnp
from jax import lax
from jax.experimental import pallas as pl
from jax.experimental.pallas import tpu as pltpu

F32 = jnp.float32
BF16 = jnp.bfloat16
HIGHEST = lax.Precision.HIGHEST

D_MODEL = 1024
GLA_HEADS = 4
GLA_DK = 128
GLA_DV = 256
GLA_GATE_RANK = 16
GLA_TAU = 16.0
GLA_QK = GLA_HEADS * GLA_DK
ML_HEADS = 4
ML_DH = 256
ML_W = ML_HEADS * ML_DH
CONV_W = 4
CHUNK = 64
PEER_HEADS = 8
PEER_DQ = 256
N_KEYS = 128
N_EXPERTS = N_KEYS * N_KEYS
PEER_TOPK = 16
EPS = 1e-6

LANES = 128
BIG_COLS = 9 * D_MODEL
WSEG_BLOCKS = 3
SMALL_COLS = 3 * LANES
VMEM_LIMIT_BYTES = 56 * 1024 * 1024

NT_DIMS = (((1,), (1,)), ((), ()))
TN_DIMS = (((0,), (0,)), ((), ()))


def _params(sem):
    return pltpu.CompilerParams(dimension_semantics=sem, vmem_limit_bytes=VMEM_LIMIT_BYTES)


def _log_sigmoid(x):
    return jnp.minimum(x, 0.0) - jnp.log1p(jnp.exp(-jnp.abs(x)))


def _sigmoid(x):
    return 1.0 / (1.0 + jnp.exp(-x))


def _silu(x):
    return x * _sigmoid(x)


def _rms(x, g):
    return x * lax.rsqrt(jnp.mean(x * x, axis=-1, keepdims=True) + EPS) * g


def _lane_col(a, j):
    lane = lax.broadcasted_iota(jnp.int32, a.shape, 1)
    return jnp.sum(jnp.where(lane == j, a, 0.0), axis=1, keepdims=True)


def _inproj_kernel(x_ref, g_ref, wa_ref, wb_ref, wc_ref, wsm_ref, wgt_ref, big_ref, sm_ref, gt_ref, h_scr):
    j = pl.program_id(1)

    @pl.when(j == 0)
    def _():
        hb = _rms(x_ref[...], g_ref[...]).astype(BF16)
        h_scr[...] = hb
        sm_ref[...] = jnp.dot(hb, wsm_ref[...], preferred_element_type=F32)
        gt_ref[...] = lax.dot_general(wgt_ref[...], hb, NT_DIMS, preferred_element_type=F32)

    for s, w_ref in enumerate((wa_ref, wb_ref, wc_ref)):
        @pl.when((j >= s * WSEG_BLOCKS) & (j < (s + 1) * WSEG_BLOCKS))
        def _(w_ref=w_ref):
            big_ref[...] = jnp.dot(h_scr[...], w_ref[...], preferred_element_type=F32).astype(BF16)


def _inproj(x, g, w_segs, w_small, w_gt, tm):
    t = x.shape[0]
    tn = D_MODEL

    def wspec(s):
        return pl.BlockSpec((D_MODEL, tn), lambda i, j: (0, jnp.clip(j - s * WSEG_BLOCKS, 0, WSEG_BLOCKS - 1)))

    return pl.pallas_call(
        _inproj_kernel,
        grid=(t // tm, BIG_COLS // tn),
        in_specs=[
            pl.BlockSpec((tm, D_MODEL), lambda i, j: (i, 0)),
            pl.BlockSpec((1, D_MODEL), lambda i, j: (0, 0)),
            wspec(0), wspec(1), wspec(2),
            pl.BlockSpec((D_MODEL, SMALL_COLS), lambda i, j: (0, 0)),
            pl.BlockSpec((8, D_MODEL), lambda i, j: (0, 0)),
        ],
        out_specs=[
            pl.BlockSpec((tm, tn), lambda i, j: (i, j)),
            pl.BlockSpec((tm, SMALL_COLS), lambda i, j: (i, 0)),
            pl.BlockSpec((8, tm), lambda i, j: (0, i)),
        ],
        out_shape=[
            jax.ShapeDtypeStruct((t, BIG_COLS), BF16),
            jax.ShapeDtypeStruct((t, SMALL_COLS), F32),
            jax.ShapeDtypeStruct((8, t), F32),
        ],
        scratch_shapes=[pltpu.VMEM((tm, D_MODEL), BF16)],
        compiler_params=_params(("arbitrary", "arbitrary")),
        name="inproj",
    )(x, g, *w_segs, w_small, w_gt)


def _headnorm(o, g_row):
    return o * lax.rsqrt(jnp.mean(o * o, axis=-1, keepdims=True) + EPS) * g_row


def _mixer_prompt_kernel(x_ref, aq_ref, ak_ref, av_ref, ag_ref, bq_ref, bk_ref, bv_ref, bo_ref,
                         ga_ref, gb_ref, sm_ref, gt_ref, w2_ref, b2_ref, gng_ref, cw_ref, cb_ref,
                         gbr_ref, gbc_ref, mng_ref, wout_ref,
                         x1_ref, gla_ref, c_ref, n_ref, m_ref, conv_ref,
                         st_scr, xp_scr, qk_scr, mix_scr, *, tc, nblk):
    i = pl.program_id(1)
    L = CHUNK

    @pl.when(i == 0)
    def _init():
        st_scr[...] = jnp.zeros_like(st_scr)
        c_ref[...] = jnp.zeros_like(c_ref)
        n_ref[...] = jnp.zeros_like(n_ref)
        m_ref[...] = jnp.zeros_like(m_ref)
        xp_scr[0:8, :] = jnp.zeros((8, 2 * ML_W), F32)

    xp_scr[8:8 + tc, 0:ML_W] = bq_ref[...].astype(F32)
    xp_scr[8:8 + tc, ML_W:2 * ML_W] = bk_ref[...].astype(F32)
    y = cb_ref[...]
    for j in range(CONV_W):
        y = y + cw_ref[j:j + 1, :] * xp_scr[5 + j:5 + j + tc, :]
    qk = _silu(y)
    qk_scr[:, 0:ML_W] = qk[:, 0:ML_W].astype(BF16)
    qk_scr[:, ML_W:2 * ML_W] = (qk[:, ML_W:2 * ML_W] * (ML_DH ** -0.5)).astype(BF16)
    last3 = xp_scr[tc + 5:tc + 8, :]
    xp_scr[5:8, :] = last3

    @pl.when(i == nblk - 1)
    def _():
        conv_ref[0] = last3

    row = lax.broadcasted_iota(jnp.int32, (L, L), 0)
    col = lax.broadcasted_iota(jnp.int32, (L, L), 1)
    causal = col <= row
    tril = causal.astype(F32)
    triu = (row <= col).astype(F32)
    lane_row = lax.broadcasted_iota(jnp.int32, (1, LANES), 1)

    def chunk(c, carry):
        r0 = pl.multiple_of(c * L, L)
        rows = pl.ds(r0, L)
        sm = sm_ref[rows, :]
        alr = sm[:, 0:LANES]
        z = jnp.dot(alr, w2_ref[...], precision=HIGHEST, preferred_element_type=F32) + b2_ref[...]
        log_a = _log_sigmoid(z) * (1.0 / GLA_TAU)
        bc = jnp.dot(tril, log_a, precision=HIGHEST, preferred_element_type=F32)

        li_c_all = sm[:, LANES:2 * LANES] + gbr_ref[:, 0:LANES]
        lf_c_all = _log_sigmoid(sm[:, 2 * LANES:3 * LANES] + gbr_ref[:, LANES:2 * LANES])
        bcum_c_all = jnp.dot(tril, lf_c_all, precision=HIGHEST, preferred_element_type=F32)
        gtb = gt_ref[c] + gbc_ref[...]
        bcum_r_all = jnp.dot(_log_sigmoid(gtb), triu, precision=HIGHEST, preferred_element_type=F32)

        for h in range(GLA_HEADS):
            sl = slice(h * GLA_DK, (h + 1) * GLA_DK)
            vl = slice(h * GLA_DV, (h + 1) * GLA_DV)
            b = bc[:, sl]
            mid = b[L // 2 - 1:L // 2, :]
            bl = b[L - 1:L, :]
            q = aq_ref[rows, sl].astype(F32) * (GLA_DK ** -0.5)
            k = ak_ref[rows, sl].astype(F32)
            v = av_ref[rows, vl]
            q_in = (q * jnp.exp(b)).astype(BF16)
            q_at = (q * jnp.exp(b - mid)).astype(BF16)
            k_at = (k * jnp.exp(mid - b)).astype(BF16)
            k_out = (k * jnp.exp(bl - b)).astype(BF16)
            att = lax.dot_general(q_at, k_at, NT_DIMS, preferred_element_type=F32)
            att = jnp.where(causal, att, 0.0)
            st = st_scr[h]
            o = lax.dot_general(q_in, st.astype(BF16), NT_DIMS, preferred_element_type=F32)
            o = o + jnp.dot(att.astype(BF16), v, preferred_element_type=F32)
            st_scr[h] = st * jnp.exp(bl) + lax.dot_general(v, k_out, TN_DIMS, preferred_element_type=F32)
            o_a = _headnorm(o, gng_ref[:, vl]) * _silu(ag_ref[rows, vl].astype(F32))

            li_c = _lane_col(li_c_all, h)
            b_c = _lane_col(bcum_c_all, h)
            li_r = gtb[h:h + 1, :]
            b_r = bcum_r_all[ML_HEADS + h:ML_HEADS + h + 1, :]
            m_all = m_ref[0]
            m_prev = jnp.sum(jnp.where(lane_row == h, m_all, 0.0), axis=1, keepdims=True)
            a_int = b_c + m_prev
            dm = jnp.where(causal, b_c - b_r + li_r, -jnp.inf)
            m_t = jnp.maximum(a_int, jnp.max(dm, axis=1, keepdims=True))
            w_int = jnp.exp(a_int - m_t)
            dexp = jnp.exp(dm - m_t)
            qb = qk_scr[rows, vl]
            kb = qk_scr[rows, ML_W + h * ML_DH:ML_W + (h + 1) * ML_DH]
            vb = bv_ref[rows, vl]
            s = lax.dot_general(qb, kb, NT_DIMS, preferred_element_type=F32) * dexp
            cst = c_ref[0, h]
            num = (w_int * jnp.dot(qb, cst.astype(BF16), preferred_element_type=F32)
                   + jnp.dot(s.astype(BF16), vb, preferred_element_type=F32))
            nrow = n_ref[0, h:h + 1, :]
            den = (w_int * jnp.sum(qb.astype(F32) * nrow, axis=1, keepdims=True)
                   + jnp.sum(s, axis=1, keepdims=True))
            hh = num / jnp.maximum(jnp.abs(den), jnp.exp(-m_t))
            b_last = b_c[L - 1:L, :]
            g_c = b_last - b_c + li_c
            m_new = jnp.maximum(b_last + m_prev, jnp.max(g_c, axis=0, keepdims=True))
            f_c = jnp.exp(b_last + m_prev - m_new)
            w_s = jnp.exp(g_c - m_new)
            kw = kb.astype(F32) * w_s
            c_ref[0, h] = f_c * cst + lax.dot_general(kw.astype(BF16), vb, TN_DIMS,
                                                      preferred_element_type=F32)
            n_ref[0, h:h + 1, :] = f_c * nrow + jnp.sum(kw, axis=0, keepdims=True)
            m_ref[0] = jnp.where(lane_row == h, m_new, m_all)
            o_b = _headnorm(hh, mng_ref[:, vl]) * _sigmoid(bo_ref[rows, vl].astype(F32))

            mix = (_sigmoid(ga_ref[rows, vl].astype(F32)) * o_a
                   + _sigmoid(gb_ref[rows, vl].astype(F32)) * o_b)
            mix_scr[rows, vl] = mix.astype(BF16)
        return carry

    lax.fori_loop(0, tc // L, chunk, 0)

    x1_ref[...] = x_ref[...] + jnp.dot(mix_scr[...], wout_ref[...], preferred_element_type=F32)

    @pl.when(i == nblk - 1)
    def _fin():
        for h in range(GLA_HEADS):
            gla_ref[0, h] = st_scr[h].T


def _mixer_prompt(x, big, sm, gtc, lw, bsz, seq, tc):
    nblk = seq // tc
    rb = lambda b, i: (b * nblk + i, 0)

    def seg(width, idx):
        return pl.BlockSpec((tc, width), lambda b, i: (b * nblk + i, idx))

    const = lambda shape: pl.BlockSpec(shape, lambda b, i: tuple(0 for _ in shape))
    in_specs = [
        pl.BlockSpec((tc, D_MODEL), rb),
        seg(GLA_QK, 0), seg(GLA_QK, 1),
        seg(D_MODEL, 1), seg(D_MODEL, 2),
        seg(D_MODEL, 3), seg(D_MODEL, 4), seg(D_MODEL, 5),
        seg(D_MODEL, 6), seg(D_MODEL, 7), seg(D_MODEL, 8),
        pl.BlockSpec((tc, SMALL_COLS), rb),
        pl.BlockSpec((tc // CHUNK, 8, CHUNK), lambda b, i: (b * nblk + i, 0, 0)),
        const((LANES, GLA_QK)), const((1, GLA_QK)), const((1, D_MODEL)),
        const((CONV_W, 2 * ML_W)), const((1, 2 * ML_W)),
        const((1, 2 * LANES)), const((8, 1)), const((1, D_MODEL)),
        const((D_MODEL, D_MODEL)),
    ]
    out_specs = [
        pl.BlockSpec((tc, D_MODEL), rb),
        pl.BlockSpec((1, GLA_HEADS, GLA_DK, GLA_DV), lambda b, i: (b, 0, 0, 0)),
        pl.BlockSpec((1, ML_HEADS, ML_DH, ML_DH), lambda b, i: (b, 0, 0, 0)),
        pl.BlockSpec((1, ML_HEADS, ML_DH), lambda b, i: (b, 0, 0)),
        pl.BlockSpec((1, 1, LANES), lambda b, i: (b, 0, 0)),
        pl.BlockSpec((1, CONV_W - 1, 2 * ML_W), lambda b, i: (b, 0, 0)),
    ]
    out_shape = [
        jax.ShapeDtypeStruct((bsz * seq, D_MODEL), F32),
        jax.ShapeDtypeStruct((bsz, GLA_HEADS, GLA_DK, GLA_DV), F32),
        jax.ShapeDtypeStruct((bsz, ML_HEADS, ML_DH, ML_DH), F32),
        jax.ShapeDtypeStruct((bsz, ML_HEADS, ML_DH), F32),
        jax.ShapeDtypeStruct((bsz, 1, LANES), F32),
        jax.ShapeDtypeStruct((bsz, CONV_W - 1, 2 * ML_W), F32),
    ]
    scratch = [
        pltpu.VMEM((GLA_HEADS, GLA_DV, GLA_DK), F32),
        pltpu.VMEM((tc + 8, 2 * ML_W), F32),
        pltpu.VMEM((tc, 2 * ML_W), BF16),
        pltpu.VMEM((tc, D_MODEL), BF16),
    ]
    return pl.pallas_call(
        functools.partial(_mixer_prompt_kernel, tc=tc, nblk=nblk),
        grid=(bsz, nblk),
        in_specs=in_specs, out_specs=out_specs, out_shape=out_shape,
        scratch_shapes=scratch,
        compiler_params=_params(("arbitrary", "arbitrary")),
        name="mixer_prompt",
    )(x, big, big, big, big, big, big, big, big, big, big, sm, gtc,
      lw["w2p"], lw["b2"], lw["gng"], lw["cw"], lw["cb"], lw["gbr"], lw["gbc"], lw["mng"], lw["wout"])


def _col_bcast(row):
    return jnp.broadcast_to(row, (LANES, LANES)).T


def _mixer_sample_kernel(*refs, nb, has_prev):
    n_in = 18 if has_prev else 16
    (x_ref, big_ref, sm_ref, gla_in, c_in, n_ref, m_ref, conv_ref,
     w2_ref, b2_ref, gng_ref, cw_ref, cb_ref, gbr_ref, mng_ref, wout_ref) = refs[:16]
    x1_ref, gla_out, c_out, n_o, m_o, conv_o = refs[n_in:n_in + 6]
    a_s, qa_s, ka_s, va_s, kw_s, qb_s, vb_s, fc_s, den_s, oa_s, num_s = refs[n_in + 6:]
    gla_ref, c_ref, gla_o, c_o = gla_in.at[0], c_in.at[0], gla_out.at[0], c_out.at[0]
    b = pl.program_id(0)
    seg = lambda idx: slice(idx * D_MODEL, (idx + 1) * D_MODEL)
    lane = lax.broadcasted_iota(jnp.int32, (nb, LANES), 1)

    @pl.when(b == 0)
    def _prep():
        sm = sm_ref[...]
        z = jnp.dot(sm[:, 0:LANES], w2_ref[...], precision=HIGHEST, preferred_element_type=F32) + b2_ref[...]
        a_s[...] = jnp.exp(_log_sigmoid(z) * (1.0 / GLA_TAU))
        qa_s[...] = big_ref[:, 0:GLA_QK].astype(F32) * (GLA_DK ** -0.5)
        ka_s[...] = big_ref[:, GLA_QK:2 * GLA_QK].astype(F32)
        va_s[...] = big_ref[:, seg(1)].astype(F32)
        vb_s[...] = big_ref[:, seg(5)].astype(F32)
        xq = big_ref[:, seg(3)].astype(F32)
        xk = big_ref[:, seg(4)].astype(F32)
        w = 2 * ML_W
        y = cb_ref[...]
        for j in range(CONV_W - 1):
            y = y + cw_ref[j:j + 1, :] * conv_ref[:, j * w:(j + 1) * w]
        y = y + cw_ref[CONV_W - 1:CONV_W, :] * jnp.concatenate([xq, xk], axis=1)
        qk = _silu(y)
        qb = qk[:, 0:ML_W]
        kb = qk[:, ML_W:w] * (ML_DH ** -0.5)
        qb_s[...] = qb
        conv_o[:, 0:2 * w] = conv_ref[:, w:3 * w]
        conv_o[:, 2 * w:2 * w + ML_W] = xq
        conv_o[:, 2 * w + ML_W:3 * w] = xk

        li = sm[:, LANES:2 * LANES] + gbr_ref[:, 0:LANES]
        lf = _log_sigmoid(sm[:, 2 * LANES:3 * LANES] + gbr_ref[:, LANES:2 * LANES])
        m_old = m_ref[...]
        m_new = jnp.maximum(lf + m_old, li)
        f_c = jnp.exp(lf + m_old - m_new)
        w_s = jnp.exp(li - m_new)
        m_o[...] = m_new
        fc_s[...] = f_c
        den = jnp.zeros((nb, LANES), F32)
        for h in range(ML_HEADS):
            vl = slice(h * ML_DH, (h + 1) * ML_DH)
            kw = kb[:, vl] * _lane_col(w_s, h)
            kw_s[:, vl] = kw
            n_new = _lane_col(f_c, h) * n_ref[:, vl] + kw
            n_o[:, vl] = n_new
            den = jnp.where(lane == h, jnp.sum(qb[:, vl] * n_new, axis=1, keepdims=True), den)
        den_s[...] = den
        oa_s[...] = jnp.zeros_like(oa_s)
        num_s[...] = jnp.zeros_like(num_s)

    tile = pl.ds(pl.multiple_of(lax.shift_right_logical(b, 3) * 8, 8), 8)
    sub = jnp.bitwise_and(b, 7)

    def row_get(ref, cols):
        t = ref[tile, cols]
        pick = lax.broadcasted_iota(jnp.int32, t.shape, 0) == sub
        return jnp.sum(jnp.where(pick, t, 0.0), axis=0, keepdims=True)

    def row_set(ref, cols, val):
        t = ref[tile, cols]
        pick = lax.broadcasted_iota(jnp.int32, t.shape, 0) == sub
        ref[tile, cols] = jnp.where(pick, val, t)

    lane_row = lax.broadcasted_iota(jnp.int32, (1, LANES), 1)
    fc_row = row_get(fc_s, slice(0, LANES))
    for h in range(GLA_HEADS):
        sl = slice(h * GLA_DK, (h + 1) * GLA_DK)
        vl = slice(h * GLA_DV, (h + 1) * GLA_DV)
        a_c = _col_bcast(row_get(a_s, sl))
        k_c = _col_bcast(row_get(ka_s, sl))
        q_c = _col_bcast(row_get(qa_s, sl))
        v_row = row_get(va_s, vl)
        s_old = gla_ref[0, h]
        halves = []
        for p in range(GLA_DV // LANES):
            ls = slice(p * LANES, (p + 1) * LANES)
            s_new = s_old[:, ls] * a_c + k_c * v_row[:, ls]
            gla_o[0, h, :, ls] = s_new
            halves.append(jnp.sum(q_c * s_new, axis=0, keepdims=True))
        row_set(oa_s, vl, jnp.concatenate(halves, axis=1))

        f_c = jnp.sum(jnp.where(lane_row == h, fc_row, 0.0), axis=1, keepdims=True)
        vb_row = row_get(vb_s, vl)
        acc = [jnp.zeros((1, LANES), F32) for _ in range(ML_DH // LANES)]
        for r in range(ML_DH // LANES):
            rs = slice(r * LANES, (r + 1) * LANES)
            ks = slice(h * ML_DH + r * LANES, h * ML_DH + (r + 1) * LANES)
            kw_c = _col_bcast(row_get(kw_s, ks))
            qb_c = _col_bcast(row_get(qb_s, ks))
            for p in range(ML_DH // LANES):
                ls = slice(p * LANES, (p + 1) * LANES)
                c_new = f_c * c_ref[0, h, rs, ls] + kw_c * vb_row[:, ls]
                c_o[0, h, rs, ls] = c_new
                acc[p] = acc[p] + jnp.sum(qb_c * c_new, axis=0, keepdims=True)
        row_set(num_s, vl, jnp.concatenate(acc, axis=1))

    @pl.when(b == nb - 1)
    def _post():
        m_new = m_o[...]
        den = den_s[...]
        for h in range(GLA_HEADS):
            vl = slice(h * GLA_DV, (h + 1) * GLA_DV)
            o_a = _headnorm(oa_s[:, vl], gng_ref[:, vl]) * _silu(big_ref[:, D_MODEL * 2 + h * GLA_DV:
                                                                       D_MODEL * 2 + (h + 1) * GLA_DV].astype(F32))
            dn = jnp.maximum(jnp.abs(_lane_col(den, h)), jnp.exp(-_lane_col(m_new, h)))
            hh = num_s[:, vl] / dn
            o_b = _headnorm(hh, mng_ref[:, vl]) * _sigmoid(
                big_ref[:, 6 * D_MODEL + h * ML_DH:6 * D_MODEL + (h + 1) * ML_DH].astype(F32))
            ga = big_ref[:, 7 * D_MODEL + h * ML_DH:7 * D_MODEL + (h + 1) * ML_DH].astype(F32)
            gb = big_ref[:, 8 * D_MODEL + h * ML_DH:8 * D_MODEL + (h + 1) * ML_DH].astype(F32)
            oa_s[:, vl] = _sigmoid(ga) * o_a + _sigmoid(gb) * o_b
        x1_ref[...] = x_ref[...] + jnp.dot(oa_s[...].astype(BF16), wout_ref[...], preferred_element_type=F32)


def _mixer_sample(x, big, sm, st_gla, st_c, st_n, st_m, st_conv, lw, li, prev_gla, prev_c):
    nb = x.shape[0]
    has_prev = prev_gla is not None
    full = lambda shape: pl.BlockSpec(shape, lambda b: tuple(0 for _ in shape))
    gla_spec = pl.BlockSpec((1, 1, GLA_HEADS, GLA_DK, GLA_DV), lambda b: (li, b, 0, 0, 0))
    c_spec = pl.BlockSpec((1, 1, ML_HEADS, ML_DH, ML_DH), lambda b: (li, b, 0, 0, 0))
    in_specs = [
        full((nb, D_MODEL)), full((nb, BIG_COLS)), full((nb, SMALL_COLS)),
        gla_spec, c_spec,
        full((nb, ML_W)), full((nb, LANES)), full((nb, 3 * 2 * ML_W)),
        full((LANES, GLA_QK)), full((1, GLA_QK)), full((1, D_MODEL)),
        full((CONV_W, 2 * ML_W)), full((1, 2 * ML_W)), full((1, 2 * LANES)), full((1, D_MODEL)),
        full((D_MODEL, D_MODEL)),
    ]
    operands = [x, big, sm, st_gla, st_c, st_n, st_m, st_conv,
                lw["w2p"], lw["b2"], lw["gng"], lw["cw"], lw["cb"], lw["gbr"], lw["mng"], lw["wout"]]
    aliases = {}
    if has_prev:
        in_specs += [pl.BlockSpec(memory_space=pl.ANY), pl.BlockSpec(memory_space=pl.ANY)]
        operands += [prev_gla, prev_c]
        aliases = {16: 1, 17: 2}
    out_specs = [
        full((nb, D_MODEL)),
        gla_spec, c_spec,
        full((nb, ML_W)), full((nb, LANES)), full((nb, 3 * 2 * ML_W)),
    ]
    out_shape = [
        jax.ShapeDtypeStruct((nb, D_MODEL), F32),
        jax.ShapeDtypeStruct(st_gla.shape, F32),
        jax.ShapeDtypeStruct(st_c.shape, F32),
        jax.ShapeDtypeStruct((nb, ML_W), F32),
        jax.ShapeDtypeStruct((nb, LANES), F32),
        jax.ShapeDtypeStruct((nb, 3 * 2 * ML_W), F32),
    ]
    vm = lambda cols: pltpu.VMEM((nb, cols), F32)
    scratch = [vm(GLA_QK), vm(GLA_QK), vm(GLA_QK), vm(D_MODEL), vm(ML_W), vm(ML_W), vm(ML_W),
               vm(LANES), vm(LANES), vm(D_MODEL), vm(ML_W)]
    return pl.pallas_call(
        functools.partial(_mixer_sample_kernel, nb=nb, has_prev=has_prev),
        grid=(nb,),
        in_specs=in_specs, out_specs=out_specs, out_shape=out_shape,
        scratch_shapes=scratch,
        input_output_aliases=aliases,
        compiler_params=_params(("arbitrary",)),
        name="mixer_sample",
    )(*operands)


def _peer_select_kernel(x_ref, g_ref, wq_ref, sk_ref, h2_ref, rank_ref, p1_ref, cnt_ref, p0_ref,
                        sc_scr, *, tq, cw):
    nch = tq // cw
    hb = _rms(x_ref[...], g_ref[...]).astype(BF16)
    h2_ref[...] = hb
    q = jnp.dot(hb, wq_ref[...], preferred_element_type=F32).astype(BF16)
    for hc in range(2 * PEER_HEADS):
        for ch in range(nch):
            qs = q[ch * cw:(ch + 1) * cw, hc * LANES:(hc + 1) * LANES]
            sc_scr[hc, ch] = lax.dot_general(sk_ref[hc % 2], qs, NT_DIMS, preferred_element_type=F32)

    K = PEER_TOPK
    neg = -jnp.inf
    iota_k = lax.broadcasted_iota(jnp.int32, (K, cw), 0)
    iota_kf = iota_k.astype(F32)

    def body(it, carry):
        h = it // nch
        ch = it % nch
        s0 = sc_scr[2 * h, ch]
        s1 = sc_scr[2 * h + 1, ch]

        sv0 = jnp.zeros((K, cw), F32)
        prev = jnp.full((1, cw), jnp.inf, F32)
        for j in range(K):
            prev = jnp.max(jnp.where(s0 < prev, s0, neg), axis=0, keepdims=True)
            sv0 = jnp.where(iota_k == j, prev, sv0)
        sv1 = jnp.zeros((K, cw), F32)
        rank1 = jnp.full((N_KEYS, cw), float(K), F32)
        work = s1
        for j in range(K):
            mx = jnp.max(work, axis=0, keepdims=True)
            hit = work == mx
            rank1 = jnp.where(hit, float(j), rank1)
            work = jnp.where(hit, neg, work)
            sv1 = jnp.where(iota_k == j, mx, sv1)

        top = sv0[0:1, :] + sv1[0:1, :]
        front = sv0 + sv1[0:1, :]
        cnt = jnp.zeros((K, cw), F32)
        zsum = jnp.zeros((1, cw), F32)
        for _ in range(K):
            fm = jnp.max(front, axis=0, keepdims=True)
            j1 = jnp.min(jnp.where(front == fm, iota_kf, float(K)), axis=0, keepdims=True)
            sel = iota_kf == j1
            zsum = zsum + jnp.exp(fm - top)
            nxt = jnp.sum(jnp.where(sel, cnt, 0.0), axis=0, keepdims=True) + 1.0
            cnt = jnp.where(sel, cnt + 1.0, cnt)
            sv1n = jnp.sum(jnp.where(iota_kf == nxt, sv1, 0.0), axis=0, keepdims=True)
            sv1n = jnp.where(nxt >= float(K), neg, sv1n)
            sv0s = jnp.sum(jnp.where(sel, sv0, 0.0), axis=0, keepdims=True)
            front = jnp.where(sel, sv0s + sv1n, front)

        cntp = jnp.zeros((N_KEYS, cw), F32)
        for j in range(K):
            cntp = cntp + jnp.where(s0 == sv0[j:j + 1, :], cnt[j:j + 1, :], 0.0)

        rank_ref[h, ch] = rank1.astype(BF16)
        p1_ref[h, ch] = jnp.exp(s1 - sv1[0:1, :]).astype(BF16)
        cnt_ref[h, ch] = cntp
        p0_ref[h, ch] = jnp.exp(s0 - sv0[0:1, :]) / zsum
        return carry

    lax.fori_loop(0, PEER_HEADS * nch, body, 0)


def _peer_select(x1, g2, wq, sk, tq, cw):
    t = x1.shape[0]
    nch = tq // cw
    meta = lambda dt: jax.ShapeDtypeStruct((PEER_HEADS, t // cw, N_KEYS, cw), dt)
    mspec = pl.BlockSpec((PEER_HEADS, nch, N_KEYS, cw), lambda i: (0, i, 0, 0))
    return pl.pallas_call(
        functools.partial(_peer_select_kernel, tq=tq, cw=cw),
        grid=(t // tq,),
        in_specs=[
            pl.BlockSpec((tq, D_MODEL), lambda i: (i, 0)),
            pl.BlockSpec((1, D_MODEL), lambda i: (0, 0)),
            pl.BlockSpec((D_MODEL, PEER_HEADS * PEER_DQ), lambda i: (0, 0)),
            pl.BlockSpec((2, N_KEYS, PEER_DQ // 2), lambda i: (0, 0, 0)),
        ],
        out_specs=[pl.BlockSpec((tq, D_MODEL), lambda i: (i, 0)), mspec, mspec, mspec, mspec],
        out_shape=[jax.ShapeDtypeStruct((t, D_MODEL), BF16), meta(BF16), meta(BF16), meta(F32), meta(F32)],
        scratch_shapes=[pltpu.VMEM((2 * PEER_HEADS, nch, N_KEYS, cw), F32)],
        compiler_params=_params(("arbitrary",)),
        name="peer_select",
    )(x1, g2, wq, sk)


def _peer_dense_kernel(x_ref, h2_ref, rank_ref, p1_ref, cnt_ref, p0_ref, u_ref, vt_ref, fg_ref,
                       out_ref, acc_scr, g_scr, ht_scr, *, tb, eb, ec, cw, nblk_e, final_norm):
    j = pl.program_id(1)

    @pl.when(j == 0)
    def _():
        acc_scr[...] = jnp.zeros_like(acc_scr)

    for grp in range(eb // ec):
        es = slice(grp * ec, (grp + 1) * ec)
        ht_scr[grp] = lax.dot_general(u_ref[es, :], h2_ref[...], NT_DIMS, preferred_element_type=F32)
    for grp in range(eb // ec):
        es = slice(grp * ec, (grp + 1) * ec)
        for k in range(ec // N_KEYS):
            i1 = j * (eb // N_KEYS) + grp * (ec // N_KEYS) + k
            ks = slice(grp * ec + k * N_KEYS, grp * ec + (k + 1) * N_KEYS)
            for ch in range(tb // cw):
                cs = slice(ch * cw, (ch + 1) * cw)
                w = jnp.zeros((N_KEYS, cw), BF16)
                for h in range(PEER_HEADS):
                    cnt_row = jnp.broadcast_to(cnt_ref[h, ch, pl.ds(i1, 1), :], (N_KEYS, cw)).astype(BF16)
                    p0_row = jnp.broadcast_to(p0_ref[h, ch, pl.ds(i1, 1), :], (N_KEYS, cw)).astype(BF16)
                    w = w + jnp.where(rank_ref[h, ch] < cnt_row, p1_ref[h, ch] * p0_row, jnp.zeros_like(w))
                z = ht_scr[grp, k * N_KEYS:(k + 1) * N_KEYS, cs]
                act = 0.5 * z * (1.0 + lax.erf(z * 0.7071067811865476))
                g_scr[ks, cs] = w * act.astype(BF16)
        acc_scr[...] += jnp.dot(vt_ref[:, es], g_scr[es, :], preferred_element_type=F32)

    @pl.when(j == nblk_e - 1)
    def _():
        y = x_ref[...] + acc_scr[...].T
        if final_norm:
            y = _rms(y, fg_ref[...])
        out_ref[...] = y


def _peer_dense(x1, h2, rank, p1, cnt, p0, u_bf, vt_bf, fg, tb, eb, cw, final_norm):
    t = x1.shape[0]
    nch = tb // cw
    nblk_e = N_EXPERTS // eb
    ec = min(eb, 512)
    mspec = pl.BlockSpec((PEER_HEADS, nch, N_KEYS, cw), lambda i, j: (0, i, 0, 0))
    return pl.pallas_call(
        functools.partial(_peer_dense_kernel, tb=tb, eb=eb, ec=ec, cw=cw, nblk_e=nblk_e,
                          final_norm=final_norm),
        grid=(t // tb, nblk_e),
        in_specs=[
            pl.BlockSpec((tb, D_MODEL), lambda i, j: (i, 0)),
            pl.BlockSpec((tb, D_MODEL), lambda i, j: (i, 0)),
            mspec, mspec, mspec, mspec,
            pl.BlockSpec((eb, D_MODEL), lambda i, j: (j, 0)),
            pl.BlockSpec((D_MODEL, eb), lambda i, j: (0, j)),
            pl.BlockSpec((1, D_MODEL), lambda i, j: (0, 0)),
        ],
        out_specs=pl.BlockSpec((tb, D_MODEL), lambda i, j: (i, 0)),
        out_shape=jax.ShapeDtypeStruct((t, D_MODEL), F32),
        scratch_shapes=[pltpu.VMEM((D_MODEL, tb), F32), pltpu.VMEM((eb, tb), BF16),
                        pltpu.VMEM((eb // ec, ec, tb), F32)],
        compiler_params=_params(("arbitrary", "arbitrary")),
        name="peer_dense",
    )(x1, h2, rank, p1, cnt, p0, u_bf, vt_bf, fg)


def _layer_weights(li, w_in, gla_w2, gla_b2, gla_norm_g, conv_w, conv_b, ml_i_b, ml_f_b, ml_norm_g,
                   w_out, peer_wq, peer_subkeys, peer_u, peer_v):
    w = w_in[li]
    o = 0
    segs = {}
    for name, width in (("aq", GLA_QK), ("ak", GLA_QK), ("av", D_MODEL), ("ag", D_MODEL),
                        ("alr", GLA_GATE_RANK), ("bq", ML_W), ("bk", ML_W), ("bv", ML_W),
                        ("bi", ML_HEADS), ("bf", ML_HEADS), ("bo", ML_W),
                        ("ga", D_MODEL), ("gb", D_MODEL)):
        segs[name] = w[:, o:o + width]
        o += width
    seg_w = WSEG_BLOCKS * D_MODEL
    starts = (0, seg_w + GLA_GATE_RANK, 2 * seg_w + GLA_GATE_RANK + 2 * ML_HEADS)
    w_segs = tuple(w[:, s:s + seg_w].astype(BF16) for s in starts)
    pad = lambda a: jnp.pad(a, ((0, 0), (0, LANES - a.shape[1])))
    w_small = jnp.concatenate([pad(segs["alr"]), pad(segs["bi"]), pad(segs["bf"])], axis=1).astype(BF16)
    w_gt = jnp.concatenate([segs["bi"], segs["bf"]], axis=1).T.astype(BF16)
    gate_b = jnp.concatenate([ml_i_b[li], ml_f_b[li]])
    return dict(
        w_segs=w_segs, w_small=w_small, w_gt=w_gt,
        w2p=jnp.pad(gla_w2[li], ((0, LANES - GLA_GATE_RANK), (0, 0))),
        b2=gla_b2[li][None, :], gng=gla_norm_g[li][None, :],
        cw=conv_w[li], cb=conv_b[li][None, :],
        gbr=jnp.concatenate([jnp.pad(ml_i_b[li], (0, LANES - ML_HEADS)),
                             jnp.pad(ml_f_b[li], (0, LANES - ML_HEADS))])[None, :],
        gbc=gate_b[:, None], mng=ml_norm_g[li][None, :],
        wout=w_out[li].astype(BF16),
        wq=peer_wq[li].astype(BF16), sk=peer_subkeys[li].astype(BF16),
        u=peer_u[li].astype(BF16), vt=peer_v[li].T.astype(BF16),
    )


def _pick(n, cands):
    for c in cands:
        if n % c == 0:
            return c
    return n


def _peer(x1, g2, lw, fg, final_norm):
    t = x1.shape[0]
    tq = _pick(t, (512, 256, 128))
    tb = _pick(t, (512, 256, 128))
    cw = _pick(tb, (2 * LANES, LANES))
    h2, rank, p1, cnt, p0 = _peer_select(x1, g2, lw["wq"], lw["sk"], tq, cw)
    return _peer_dense(x1, h2, rank, p1, cnt, p0, lw["u"], lw["vt"], fg, tb, 2048, cw, final_norm)


def kernel(x_prompt, x_sample, state_gla, state_mlstm_c, state_mlstm_n, state_mlstm_m, state_conv,
           norm1_g, w_in, gla_w2, gla_b2, gla_norm_g, conv_w, conv_b, ml_i_b, ml_f_b, ml_norm_g,
           w_out, norm2_g, peer_wq, peer_subkeys, peer_u, peer_v, final_g):
    depth = w_in.shape[0]
    bsz, seq, _ = x_prompt.shape
    nb = x_sample.shape[0]
    assert seq % CHUNK == 0 and x_sample.shape[1] == 1 and nb % LANES == 0
    tp = bsz * seq
    tm = _pick(tp, (1024, 512, 256, 128))
    tc = _pick(seq, (256, 128, 64))
    fg = final_g[None, :]

    xp = x_prompt.reshape(tp, D_MODEL)
    xs = x_sample.reshape(nb, D_MODEL)
    p_out = [[] for _ in range(5)]
    s_out = [[] for _ in range(5)]
    s_gla = s_c = None
    for li in range(depth):
        lw = _layer_weights(li, w_in, gla_w2, gla_b2, gla_norm_g, conv_w, conv_b, ml_i_b, ml_f_b,
                            ml_norm_g, w_out, peer_wq, peer_subkeys, peer_u, peer_v)
        g1 = norm1_g[li][None, :]
        g2 = norm2_g[li][None, :]
        last = li == depth - 1

        big, sm, gt = _inproj(xp, g1, lw["w_segs"], lw["w_small"], lw["w_gt"], tm)
        gtc = gt.reshape(8, tp // CHUNK, CHUNK).transpose(1, 0, 2)
        x1, gla, c, n, m, conv = _mixer_prompt(xp, big, sm, gtc, lw, bsz, seq, tc)
        xp = _peer(x1, g2, lw, fg, last)
        for lst, val in zip(p_out, (gla, c, n, m[:, 0, :ML_HEADS], conv)):
            lst.append(val)

        big, sm, _ = _inproj(xs, g1, lw["w_segs"], lw["w_small"], lw["w_gt"], nb)
        x1, s_gla, s_c, n, m, conv = _mixer_sample(
            xs, big, sm, state_gla, state_mlstm_c,
            state_mlstm_n[li].reshape(nb, ML_W),
            jnp.pad(state_mlstm_m[li], ((0, 0), (0, LANES - ML_HEADS))),
            state_conv[li].reshape(nb, (CONV_W - 1) * 2 * ML_W), lw, li, s_gla, s_c)
        xs = _peer(x1, g2, lw, fg, last)
        for lst, val in zip(s_out[2:], (n.reshape(nb, ML_HEADS, ML_DH), m[:, :ML_HEADS],
                                        conv.reshape(nb, CONV_W - 1, 2 * ML_W))):
            lst.append(val)

    y_prompt = xp.reshape(bsz, seq, D_MODEL)
    y_sample = xs.reshape(nb, 1, D_MODEL)
    return (y_prompt, y_sample, *[jnp.stack(v) for v in p_out],
            s_gla, s_c, *[jnp.stack(v) for v in s_out[2:]])
```

```python
import functools

import jax
import jax.numpy as jnp
from jax import lax
from jax.experimental import pallas as pl
from jax.experimental.pallas import tpu as pltpu

F32 = jnp.float32
BF16 = jnp.bfloat16
HIGHEST = lax.Precision.HIGHEST

D_MODEL = 1024
GLA_HEADS = 4
GLA_DK = 128
GLA_DV = 256
GLA_GATE_RANK = 16
GLA_TAU = 16.0
GLA_QK = GLA_HEADS * GLA_DK
ML_HEADS = 4
ML_DH = 256
ML_W = ML_HEADS * ML_DH
CONV_W = 4
CHUNK = 64
PEER_HEADS = 8
PEER_DQ = 256
N_KEYS = 128
N_EXPERTS = N_KEYS * N_KEYS
PEER_TOPK = 16
EPS = 1e-6

LANES = 128
BIG_COLS = 9 * D_MODEL
WSEG_BLOCKS = 3
SMALL_COLS = 3 * LANES
VMEM_LIMIT_BYTES = 56 * 1024 * 1024

NT_DIMS = (((1,), (1,)), ((), ()))
TN_DIMS = (((0,), (0,)), ((), ()))


def _params(sem, flags=None):
    return pltpu.CompilerParams(dimension_semantics=sem, vmem_limit_bytes=VMEM_LIMIT_BYTES, flags=flags)


def _log_sigmoid(x):
    return jnp.minimum(x, 0.0) - jnp.log1p(jnp.exp(-jnp.abs(x)))


def _sigmoid(x):
    return 1.0 / (1.0 + jnp.exp(-x))


def _silu(x):
    return x * _sigmoid(x)


def _rms(x, g):
    return x * lax.rsqrt(jnp.mean(x * x, axis=-1, keepdims=True) + EPS) * g


def _lane_col(a, j):
    lane = lax.broadcasted_iota(jnp.int32, a.shape, 1)
    return jnp.sum(jnp.where(lane == j, a, 0.0), axis=1, keepdims=True)


def _inproj_kernel(x_ref, g_ref, wa_ref, wb_ref, wc_ref, wsm_ref, wgt_ref, big_ref, sm_ref, gt_ref, h_scr):
    j = pl.program_id(1)

    @pl.when(j == 0)
    def _():
        hb = _rms(x_ref[...], g_ref[...]).astype(BF16)
        h_scr[...] = hb
        sm_ref[...] = jnp.dot(hb, wsm_ref[...], preferred_element_type=F32)
        gt_ref[...] = lax.dot_general(wgt_ref[...], hb, NT_DIMS, preferred_element_type=F32)

    for s, w_ref in enumerate((wa_ref, wb_ref, wc_ref)):
        @pl.when((j >= s * WSEG_BLOCKS) & (j < (s + 1) * WSEG_BLOCKS))
        def _(w_ref=w_ref):
            big_ref[...] = jnp.dot(h_scr[...], w_ref[...], preferred_element_type=F32).astype(BF16)


def _inproj(x, g, w_segs, w_small, w_gt, tm):
    t = x.shape[0]
    tn = D_MODEL

    def wspec(s):
        return pl.BlockSpec((D_MODEL, tn), lambda i, j: (0, jnp.clip(j - s * WSEG_BLOCKS, 0, WSEG_BLOCKS - 1)))

    return pl.pallas_call(
        _inproj_kernel,
        grid=(t // tm, BIG_COLS // tn),
        in_specs=[
            pl.BlockSpec((tm, D_MODEL), lambda i, j: (i, 0)),
            pl.BlockSpec((1, D_MODEL), lambda i, j: (0, 0)),
            wspec(0), wspec(1), wspec(2),
            pl.BlockSpec((D_MODEL, SMALL_COLS), lambda i, j: (0, 0)),
            pl.BlockSpec((8, D_MODEL), lambda i, j: (0, 0)),
        ],
        out_specs=[
            pl.BlockSpec((tm, tn), lambda i, j: (i, j)),
            pl.BlockSpec((tm, SMALL_COLS), lambda i, j: (i, 0)),
            pl.BlockSpec((8, tm), lambda i, j: (0, i)),
        ],
        out_shape=[
            jax.ShapeDtypeStruct((t, BIG_COLS), BF16),
            jax.ShapeDtypeStruct((t, SMALL_COLS), F32),
            jax.ShapeDtypeStruct((8, t), F32),
        ],
        scratch_shapes=[pltpu.VMEM((tm, D_MODEL), BF16)],
        compiler_params=_params(("arbitrary", "arbitrary")),
        name="inproj",
    )(x, g, *w_segs, w_small, w_gt)


def _headnorm(o, g_row):
    return o * lax.rsqrt(jnp.mean(o * o, axis=-1, keepdims=True) + EPS) * g_row


def _mixer_prompt_kernel(x_ref, aq_ref, ak_ref, av_ref, ag_ref, bq_ref, bk_ref, bv_ref, bo_ref,
                         ga_ref, gb_ref, sm_ref, gt_ref, w2_ref, b2_ref, gng_ref, cw_ref, cb_ref,
                         gbr_ref, gbc_ref, mng_ref, wout_ref,
                         x1_ref, gla_ref, c_ref, n_ref, m_ref, conv_ref,
                         st_scr, xp_scr, qk_scr, mix_scr, *, tc, nblk, npar):
    i = pl.program_id(1)
    L = CHUNK

    @pl.when(i == 0)
    def _init():
        st_scr[...] = jnp.zeros_like(st_scr)
        c_ref[...] = jnp.zeros_like(c_ref)
        n_ref[...] = jnp.zeros_like(n_ref)
        m_ref[...] = jnp.zeros_like(m_ref)
        xp_scr[:, 0:8, :] = jnp.zeros((npar, 8, 2 * ML_W), F32)

    for p in range(npar):
        xp_scr[p, 8:8 + tc, 0:ML_W] = bq_ref[p].astype(F32)
        xp_scr[p, 8:8 + tc, ML_W:2 * ML_W] = bk_ref[p].astype(F32)
        y = cb_ref[...]
        for j in range(CONV_W):
            y = y + cw_ref[j:j + 1, :] * xp_scr[p, 5 + j:5 + j + tc, :]
        qk = _silu(y)
        qk_scr[p, :, 0:ML_W] = qk[:, 0:ML_W].astype(BF16)
        qk_scr[p, :, ML_W:2 * ML_W] = (qk[:, ML_W:2 * ML_W] * (ML_DH ** -0.5)).astype(BF16)
        xp_scr[p, 5:8, :] = xp_scr[p, tc + 5:tc + 8, :]

    @pl.when(i == nblk - 1)
    def _():
        conv_ref[...] = xp_scr[:, 5:8, :]

    row = lax.broadcasted_iota(jnp.int32, (L, L), 0)
    col = lax.broadcasted_iota(jnp.int32, (L, L), 1)
    causal = col <= row
    tril = causal.astype(F32)
    triu = (row <= col).astype(F32)
    lane_row = lax.broadcasted_iota(jnp.int32, (1, LANES), 1)

    def gates(p, c, rows):
        sm = sm_ref[p, rows, :]
        z = jnp.dot(sm[:, 0:LANES], w2_ref[...], precision=HIGHEST, preferred_element_type=F32) + b2_ref[...]
        log_a = _log_sigmoid(z) * (1.0 / GLA_TAU)
        bc = jnp.dot(tril, log_a, precision=HIGHEST, preferred_element_type=F32)
        li_c_all = sm[:, LANES:2 * LANES] + gbr_ref[:, 0:LANES]
        lf_c_all = _log_sigmoid(sm[:, 2 * LANES:3 * LANES] + gbr_ref[:, LANES:2 * LANES])
        bcum_c_all = jnp.dot(tril, lf_c_all, precision=HIGHEST, preferred_element_type=F32)
        gtb = gt_ref[p, c] + gbc_ref[...]
        bcum_r_all = jnp.dot(_log_sigmoid(gtb), triu, precision=HIGHEST, preferred_element_type=F32)
        return bc, li_c_all, bcum_c_all, gtb, bcum_r_all

    def head(p, h, rows, shared):
        bc, li_c_all, bcum_c_all, gtb, bcum_r_all = shared
        sl = slice(h * GLA_DK, (h + 1) * GLA_DK)
        vl = slice(h * GLA_DV, (h + 1) * GLA_DV)
        b = bc[:, sl]
        mid = b[L // 2 - 1:L // 2, :]
        bl = b[L - 1:L, :]
        q = aq_ref[p, rows, sl].astype(F32) * (GLA_DK ** -0.5)
        k = ak_ref[p, rows, sl].astype(F32)
        v = av_ref[p, rows, vl]
        q_in = (q * jnp.exp(b)).astype(BF16)
        q_at = (q * jnp.exp(b - mid)).astype(BF16)
        yield
        k_at = (k * jnp.exp(mid - b)).astype(BF16)
        k_out = (k * jnp.exp(bl - b)).astype(BF16)
        yield
        att = lax.dot_general(q_at, k_at, NT_DIMS, preferred_element_type=F32)
        att = jnp.where(causal, att, 0.0)
        st = st_scr[p, h]
        yield
        o = lax.dot_general(q_in, st.astype(BF16), NT_DIMS, preferred_element_type=F32)
        yield
        o = o + jnp.dot(att.astype(BF16), v, preferred_element_type=F32)
        yield
        st_scr[p, h] = st * jnp.exp(bl) + lax.dot_general(v, k_out, TN_DIMS, preferred_element_type=F32)
        yield
        o_a = _headnorm(o, gng_ref[:, vl]) * _silu(ag_ref[p, rows, vl].astype(F32))
        yield

        li_c = _lane_col(li_c_all, h)
        b_c = _lane_col(bcum_c_all, h)
        li_r = gtb[h:h + 1, :]
        b_r = bcum_r_all[ML_HEADS + h:ML_HEADS + h + 1, :]
        m_prev = jnp.sum(jnp.where(lane_row == h, m_ref[p], 0.0), axis=1, keepdims=True)
        a_int = b_c + m_prev
        dm = jnp.where(causal, b_c - b_r + li_r, -jnp.inf)
        m_t = jnp.maximum(a_int, jnp.max(dm, axis=1, keepdims=True))
        yield
        w_int = jnp.exp(a_int - m_t)
        dexp = jnp.exp(dm - m_t)
        qb = qk_scr[p, rows, vl]
        kb = qk_scr[p, rows, ML_W + h * ML_DH:ML_W + (h + 1) * ML_DH]
        vb = bv_ref[p, rows, vl]
        s = lax.dot_general(qb, kb, NT_DIMS, preferred_element_type=F32) * dexp
        yield
        cst = c_ref[p, h]
        num = w_int * jnp.dot(qb, cst.astype(BF16), preferred_element_type=F32)
        yield
        num = num + jnp.dot(s.astype(BF16), vb, preferred_element_type=F32)
        nrow = n_ref[p, h:h + 1, :]
        den = (w_int * jnp.sum(qb.astype(F32) * nrow, axis=1, keepdims=True)
               + jnp.sum(s, axis=1, keepdims=True))
        yield
        hh = num / jnp.maximum(jnp.abs(den), jnp.exp(-m_t))
        b_last = b_c[L - 1:L, :]
        g_c = b_last - b_c + li_c
        m_new = jnp.maximum(b_last + m_prev, jnp.max(g_c, axis=0, keepdims=True))
        f_c = jnp.exp(b_last + m_prev - m_new)
        w_s = jnp.exp(g_c - m_new)
        kw = kb.astype(F32) * w_s
        yield
        c_ref[p, h] = f_c * cst + lax.dot_general(kw.astype(BF16), vb, TN_DIMS, preferred_element_type=F32)
        n_ref[p, h:h + 1, :] = f_c * nrow + jnp.sum(kw, axis=0, keepdims=True)
        m_ref[p] = jnp.where(lane_row == h, m_new, m_ref[p])
        yield
        o_b = _headnorm(hh, mng_ref[:, vl]) * _sigmoid(bo_ref[p, rows, vl].astype(F32))
        yield
        mix = (_sigmoid(ga_ref[p, rows, vl].astype(F32)) * o_a
               + _sigmoid(gb_ref[p, rows, vl].astype(F32)) * o_b)
        mix_scr[p, rows, vl] = mix.astype(BF16)

    def chunk(c, carry):
        rows = pl.ds(pl.multiple_of(c * L, L), L)
        shared = [gates(p, c, rows) for p in range(npar)]
        for h in range(GLA_HEADS):
            units = [head(p, h, rows, shared[p]) for p in range(npar)]
            while units:
                for u in list(units):
                    if next(u, "done") == "done":
                        units.remove(u)
        return carry

    lax.fori_loop(0, tc // L, chunk, 0)

    for p in range(npar):
        x1_ref[p] = x_ref[p] + jnp.dot(mix_scr[p], wout_ref[...], preferred_element_type=F32)

    @pl.when(i == nblk - 1)
    def _fin():
        for p in range(npar):
            for h in range(GLA_HEADS):
                gla_ref[p, h] = st_scr[p, h].T


def _mixer_prompt(x, big, sm, gtc, lw, bsz, seq, tc, npar):
    nblk = seq // tc

    def seg(width, idx):
        return pl.BlockSpec((npar, tc, width), lambda b, i: (b, i, idx))

    const = lambda shape: pl.BlockSpec(shape, lambda b, i: tuple(0 for _ in shape))
    in_specs = [
        seg(D_MODEL, 0),
        seg(GLA_QK, 0), seg(GLA_QK, 1),
        seg(D_MODEL, 1), seg(D_MODEL, 2),
        seg(D_MODEL, 3), seg(D_MODEL, 4), seg(D_MODEL, 5),
        seg(D_MODEL, 6), seg(D_MODEL, 7), seg(D_MODEL, 8),
        seg(SMALL_COLS, 0),
        pl.BlockSpec((npar, tc // CHUNK, 8, CHUNK), lambda b, i: (b, i, 0, 0)),
        const((LANES, GLA_QK)), const((1, GLA_QK)), const((1, D_MODEL)),
        const((CONV_W, 2 * ML_W)), const((1, 2 * ML_W)),
        const((1, 2 * LANES)), const((8, 1)), const((1, D_MODEL)),
        const((D_MODEL, D_MODEL)),
    ]
    out_specs = [
        seg(D_MODEL, 0),
        pl.BlockSpec((npar, GLA_HEADS, GLA_DK, GLA_DV), lambda b, i: (b, 0, 0, 0)),
        pl.BlockSpec((npar, ML_HEADS, ML_DH, ML_DH), lambda b, i: (b, 0, 0, 0)),
        pl.BlockSpec((npar, ML_HEADS, ML_DH), lambda b, i: (b, 0, 0)),
        pl.BlockSpec((npar, 1, LANES), lambda b, i: (b, 0, 0)),
        pl.BlockSpec((npar, CONV_W - 1, 2 * ML_W), lambda b, i: (b, 0, 0)),
    ]
    out_shape = [
        jax.ShapeDtypeStruct((bsz, seq, D_MODEL), F32),
        jax.ShapeDtypeStruct((bsz, GLA_HEADS, GLA_DK, GLA_DV), F32),
        jax.ShapeDtypeStruct((bsz, ML_HEADS, ML_DH, ML_DH), F32),
        jax.ShapeDtypeStruct((bsz, ML_HEADS, ML_DH), F32),
        jax.ShapeDtypeStruct((bsz, 1, LANES), F32),
        jax.ShapeDtypeStruct((bsz, CONV_W - 1, 2 * ML_W), F32),
    ]
    scratch = [
        pltpu.VMEM((npar, GLA_HEADS, GLA_DV, GLA_DK), F32),
        pltpu.VMEM((npar, tc + 8, 2 * ML_W), F32),
        pltpu.VMEM((npar, tc, 2 * ML_W), BF16),
        pltpu.VMEM((npar, tc, D_MODEL), BF16),
    ]
    return pl.pallas_call(
        functools.partial(_mixer_prompt_kernel, tc=tc, nblk=nblk, npar=npar),
        grid=(bsz // npar, nblk),
        in_specs=in_specs, out_specs=out_specs, out_shape=out_shape,
        scratch_shapes=scratch,
        compiler_params=_params(("arbitrary", "arbitrary")),
        name="mixer_prompt",
    )(x, big, big, big, big, big, big, big, big, big, big, sm, gtc,
      lw["w2p"], lw["b2"], lw["gng"], lw["cw"], lw["cb"], lw["gbr"], lw["gbc"], lw["mng"], lw["wout"])


def _col_bcast(row):
    return jnp.broadcast_to(row, (LANES, LANES)).T


def _mixer_sample_kernel(*refs, nb, has_prev):
    n_in = 18 if has_prev else 16
    (x_ref, big_ref, sm_ref, gla_in, c_in, n_ref, m_ref, conv_ref,
     w2_ref, b2_ref, gng_ref, cw_ref, cb_ref, gbr_ref, mng_ref, wout_ref) = refs[:16]
    x1_ref, gla_out, c_out, n_o, m_o, conv_o = refs[n_in:n_in + 6]
    a_s, qa_s, ka_s, va_s, kw_s, qb_s, vb_s, fc_s, den_s, oa_s, num_s = refs[n_in + 6:]
    gla_ref, c_ref, gla_o, c_o = gla_in.at[0], c_in.at[0], gla_out.at[0], c_out.at[0]
    b = pl.program_id(0)
    seg = lambda idx: slice(idx * D_MODEL, (idx + 1) * D_MODEL)
    lane = lax.broadcasted_iota(jnp.int32, (nb, LANES), 1)

    @pl.when(b == 0)
    def _prep():
        sm = sm_ref[...]
        z = jnp.dot(sm[:, 0:LANES], w2_ref[...], precision=HIGHEST, preferred_element_type=F32) + b2_ref[...]
        a_s[...] = jnp.exp(_log_sigmoid(z) * (1.0 / GLA_TAU))
        qa_s[...] = big_ref[:, 0:GLA_QK].astype(F32) * (GLA_DK ** -0.5)
        ka_s[...] = big_ref[:, GLA_QK:2 * GLA_QK].astype(F32)
        va_s[...] = big_ref[:, seg(1)].astype(F32)
        vb_s[...] = big_ref[:, seg(5)].astype(F32)
        xq = big_ref[:, seg(3)].astype(F32)
        xk = big_ref[:, seg(4)].astype(F32)
        w = 2 * ML_W
        y = cb_ref[...]
        for j in range(CONV_W - 1):
            y = y + cw_ref[j:j + 1, :] * conv_ref[:, j * w:(j + 1) * w]
        y = y + cw_ref[CONV_W - 1:CONV_W, :] * jnp.concatenate([xq, xk], axis=1)
        qk = _silu(y)
        qb = qk[:, 0:ML_W]
        kb = qk[:, ML_W:w] * (ML_DH ** -0.5)
        qb_s[...] = qb
        conv_o[:, 0:2 * w] = conv_ref[:, w:3 * w]
        conv_o[:, 2 * w:2 * w + ML_W] = xq
        conv_o[:, 2 * w + ML_W:3 * w] = xk

        li = sm[:, LANES:2 * LANES] + gbr_ref[:, 0:LANES]
        lf = _log_sigmoid(sm[:, 2 * LANES:3 * LANES] + gbr_ref[:, LANES:2 * LANES])
        m_old = m_ref[...]
        m_new = jnp.maximum(lf + m_old, li)
        f_c = jnp.exp(lf + m_old - m_new)
        w_s = jnp.exp(li - m_new)
        m_o[...] = m_new
        fc_s[...] = f_c
        den = jnp.zeros((nb, LANES), F32)
        for h in range(ML_HEADS):
            vl = slice(h * ML_DH, (h + 1) * ML_DH)
            kw = kb[:, vl] * _lane_col(w_s, h)
            kw_s[:, vl] = kw
            n_new = _lane_col(f_c, h) * n_ref[:, vl] + kw
            n_o[:, vl] = n_new
            den = jnp.where(lane == h, jnp.sum(qb[:, vl] * n_new, axis=1, keepdims=True), den)
        den_s[...] = den
        oa_s[...] = jnp.zeros_like(oa_s)
        num_s[...] = jnp.zeros_like(num_s)

    tile = pl.ds(pl.multiple_of(lax.shift_right_logical(b, 3) * 8, 8), 8)
    sub = jnp.bitwise_and(b, 7)

    def row_get(ref, cols):
        t = ref[tile, cols]
        pick = lax.broadcasted_iota(jnp.int32, t.shape, 0) == sub
        return jnp.sum(jnp.where(pick, t, 0.0), axis=0, keepdims=True)

    def row_set(ref, cols, val):
        t = ref[tile, cols]
        pick = lax.broadcasted_iota(jnp.int32, t.shape, 0) == sub
        ref[tile, cols] = jnp.where(pick, val, t)

    lane_row = lax.broadcasted_iota(jnp.int32, (1, LANES), 1)
    fc_row = row_get(fc_s, slice(0, LANES))
    for h in range(GLA_HEADS):
        sl = slice(h * GLA_DK, (h + 1) * GLA_DK)
        vl = slice(h * GLA_DV, (h + 1) * GLA_DV)
        a_c = _col_bcast(row_get(a_s, sl))
        k_c = _col_bcast(row_get(ka_s, sl))
        q_c = _col_bcast(row_get(qa_s, sl))
        v_row = row_get(va_s, vl)
        s_old = gla_ref[0, h]
        halves = []
        for p in range(GLA_DV // LANES):
            ls = slice(p * LANES, (p + 1) * LANES)
            s_new = s_old[:, ls] * a_c + k_c * v_row[:, ls]
            gla_o[0, h, :, ls] = s_new
            halves.append(jnp.sum(q_c * s_new, axis=0, keepdims=True))
        row_set(oa_s, vl, jnp.concatenate(halves, axis=1))

        f_c = jnp.sum(jnp.where(lane_row == h, fc_row, 0.0), axis=1, keepdims=True)
        vb_row = row_get(vb_s, vl)
        acc = [jnp.zeros((1, LANES), F32) for _ in range(ML_DH // LANES)]
        for r in range(ML_DH // LANES):
            rs = slice(r * LANES, (r + 1) * LANES)
            ks = slice(h * ML_DH + r * LANES, h * ML_DH + (r + 1) * LANES)
            kw_c = _col_bcast(row_get(kw_s, ks))
            qb_c = _col_bcast(row_get(qb_s, ks))
            for p in range(ML_DH // LANES):
                ls = slice(p * LANES, (p + 1) * LANES)
                c_new = f_c * c_ref[0, h, rs, ls] + kw_c * vb_row[:, ls]
                c_o[0, h, rs, ls] = c_new
                acc[p] = acc[p] + jnp.sum(qb_c * c_new, axis=0, keepdims=True)
        row_set(num_s, vl, jnp.concatenate(acc, axis=1))

    @pl.when(b == nb - 1)
    def _post():
        m_new = m_o[...]
        den = den_s[...]
        for h in range(GLA_HEADS):
            vl = slice(h * GLA_DV, (h + 1) * GLA_DV)
            o_a = _headnorm(oa_s[:, vl], gng_ref[:, vl]) * _silu(big_ref[:, D_MODEL * 2 + h * GLA_DV:
                                                                       D_MODEL * 2 + (h + 1) * GLA_DV].astype(F32))
            dn = jnp.maximum(jnp.abs(_lane_col(den, h)), jnp.exp(-_lane_col(m_new, h)))
            hh = num_s[:, vl] / dn
            o_b = _headnorm(hh, mng_ref[:, vl]) * _sigmoid(
                big_ref[:, 6 * D_MODEL + h * ML_DH:6 * D_MODEL + (h + 1) * ML_DH].astype(F32))
            ga = big_ref[:, 7 * D_MODEL + h * ML_DH:7 * D_MODEL + (h + 1) * ML_DH].astype(F32)
            gb = big_ref[:, 8 * D_MODEL + h * ML_DH:8 * D_MODEL + (h + 1) * ML_DH].astype(F32)
            oa_s[:, vl] = _sigmoid(ga) * o_a + _sigmoid(gb) * o_b
        x1_ref[...] = x_ref[...] + jnp.dot(oa_s[...].astype(BF16), wout_ref[...], preferred_element_type=F32)


def _mixer_sample(x, big, sm, st_gla, st_c, st_n, st_m, st_conv, lw, li, prev_gla, prev_c):
    nb = x.shape[0]
    has_prev = prev_gla is not None
    full = lambda shape: pl.BlockSpec(shape, lambda b: tuple(0 for _ in shape))
    gla_spec = pl.BlockSpec((1, 1, GLA_HEADS, GLA_DK, GLA_DV), lambda b: (li, b, 0, 0, 0))
    c_spec = pl.BlockSpec((1, 1, ML_HEADS, ML_DH, ML_DH), lambda b: (li, b, 0, 0, 0))
    in_specs = [
        full((nb, D_MODEL)), full((nb, BIG_COLS)), full((nb, SMALL_COLS)),
        gla_spec, c_spec,
        full((nb, ML_W)), full((nb, LANES)), full((nb, 3 * 2 * ML_W)),
        full((LANES, GLA_QK)), full((1, GLA_QK)), full((1, D_MODEL)),
        full((CONV_W, 2 * ML_W)), full((1, 2 * ML_W)), full((1, 2 * LANES)), full((1, D_MODEL)),
        full((D_MODEL, D_MODEL)),
    ]
    operands = [x, big, sm, st_gla, st_c, st_n, st_m, st_conv,
                lw["w2p"], lw["b2"], lw["gng"], lw["cw"], lw["cb"], lw["gbr"], lw["mng"], lw["wout"]]
    aliases = {}
    if has_prev:
        in_specs += [pl.BlockSpec(memory_space=pl.ANY), pl.BlockSpec(memory_space=pl.ANY)]
        operands += [prev_gla, prev_c]
        aliases = {16: 1, 17: 2}
    out_specs = [
        full((nb, D_MODEL)),
        gla_spec, c_spec,
        full((nb, ML_W)), full((nb, LANES)), full((nb, 3 * 2 * ML_W)),
    ]
    out_shape = [
        jax.ShapeDtypeStruct((nb, D_MODEL), F32),
        jax.ShapeDtypeStruct(st_gla.shape, F32),
        jax.ShapeDtypeStruct(st_c.shape, F32),
        jax.ShapeDtypeStruct((nb, ML_W), F32),
        jax.ShapeDtypeStruct((nb, LANES), F32),
        jax.ShapeDtypeStruct((nb, 3 * 2 * ML_W), F32),
    ]
    vm = lambda cols: pltpu.VMEM((nb, cols), F32)
    scratch = [vm(GLA_QK), vm(GLA_QK), vm(GLA_QK), vm(D_MODEL), vm(ML_W), vm(ML_W), vm(ML_W),
               vm(LANES), vm(LANES), vm(D_MODEL), vm(ML_W)]
    return pl.pallas_call(
        functools.partial(_mixer_sample_kernel, nb=nb, has_prev=has_prev),
        grid=(nb,),
        in_specs=in_specs, out_specs=out_specs, out_shape=out_shape,
        scratch_shapes=scratch,
        input_output_aliases=aliases,
        compiler_params=_params(("arbitrary",)),
        name="mixer_sample",
    )(*operands)


def _peer_select_kernel(x_ref, g_ref, wq_ref, sk_ref, h2_ref, rank_ref, p1_ref, cnt_ref, p0_ref,
                        sc_scr, *, tq, cw):
    nch = tq // cw
    hb = _rms(x_ref[...], g_ref[...]).astype(BF16)
    h2_ref[...] = hb
    q = jnp.dot(hb, wq_ref[...], preferred_element_type=F32).astype(BF16)
    for hc in range(2 * PEER_HEADS):
        for ch in range(nch):
            qs = q[ch * cw:(ch + 1) * cw, hc * LANES:(hc + 1) * LANES]
            sc_scr[hc, ch] = lax.dot_general(sk_ref[hc % 2], qs, NT_DIMS, preferred_element_type=F32)

    K = PEER_TOPK
    neg = -jnp.inf
    iota_k = lax.broadcasted_iota(jnp.int32, (K, cw), 0)
    iota_kf = iota_k.astype(F32)

    def body(it, carry):
        h = it // nch
        ch = it % nch
        s0 = sc_scr[2 * h, ch]
        s1 = sc_scr[2 * h + 1, ch]

        sv0 = jnp.zeros((K, cw), F32)
        prev = jnp.full((1, cw), jnp.inf, F32)
        for j in range(K):
            prev = jnp.max(jnp.where(s0 < prev, s0, neg), axis=0, keepdims=True)
            sv0 = jnp.where(iota_k == j, prev, sv0)
        sv1 = jnp.zeros((K, cw), F32)
        rank1 = jnp.full((N_KEYS, cw), float(K), F32)
        work = s1
        for j in range(K):
            mx = jnp.max(work, axis=0, keepdims=True)
            hit = work == mx
            rank1 = jnp.where(hit, float(j), rank1)
            work = jnp.where(hit, neg, work)
            sv1 = jnp.where(iota_k == j, mx, sv1)

        top = sv0[0:1, :] + sv1[0:1, :]
        front = sv0 + sv1[0:1, :]
        cnt = jnp.zeros((K, cw), F32)
        zsum = jnp.zeros((1, cw), F32)
        for _ in range(K):
            fm = jnp.max(front, axis=0, keepdims=True)
            j1 = jnp.min(jnp.where(front == fm, iota_kf, float(K)), axis=0, keepdims=True)
            sel = iota_kf == j1
            zsum = zsum + jnp.exp(fm - top)
            nxt = jnp.sum(jnp.where(sel, cnt, 0.0), axis=0, keepdims=True) + 1.0
            cnt = jnp.where(sel, cnt + 1.0, cnt)
            sv1n = jnp.sum(jnp.where(iota_kf == nxt, sv1, 0.0), axis=0, keepdims=True)
            sv1n = jnp.where(nxt >= float(K), neg, sv1n)
            sv0s = jnp.sum(jnp.where(sel, sv0, 0.0), axis=0, keepdims=True)
            front = jnp.where(sel, sv0s + sv1n, front)

        cntp = jnp.zeros((N_KEYS, cw), F32)
        for j in range(K):
            cntp = cntp + jnp.where(s0 == sv0[j:j + 1, :], cnt[j:j + 1, :], 0.0)

        rank_ref[h, ch] = rank1.astype(BF16)
        p1_ref[h, ch] = jnp.exp(s1 - sv1[0:1, :]).astype(BF16)
        cnt_ref[h, ch] = cntp
        p0_ref[h, ch] = jnp.exp(s0 - sv0[0:1, :]) / zsum
        return carry

    lax.fori_loop(0, PEER_HEADS * nch, body, 0)


def _peer_select(x1, g2, wq, sk, tq, cw):
    t = x1.shape[0]
    nch = tq // cw
    meta = lambda dt: jax.ShapeDtypeStruct((PEER_HEADS, t // cw, N_KEYS, cw), dt)
    mspec = pl.BlockSpec((PEER_HEADS, nch, N_KEYS, cw), lambda i: (0, i, 0, 0))
    return pl.pallas_call(
        functools.partial(_peer_select_kernel, tq=tq, cw=cw),
        grid=(t // tq,),
        in_specs=[
            pl.BlockSpec((tq, D_MODEL), lambda i: (i, 0)),
            pl.BlockSpec((1, D_MODEL), lambda i: (0, 0)),
            pl.BlockSpec((D_MODEL, PEER_HEADS * PEER_DQ), lambda i: (0, 0)),
            pl.BlockSpec((2, N_KEYS, PEER_DQ // 2), lambda i: (0, 0, 0)),
        ],
        out_specs=[pl.BlockSpec((tq, D_MODEL), lambda i: (i, 0)), mspec, mspec, mspec, mspec],
        out_shape=[jax.ShapeDtypeStruct((t, D_MODEL), BF16), meta(BF16), meta(BF16), meta(F32), meta(F32)],
        scratch_shapes=[pltpu.VMEM((2 * PEER_HEADS, nch, N_KEYS, cw), F32)],
        compiler_params=_params(("arbitrary",)),
        name="peer_select",
    )(x1, g2, wq, sk)


def _peer_dense_kernel(x_ref, h2_ref, rank_ref, p1_ref, cnt_ref, p0_ref, u_ref, vt_ref, fg_ref,
                       out_ref, acc_scr, g_scr, ht_scr, *, tb, eb, ec, cw, nblk_e, final_norm):
    ngrp = eb // ec
    j = pl.program_id(1)

    @pl.when(j == 0)
    def _():
        acc_scr[...] = jnp.zeros_like(acc_scr)

    for grp in range(ngrp):
        es = slice(grp * ec, (grp + 1) * ec)
        ht_scr[grp] = lax.dot_general(u_ref[es, :], h2_ref[...], NT_DIMS, preferred_element_type=F32)
    for grp in range(ngrp):
        es = slice(grp * ec, (grp + 1) * ec)
        for k in range(ec // N_KEYS):
            r = grp * (ec // N_KEYS) + k
            ks = slice(grp * ec + k * N_KEYS, grp * ec + (k + 1) * N_KEYS)
            for ch in range(tb // cw):
                cs = slice(ch * cw, (ch + 1) * cw)
                w = jnp.zeros((N_KEYS, cw), BF16)
                for h in range(PEER_HEADS):
                    cnt_row = jnp.broadcast_to(cnt_ref[h, ch, r:r + 1, :], (N_KEYS, cw)).astype(BF16)
                    p0_row = jnp.broadcast_to(p0_ref[h, ch, r:r + 1, :], (N_KEYS, cw)).astype(BF16)
                    w = w + jnp.where(rank_ref[h, ch] < cnt_row, p1_ref[h, ch] * p0_row, jnp.zeros_like(w))
                z = ht_scr[grp, k * N_KEYS:(k + 1) * N_KEYS, cs]
                act = 0.5 * z * (1.0 + lax.erf(z * 0.7071067811865476))
                g_scr[ks, cs] = w * act.astype(BF16)
        acc_scr[...] += jnp.dot(vt_ref[:, es], g_scr[es, :], preferred_element_type=F32)

    @pl.when(j == nblk_e - 1)
    def _():
        y = x_ref[...] + acc_scr[...].T
        if final_norm:
            y = _rms(y, fg_ref[...])
        out_ref[...] = y


def _peer_dense(x1, h2, rank, p1, cnt, p0, u_bf, vt_bf, fg, tb, eb, cw, final_norm):
    t = x1.shape[0]
    nch = tb // cw
    nblk_e = N_EXPERTS // eb
    ec = min(eb, 512)
    mspec = pl.BlockSpec((PEER_HEADS, nch, N_KEYS, cw), lambda i, j: (0, i, 0, 0))
    rspec = pl.BlockSpec((PEER_HEADS, nch, eb // N_KEYS, cw), lambda i, j: (0, i, j, 0))
    return pl.pallas_call(
        functools.partial(_peer_dense_kernel, tb=tb, eb=eb, ec=ec, cw=cw, nblk_e=nblk_e,
                          final_norm=final_norm),
        grid=(t // tb, nblk_e),
        in_specs=[
            pl.BlockSpec((tb, D_MODEL), lambda i, j: (i, 0)),
            pl.BlockSpec((tb, D_MODEL), lambda i, j: (i, 0)),
            mspec, mspec, rspec, rspec,
            pl.BlockSpec((eb, D_MODEL), lambda i, j: (j, 0)),
            pl.BlockSpec((D_MODEL, eb), lambda i, j: (0, j)),
            pl.BlockSpec((1, D_MODEL), lambda i, j: (0, 0)),
        ],
        out_specs=pl.BlockSpec((tb, D_MODEL), lambda i, j: (i, 0)),
        out_shape=jax.ShapeDtypeStruct((t, D_MODEL), F32),
        scratch_shapes=[pltpu.VMEM((D_MODEL, tb), F32), pltpu.VMEM((eb, tb), BF16),
                        pltpu.VMEM((eb // ec, ec, tb), F32)],
        compiler_params=_params(("arbitrary", "arbitrary")),
        name="peer_dense",
    )(x1, h2, rank, p1, cnt, p0, u_bf, vt_bf, fg)


def _layer_weights(li, w_in, gla_w2, gla_b2, gla_norm_g, conv_w, conv_b, ml_i_b, ml_f_b, ml_norm_g,
                   w_out, peer_wq, peer_subkeys, peer_u, peer_v):
    w = w_in[li]
    o = 0
    segs = {}
    for name, width in (("aq", GLA_QK), ("ak", GLA_QK), ("av", D_MODEL), ("ag", D_MODEL),
                        ("alr", GLA_GATE_RANK), ("bq", ML_W), ("bk", ML_W), ("bv", ML_W),
                        ("bi", ML_HEADS), ("bf", ML_HEADS), ("bo", ML_W),
                        ("ga", D_MODEL), ("gb", D_MODEL)):
        segs[name] = w[:, o:o + width]
        o += width
    seg_w = WSEG_BLOCKS * D_MODEL
    starts = (0, seg_w + GLA_GATE_RANK, 2 * seg_w + GLA_GATE_RANK + 2 * ML_HEADS)
    w_segs = tuple(w[:, s:s + seg_w].astype(BF16) for s in starts)
    pad = lambda a: jnp.pad(a, ((0, 0), (0, LANES - a.shape[1])))
    w_small = jnp.concatenate([pad(segs["alr"]), pad(segs["bi"]), pad(segs["bf"])], axis=1).astype(BF16)
    w_gt = jnp.concatenate([segs["bi"], segs["bf"]], axis=1).T.astype(BF16)
    gate_b = jnp.concatenate([ml_i_b[li], ml_f_b[li]])
    return dict(
        w_segs=w_segs, w_small=w_small, w_gt=w_gt,
        w2p=jnp.pad(gla_w2[li], ((0, LANES - GLA_GATE_RANK), (0, 0))),
        b2=gla_b2[li][None, :], gng=gla_norm_g[li][None, :],
        cw=conv_w[li], cb=conv_b[li][None, :],
        gbr=jnp.concatenate([jnp.pad(ml_i_b[li], (0, LANES - ML_HEADS)),
                             jnp.pad(ml_f_b[li], (0, LANES - ML_HEADS))])[None, :],
        gbc=gate_b[:, None], mng=ml_norm_g[li][None, :],
        wout=w_out[li].astype(BF16),
        wq=peer_wq[li].astype(BF16), sk=peer_subkeys[li].astype(BF16),
        u=peer_u[li].astype(BF16), vt=peer_v[li].T.astype(BF16),
    )


def _pick(n, cands):
    for c in cands:
        if n % c == 0:
            return c
    return n


def _peer(x1, g2, lw, fg, final_norm):
    t = x1.shape[0]
    tq = _pick(t, (512, 256, 128))
    tb = _pick(t, (512, 256, 128))
    cw = _pick(tb, (2 * LANES, LANES))
    h2, rank, p1, cnt, p0 = _peer_select(x1, g2, lw["wq"], lw["sk"], tq, cw)
    return _peer_dense(x1, h2, rank, p1, cnt, p0, lw["u"], lw["vt"], fg, tb, 2048, cw, final_norm)


def kernel(x_prompt, x_sample, state_gla, state_mlstm_c, state_mlstm_n, state_mlstm_m, state_conv,
           norm1_g, w_in, gla_w2, gla_b2, gla_norm_g, conv_w, conv_b, ml_i_b, ml_f_b, ml_norm_g,
           w_out, norm2_g, peer_wq, peer_subkeys, peer_u, peer_v, final_g):
    depth = w_in.shape[0]
    bsz, seq, _ = x_prompt.shape
    nb = x_sample.shape[0]
    assert seq % CHUNK == 0 and x_sample.shape[1] == 1 and nb % LANES == 0
    tp = bsz * seq
    tm = _pick(tp, (2048, 1024, 512, 256, 128))
    tc = _pick(seq, (256, 128, 64))
    npar = _pick(bsz, (2, 1))
    fg = final_g[None, :]

    xp = x_prompt.reshape(tp, D_MODEL)
    xs = x_sample.reshape(nb, D_MODEL)
    p_out = [[] for _ in range(5)]
    s_out = [[] for _ in range(5)]
    s_gla = s_c = None
    for li in range(depth):
        lw = _layer_weights(li, w_in, gla_w2, gla_b2, gla_norm_g, conv_w, conv_b, ml_i_b, ml_f_b,
                            ml_norm_g, w_out, peer_wq, peer_subkeys, peer_u, peer_v)
        g1 = norm1_g[li][None, :]
        g2 = norm2_g[li][None, :]
        last = li == depth - 1

        big, sm, gt = _inproj(xp, g1, lw["w_segs"], lw["w_small"], lw["w_gt"], tm)
        gtc = gt.reshape(8, bsz, seq // CHUNK, CHUNK).transpose(1, 2, 0, 3)
        x1, gla, c, n, m, conv = _mixer_prompt(
            xp.reshape(bsz, seq, D_MODEL), big.reshape(bsz, seq, BIG_COLS),
            sm.reshape(bsz, seq, SMALL_COLS), gtc, lw, bsz, seq, tc, npar)
        xp = _peer(x1.reshape(tp, D_MODEL), g2, lw, fg, last)
        for lst, val in zip(p_out, (gla, c, n, m[:, 0, :ML_HEADS], conv)):
            lst.append(val)

        big, sm, _ = _inproj(xs, g1, lw["w_segs"], lw["w_small"], lw["w_gt"], nb)
        x1, s_gla, s_c, n, m, conv = _mixer_sample(
            xs, big, sm, state_gla, state_mlstm_c,
            state_mlstm_n[li].reshape(nb, ML_W),
            jnp.pad(state_mlstm_m[li], ((0, 0), (0, LANES - ML_HEADS))),
            state_conv[li].reshape(nb, (CONV_W - 1) * 2 * ML_W), lw, li, s_gla, s_c)
        xs = _peer(x1, g2, lw, fg, last)
        for lst, val in zip(s_out[2:], (n.reshape(nb, ML_HEADS, ML_DH), m[:, :ML_HEADS],
                                        conv.reshape(nb, CONV_W - 1, 2 * ML_W))):
            lst.append(val)

    y_prompt = xp.reshape(bsz, seq, D_MODEL)
    y_sample = xs.reshape(nb, 1, D_MODEL)
    return (y_prompt, y_sample, *[jnp.stack(v) for v in p_out],
            s_gla, s_c, *[jnp.stack(v) for v in s_out[2:]])
```

```python
import functools

import jax
import jax.numpy as jnp
from jax import lax
from jax.experimental import pallas as pl
from jax.experimental.pallas import tpu as pltpu

F32 = jnp.float32
BF16 = jnp.bfloat16
HIGHEST = lax.Precision.HIGHEST

D_MODEL = 1024
GLA_HEADS = 4
GLA_DK = 128
GLA_DV = 256
GLA_GATE_RANK = 16
GLA_TAU = 16.0
GLA_QK = GLA_HEADS * GLA_DK
ML_HEADS = 4
ML_DH = 256
ML_W = ML_HEADS * ML_DH
CONV_W = 4
CHUNK = 64
PEER_HEADS = 8
PEER_DQ = 256
N_KEYS = 128
N_EXPERTS = N_KEYS * N_KEYS
PEER_TOPK = 16
EPS = 1e-6
GELU_IN_SCALE = 0.7071067811865476

LANES = 128
BIG_COLS = 9 * D_MODEL
WSEG_BLOCKS = 3
HEADS_PER_ROUND = 4
SMALL_COLS = 3 * LANES
VMEM_LIMIT_BYTES = 56 * 1024 * 1024

NT_DIMS = (((1,), (1,)), ((), ()))
TN_DIMS = (((0,), (0,)), ((), ()))


def _params(sem, flags=None):
    return pltpu.CompilerParams(dimension_semantics=sem, vmem_limit_bytes=VMEM_LIMIT_BYTES, flags=flags)


def _log_sigmoid(x):
    return jnp.minimum(x, 0.0) - jnp.log1p(jnp.exp(-jnp.abs(x)))


def _sigmoid(x):
    return 1.0 / (1.0 + jnp.exp(-x))


def _silu(x):
    return x * _sigmoid(x)


def _rms(x, g):
    return x * lax.rsqrt(jnp.mean(x * x, axis=-1, keepdims=True) + EPS) * g


def _lane_col(a, j):
    lane = lax.broadcasted_iota(jnp.int32, a.shape, 1)
    return jnp.sum(jnp.where(lane == j, a, 0.0), axis=1, keepdims=True)


def _inproj_kernel(x_ref, g_ref, wa_ref, wb_ref, wc_ref, wsm_ref, wgt_ref, big_ref, sm_ref, gt_ref, h_scr):
    j = pl.program_id(1)

    @pl.when(j == 0)
    def _():
        hb = _rms(x_ref[...], g_ref[...]).astype(BF16)
        h_scr[...] = hb
        sm_ref[...] = jnp.dot(hb, wsm_ref[...], preferred_element_type=F32)
        gt_ref[...] = lax.dot_general(wgt_ref[...], hb, NT_DIMS, preferred_element_type=F32)

    for s, w_ref in enumerate((wa_ref, wb_ref, wc_ref)):
        @pl.when((j >= s * WSEG_BLOCKS) & (j < (s + 1) * WSEG_BLOCKS))
        def _(w_ref=w_ref):
            big_ref[...] = jnp.dot(h_scr[...], w_ref[...], preferred_element_type=F32).astype(BF16)


def _inproj(x, g, w_segs, w_small, w_gt, tm):
    t = x.shape[0]
    tn = D_MODEL

    def wspec(s):
        return pl.BlockSpec((D_MODEL, tn), lambda i, j: (0, jnp.clip(j - s * WSEG_BLOCKS, 0, WSEG_BLOCKS - 1)))

    return pl.pallas_call(
        _inproj_kernel,
        grid=(t // tm, BIG_COLS // tn),
        in_specs=[
            pl.BlockSpec((tm, D_MODEL), lambda i, j: (i, 0)),
            pl.BlockSpec((1, D_MODEL), lambda i, j: (0, 0)),
            wspec(0), wspec(1), wspec(2),
            pl.BlockSpec((D_MODEL, SMALL_COLS), lambda i, j: (0, 0)),
            pl.BlockSpec((8, D_MODEL), lambda i, j: (0, 0)),
        ],
        out_specs=[
            pl.BlockSpec((tm, tn), lambda i, j: (i, j)),
            pl.BlockSpec((tm, SMALL_COLS), lambda i, j: (i, 0)),
            pl.BlockSpec((8, tm), lambda i, j: (0, i)),
        ],
        out_shape=[
            jax.ShapeDtypeStruct((t, BIG_COLS), BF16),
            jax.ShapeDtypeStruct((t, SMALL_COLS), F32),
            jax.ShapeDtypeStruct((8, t), F32),
        ],
        scratch_shapes=[pltpu.VMEM((tm, D_MODEL), BF16)],
        compiler_params=_params(("arbitrary", "arbitrary")),
        name="inproj",
    )(x, g, *w_segs, w_small, w_gt)


def _headnorm(o, g_row):
    return o * lax.rsqrt(jnp.mean(o * o, axis=-1, keepdims=True) + EPS) * g_row


def _mixer_prompt_kernel(x_ref, aq_ref, ak_ref, av_ref, ag_ref, bq_ref, bk_ref, bv_ref, bo_ref,
                         ga_ref, gb_ref, sm_ref, gt_ref, w2_ref, b2_ref, gng_ref, cw_ref, cb_ref,
                         gbr_ref, gbc_ref, mng_ref, wout_ref,
                         x1_ref, gla_ref, c_ref, n_ref, m_ref, conv_ref,
                         st_scr, xp_scr, qk_scr, mix_scr, *, tc, nblk, npar):
    i = pl.program_id(1)
    L = CHUNK

    @pl.when(i == 0)
    def _init():
        st_scr[...] = jnp.zeros_like(st_scr)
        c_ref[...] = jnp.zeros_like(c_ref)
        n_ref[...] = jnp.zeros_like(n_ref)
        m_ref[...] = jnp.zeros_like(m_ref)
        xp_scr[:, 0:8, :] = jnp.zeros((npar, 8, 2 * ML_W), F32)

    for p in range(npar):
        xp_scr[p, 8:8 + tc, 0:ML_W] = bq_ref[p].astype(F32)
        xp_scr[p, 8:8 + tc, ML_W:2 * ML_W] = bk_ref[p].astype(F32)
        y = cb_ref[...]
        for j in range(CONV_W):
            y = y + cw_ref[j:j + 1, :] * xp_scr[p, 5 + j:5 + j + tc, :]
        qk = _silu(y)
        qk_scr[p, :, 0:ML_W] = qk[:, 0:ML_W].astype(BF16)
        qk_scr[p, :, ML_W:2 * ML_W] = (qk[:, ML_W:2 * ML_W] * (ML_DH ** -0.5)).astype(BF16)
        xp_scr[p, 5:8, :] = xp_scr[p, tc + 5:tc + 8, :]

    @pl.when(i == nblk - 1)
    def _():
        conv_ref[...] = xp_scr[:, 5:8, :]

    row = lax.broadcasted_iota(jnp.int32, (L, L), 0)
    col = lax.broadcasted_iota(jnp.int32, (L, L), 1)
    causal = col <= row
    tril = causal.astype(F32)
    triu = (row <= col).astype(F32)
    lane_row = lax.broadcasted_iota(jnp.int32, (1, LANES), 1)

    def gates(p, c, rows):
        sm = sm_ref[p, rows, :]
        z = jnp.dot(sm[:, 0:LANES], w2_ref[...], precision=HIGHEST, preferred_element_type=F32) + b2_ref[...]
        log_a = _log_sigmoid(z) * (1.0 / GLA_TAU)
        bc = jnp.dot(tril, log_a, precision=HIGHEST, preferred_element_type=F32)
        li_c_all = sm[:, LANES:2 * LANES] + gbr_ref[:, 0:LANES]
        lf_c_all = _log_sigmoid(sm[:, 2 * LANES:3 * LANES] + gbr_ref[:, LANES:2 * LANES])
        bcum_c_all = jnp.dot(tril, lf_c_all, precision=HIGHEST, preferred_element_type=F32)
        gtb = gt_ref[p, c] + gbc_ref[...]
        bcum_r_all = jnp.dot(_log_sigmoid(gtb), triu, precision=HIGHEST, preferred_element_type=F32)
        return bc, li_c_all, bcum_c_all, gtb, bcum_r_all

    def head(p, h, rows, shared):
        bc, li_c_all, bcum_c_all, gtb, bcum_r_all = shared
        sl = slice(h * GLA_DK, (h + 1) * GLA_DK)
        vl = slice(h * GLA_DV, (h + 1) * GLA_DV)
        b = bc[:, sl]
        mid = b[L // 2 - 1:L // 2, :]
        bl = b[L - 1:L, :]
        q = aq_ref[p, rows, sl].astype(F32) * (GLA_DK ** -0.5)
        k = ak_ref[p, rows, sl].astype(F32)
        v = av_ref[p, rows, vl]
        q_in = (q * jnp.exp(b)).astype(BF16)
        q_at = (q * jnp.exp(b - mid)).astype(BF16)
        yield
        k_at = (k * jnp.exp(mid - b)).astype(BF16)
        k_out = (k * jnp.exp(bl - b)).astype(BF16)
        yield
        att = lax.dot_general(q_at, k_at, NT_DIMS, preferred_element_type=F32)
        att = jnp.where(causal, att, 0.0)
        st = st_scr[p, h]
        yield
        o = lax.dot_general(q_in, st.astype(BF16), NT_DIMS, preferred_element_type=F32)
        yield
        o = o + jnp.dot(att.astype(BF16), v, preferred_element_type=F32)
        yield
        st_scr[p, h] = st * jnp.exp(bl) + lax.dot_general(v, k_out, TN_DIMS, preferred_element_type=F32)
        yield
        o_a = _headnorm(o, gng_ref[:, vl]) * _silu(ag_ref[p, rows, vl].astype(F32))
        yield

        li_c = _lane_col(li_c_all, h)
        b_c = _lane_col(bcum_c_all, h)
        li_r = gtb[h:h + 1, :]
        b_r = bcum_r_all[ML_HEADS + h:ML_HEADS + h + 1, :]
        m_prev = jnp.sum(jnp.where(lane_row == h, m_ref[p], 0.0), axis=1, keepdims=True)
        a_int = b_c + m_prev
        dm = jnp.where(causal, b_c - b_r + li_r, -jnp.inf)
        m_t = jnp.maximum(a_int, jnp.max(dm, axis=1, keepdims=True))
        yield
        w_int = jnp.exp(a_int - m_t)
        dexp = jnp.exp(dm - m_t)
        qb = qk_scr[p, rows, vl]
        kb = qk_scr[p, rows, ML_W + h * ML_DH:ML_W + (h + 1) * ML_DH]
        vb = bv_ref[p, rows, vl]
        s = lax.dot_general(qb, kb, NT_DIMS, preferred_element_type=F32) * dexp
        yield
        cst = c_ref[p, h]
        num = w_int * jnp.dot(qb, cst.astype(BF16), preferred_element_type=F32)
        yield
        num = num + jnp.dot(s.astype(BF16), vb, preferred_element_type=F32)
        nrow = n_ref[p, h:h + 1, :]
        den = (w_int * jnp.sum(qb.astype(F32) * nrow, axis=1, keepdims=True)
               + jnp.sum(s, axis=1, keepdims=True))
        yield
        hh = num / jnp.maximum(jnp.abs(den), jnp.exp(-m_t))
        b_last = b_c[L - 1:L, :]
        g_c = b_last - b_c + li_c
        m_new = jnp.maximum(b_last + m_prev, jnp.max(g_c, axis=0, keepdims=True))
        f_c = jnp.exp(b_last + m_prev - m_new)
        w_s = jnp.exp(g_c - m_new)
        kw = kb.astype(F32) * w_s
        yield
        c_ref[p, h] = f_c * cst + lax.dot_general(kw.astype(BF16), vb, TN_DIMS, preferred_element_type=F32)
        n_ref[p, h:h + 1, :] = f_c * nrow + jnp.sum(kw, axis=0, keepdims=True)
        m_ref[p] = jnp.where(lane_row == h, m_new, m_ref[p])
        yield
        o_b = _headnorm(hh, mng_ref[:, vl]) * _sigmoid(bo_ref[p, rows, vl].astype(F32))
        yield
        mix = (_sigmoid(ga_ref[p, rows, vl].astype(F32)) * o_a
               + _sigmoid(gb_ref[p, rows, vl].astype(F32)) * o_b)
        mix_scr[p, rows, vl] = mix.astype(BF16)

    def chunk(c, carry):
        rows = pl.ds(pl.multiple_of(c * L, L), L)
        shared = [gates(p, c, rows) for p in range(npar)]
        for h0 in range(0, GLA_HEADS, HEADS_PER_ROUND):
            units = [head(p, h, rows, shared[p]) for h in range(h0, h0 + HEADS_PER_ROUND) for p in range(npar)]
            while units:
                for u in list(units):
                    if next(u, "done") == "done":
                        units.remove(u)
        return carry

    lax.fori_loop(0, tc // L, chunk, 0)

    for p in range(npar):
        x1_ref[p] = x_ref[p] + jnp.dot(mix_scr[p], wout_ref[...], preferred_element_type=F32)

    @pl.when(i == nblk - 1)
    def _fin():
        for p in range(npar):
            for h in range(GLA_HEADS):
                gla_ref[p, h] = st_scr[p, h].T


def _mixer_prompt(x, big, sm, gtc, lw, bsz, seq, tc, npar):
    nblk = seq // tc

    def seg(width, idx):
        return pl.BlockSpec((npar, tc, width), lambda b, i: (b, i, idx))

    const = lambda shape: pl.BlockSpec(shape, lambda b, i: tuple(0 for _ in shape))
    in_specs = [
        seg(D_MODEL, 0),
        seg(GLA_QK, 0), seg(GLA_QK, 1),
        seg(D_MODEL, 1), seg(D_MODEL, 2),
        seg(D_MODEL, 3), seg(D_MODEL, 4), seg(D_MODEL, 5),
        seg(D_MODEL, 6), seg(D_MODEL, 7), seg(D_MODEL, 8),
        seg(SMALL_COLS, 0),
        pl.BlockSpec((npar, tc // CHUNK, 8, CHUNK), lambda b, i: (b, i, 0, 0)),
        const((LANES, GLA_QK)), const((1, GLA_QK)), const((1, D_MODEL)),
        const((CONV_W, 2 * ML_W)), const((1, 2 * ML_W)),
        const((1, 2 * LANES)), const((8, 1)), const((1, D_MODEL)),
        const((D_MODEL, D_MODEL)),
    ]
    out_specs = [
        seg(D_MODEL, 0),
        pl.BlockSpec((npar, GLA_HEADS, GLA_DK, GLA_DV), lambda b, i: (b, 0, 0, 0)),
        pl.BlockSpec((npar, ML_HEADS, ML_DH, ML_DH), lambda b, i: (b, 0, 0, 0)),
        pl.BlockSpec((npar, ML_HEADS, ML_DH), lambda b, i: (b, 0, 0)),
        pl.BlockSpec((npar, 1, LANES), lambda b, i: (b, 0, 0)),
        pl.BlockSpec((npar, CONV_W - 1, 2 * ML_W), lambda b, i: (b, 0, 0)),
    ]
    out_shape = [
        jax.ShapeDtypeStruct((bsz, seq, D_MODEL), F32),
        jax.ShapeDtypeStruct((bsz, GLA_HEADS, GLA_DK, GLA_DV), F32),
        jax.ShapeDtypeStruct((bsz, ML_HEADS, ML_DH, ML_DH), F32),
        jax.ShapeDtypeStruct((bsz, ML_HEADS, ML_DH), F32),
        jax.ShapeDtypeStruct((bsz, 1, LANES), F32),
        jax.ShapeDtypeStruct((bsz, CONV_W - 1, 2 * ML_W), F32),
    ]
    scratch = [
        pltpu.VMEM((npar, GLA_HEADS, GLA_DV, GLA_DK), F32),
        pltpu.VMEM((npar, tc + 8, 2 * ML_W), F32),
        pltpu.VMEM((npar, tc, 2 * ML_W), BF16),
        pltpu.VMEM((npar, tc, D_MODEL), BF16),
    ]
    return pl.pallas_call(
        functools.partial(_mixer_prompt_kernel, tc=tc, nblk=nblk, npar=npar),
        grid=(bsz // npar, nblk),
        in_specs=in_specs, out_specs=out_specs, out_shape=out_shape,
        scratch_shapes=scratch,
        compiler_params=_params(("arbitrary", "arbitrary")),
        name="mixer_prompt",
    )(x, big, big, big, big, big, big, big, big, big, big, sm, gtc,
      lw["w2p"], lw["b2"], lw["gng"], lw["cw"], lw["cb"], lw["gbr"], lw["gbc"], lw["mng"], lw["wout"])


def _col_bcast(row):
    return jnp.broadcast_to(row, (LANES, LANES)).T


def _mixer_sample_kernel(*refs, nb, has_prev):
    n_in = 18 if has_prev else 16
    (x_ref, big_ref, sm_ref, gla_in, c_in, n_ref, m_ref, conv_ref,
     w2_ref, b2_ref, gng_ref, cw_ref, cb_ref, gbr_ref, mng_ref, wout_ref) = refs[:16]
    x1_ref, gla_out, c_out, n_o, m_o, conv_o = refs[n_in:n_in + 6]
    a_s, qa_s, ka_s, va_s, kw_s, qb_s, vb_s, fc_s, den_s, oa_s, num_s = refs[n_in + 6:]
    gla_ref, c_ref, gla_o, c_o = gla_in.at[0], c_in.at[0], gla_out.at[0], c_out.at[0]
    b = pl.program_id(0)
    seg = lambda idx: slice(idx * D_MODEL, (idx + 1) * D_MODEL)
    lane = lax.broadcasted_iota(jnp.int32, (nb, LANES), 1)

    @pl.when(b == 0)
    def _prep():
        sm = sm_ref[...]
        z = jnp.dot(sm[:, 0:LANES], w2_ref[...], precision=HIGHEST, preferred_element_type=F32) + b2_ref[...]
        a_s[...] = jnp.exp(_log_sigmoid(z) * (1.0 / GLA_TAU))
        qa_s[...] = big_ref[:, 0:GLA_QK].astype(F32) * (GLA_DK ** -0.5)
        ka_s[...] = big_ref[:, GLA_QK:2 * GLA_QK].astype(F32)
        va_s[...] = big_ref[:, seg(1)].astype(F32)
        vb_s[...] = big_ref[:, seg(5)].astype(F32)
        xq = big_ref[:, seg(3)].astype(F32)
        xk = big_ref[:, seg(4)].astype(F32)
        w = 2 * ML_W
        y = cb_ref[...]
        for j in range(CONV_W - 1):
            y = y + cw_ref[j:j + 1, :] * conv_ref[:, j * w:(j + 1) * w]
        y = y + cw_ref[CONV_W - 1:CONV_W, :] * jnp.concatenate([xq, xk], axis=1)
        qk = _silu(y)
        qb = qk[:, 0:ML_W]
        kb = qk[:, ML_W:w] * (ML_DH ** -0.5)
        qb_s[...] = qb
        conv_o[:, 0:2 * w] = conv_ref[:, w:3 * w]
        conv_o[:, 2 * w:2 * w + ML_W] = xq
        conv_o[:, 2 * w + ML_W:3 * w] = xk

        li = sm[:, LANES:2 * LANES] + gbr_ref[:, 0:LANES]
        lf = _log_sigmoid(sm[:, 2 * LANES:3 * LANES] + gbr_ref[:, LANES:2 * LANES])
        m_old = m_ref[...]
        m_new = jnp.maximum(lf + m_old, li)
        f_c = jnp.exp(lf + m_old - m_new)
        w_s = jnp.exp(li - m_new)
        m_o[...] = m_new
        fc_s[...] = f_c
        den = jnp.zeros((nb, LANES), F32)
        for h in range(ML_HEADS):
            vl = slice(h * ML_DH, (h + 1) * ML_DH)
            kw = kb[:, vl] * _lane_col(w_s, h)
            kw_s[:, vl] = kw
            n_new = _lane_col(f_c, h) * n_ref[:, vl] + kw
            n_o[:, vl] = n_new
            den = jnp.where(lane == h, jnp.sum(qb[:, vl] * n_new, axis=1, keepdims=True), den)
        den_s[...] = den
        oa_s[...] = jnp.zeros_like(oa_s)
        num_s[...] = jnp.zeros_like(num_s)

    tile = pl.ds(pl.multiple_of(lax.shift_right_logical(b, 3) * 8, 8), 8)
    sub = jnp.bitwise_and(b, 7)

    def row_get(ref, cols):
        t = ref[tile, cols]
        pick = lax.broadcasted_iota(jnp.int32, t.shape, 0) == sub
        return jnp.sum(jnp.where(pick, t, 0.0), axis=0, keepdims=True)

    def row_set(ref, cols, val):
        t = ref[tile, cols]
        pick = lax.broadcasted_iota(jnp.int32, t.shape, 0) == sub
        ref[tile, cols] = jnp.where(pick, val, t)

    lane_row = lax.broadcasted_iota(jnp.int32, (1, LANES), 1)
    fc_row = row_get(fc_s, slice(0, LANES))
    for h in range(GLA_HEADS):
        sl = slice(h * GLA_DK, (h + 1) * GLA_DK)
        vl = slice(h * GLA_DV, (h + 1) * GLA_DV)
        a_c = _col_bcast(row_get(a_s, sl))
        k_c = _col_bcast(row_get(ka_s, sl))
        q_c = _col_bcast(row_get(qa_s, sl))
        v_row = row_get(va_s, vl)
        s_old = gla_ref[0, h]
        halves = []
        for p in range(GLA_DV // LANES):
            ls = slice(p * LANES, (p + 1) * LANES)
            s_new = s_old[:, ls] * a_c + k_c * v_row[:, ls]
            gla_o[0, h, :, ls] = s_new
            halves.append(jnp.sum(q_c * s_new, axis=0, keepdims=True))
        row_set(oa_s, vl, jnp.concatenate(halves, axis=1))

        f_c = jnp.sum(jnp.where(lane_row == h, fc_row, 0.0), axis=1, keepdims=True)
        vb_row = row_get(vb_s, vl)
        acc = [jnp.zeros((1, LANES), F32) for _ in range(ML_DH // LANES)]
        for r in range(ML_DH // LANES):
            rs = slice(r * LANES, (r + 1) * LANES)
            ks = slice(h * ML_DH + r * LANES, h * ML_DH + (r + 1) * LANES)
            kw_c = _col_bcast(row_get(kw_s, ks))
            qb_c = _col_bcast(row_get(qb_s, ks))
            for p in range(ML_DH // LANES):
                ls = slice(p * LANES, (p + 1) * LANES)
                c_new = f_c * c_ref[0, h, rs, ls] + kw_c * vb_row[:, ls]
                c_o[0, h, rs, ls] = c_new
                acc[p] = acc[p] + jnp.sum(qb_c * c_new, axis=0, keepdims=True)
        row_set(num_s, vl, jnp.concatenate(acc, axis=1))

    @pl.when(b == nb - 1)
    def _post():
        m_new = m_o[...]
        den = den_s[...]
        for h in range(GLA_HEADS):
            vl = slice(h * GLA_DV, (h + 1) * GLA_DV)
            o_a = _headnorm(oa_s[:, vl], gng_ref[:, vl]) * _silu(big_ref[:, D_MODEL * 2 + h * GLA_DV:
                                                                       D_MODEL * 2 + (h + 1) * GLA_DV].astype(F32))
            dn = jnp.maximum(jnp.abs(_lane_col(den, h)), jnp.exp(-_lane_col(m_new, h)))
            hh = num_s[:, vl] / dn
            o_b = _headnorm(hh, mng_ref[:, vl]) * _sigmoid(
                big_ref[:, 6 * D_MODEL + h * ML_DH:6 * D_MODEL + (h + 1) * ML_DH].astype(F32))
            ga = big_ref[:, 7 * D_MODEL + h * ML_DH:7 * D_MODEL + (h + 1) * ML_DH].astype(F32)
            gb = big_ref[:, 8 * D_MODEL + h * ML_DH:8 * D_MODEL + (h + 1) * ML_DH].astype(F32)
            oa_s[:, vl] = _sigmoid(ga) * o_a + _sigmoid(gb) * o_b
        x1_ref[...] = x_ref[...] + jnp.dot(oa_s[...].astype(BF16), wout_ref[...], preferred_element_type=F32)


def _mixer_sample(x, big, sm, st_gla, st_c, st_n, st_m, st_conv, lw, li, prev_gla, prev_c):
    nb = x.shape[0]
    has_prev = prev_gla is not None
    full = lambda shape: pl.BlockSpec(shape, lambda b: tuple(0 for _ in shape))
    gla_spec = pl.BlockSpec((1, 1, GLA_HEADS, GLA_DK, GLA_DV), lambda b: (li, b, 0, 0, 0))
    c_spec = pl.BlockSpec((1, 1, ML_HEADS, ML_DH, ML_DH), lambda b: (li, b, 0, 0, 0))
    in_specs = [
        full((nb, D_MODEL)), full((nb, BIG_COLS)), full((nb, SMALL_COLS)),
        gla_spec, c_spec,
        full((nb, ML_W)), full((nb, LANES)), full((nb, 3 * 2 * ML_W)),
        full((LANES, GLA_QK)), full((1, GLA_QK)), full((1, D_MODEL)),
        full((CONV_W, 2 * ML_W)), full((1, 2 * ML_W)), full((1, 2 * LANES)), full((1, D_MODEL)),
        full((D_MODEL, D_MODEL)),
    ]
    operands = [x, big, sm, st_gla, st_c, st_n, st_m, st_conv,
                lw["w2p"], lw["b2"], lw["gng"], lw["cw"], lw["cb"], lw["gbr"], lw["mng"], lw["wout"]]
    aliases = {}
    if has_prev:
        in_specs += [pl.BlockSpec(memory_space=pl.ANY), pl.BlockSpec(memory_space=pl.ANY)]
        operands += [prev_gla, prev_c]
        aliases = {16: 1, 17: 2}
    out_specs = [
        full((nb, D_MODEL)),
        gla_spec, c_spec,
        full((nb, ML_W)), full((nb, LANES)), full((nb, 3 * 2 * ML_W)),
    ]
    out_shape = [
        jax.ShapeDtypeStruct((nb, D_MODEL), F32),
        jax.ShapeDtypeStruct(st_gla.shape, F32),
        jax.ShapeDtypeStruct(st_c.shape, F32),
        jax.ShapeDtypeStruct((nb, ML_W), F32),
        jax.ShapeDtypeStruct((nb, LANES), F32),
        jax.ShapeDtypeStruct((nb, 3 * 2 * ML_W), F32),
    ]
    vm = lambda cols: pltpu.VMEM((nb, cols), F32)
    scratch = [vm(GLA_QK), vm(GLA_QK), vm(GLA_QK), vm(D_MODEL), vm(ML_W), vm(ML_W), vm(ML_W),
               vm(LANES), vm(LANES), vm(D_MODEL), vm(ML_W)]
    return pl.pallas_call(
        functools.partial(_mixer_sample_kernel, nb=nb, has_prev=has_prev),
        grid=(nb,),
        in_specs=in_specs, out_specs=out_specs, out_shape=out_shape,
        scratch_shapes=scratch,
        input_output_aliases=aliases,
        compiler_params=_params(("arbitrary",)),
        name="mixer_sample",
    )(*operands)


def _peer_select_kernel(x_ref, g_ref, wq_ref, sk_ref, h2_ref, rank_ref, p1_ref, cnt_ref, p0_ref,
                        sc_scr, *, tq, cw):
    nch = tq // cw
    hb = _rms(x_ref[...], g_ref[...]).astype(BF16)
    h2_ref[...] = hb
    q = jnp.dot(hb, wq_ref[...], preferred_element_type=F32).astype(BF16)
    for hc in range(2 * PEER_HEADS):
        for ch in range(nch):
            qs = q[ch * cw:(ch + 1) * cw, hc * LANES:(hc + 1) * LANES]
            sc_scr[hc, ch] = lax.dot_general(sk_ref[hc % 2], qs, NT_DIMS, preferred_element_type=F32)

    K = PEER_TOPK
    neg = -jnp.inf
    iota_k = lax.broadcasted_iota(jnp.int32, (K, cw), 0)
    iota_kf = iota_k.astype(F32)

    def unit(idx):
        h = idx // nch
        ch = idx % nch
        s0 = sc_scr[2 * h, ch]
        s1 = sc_scr[2 * h + 1, ch]

        sv0 = jnp.zeros((K, cw), F32)
        prev = jnp.full((1, cw), jnp.inf, F32)
        sv1 = jnp.zeros((K, cw), F32)
        rank1 = jnp.full((N_KEYS, cw), float(K), F32)
        work = s1
        for j in range(K):
            prev = jnp.max(jnp.where(s0 < prev, s0, neg), axis=0, keepdims=True)
            sv0 = jnp.where(iota_k == j, prev, sv0)
            yield
            mx = jnp.max(work, axis=0, keepdims=True)
            hit = work == mx
            rank1 = jnp.where(hit, float(j), rank1)
            work = jnp.where(hit, neg, work)
            sv1 = jnp.where(iota_k == j, mx, sv1)
            yield
        rank_ref[h, ch] = rank1.astype(BF16)
        p1_ref[h, ch] = jnp.exp(s1 - sv1[0:1, :]).astype(BF16)
        yield

        top = sv0[0:1, :] + sv1[0:1, :]
        front = sv0 + sv1[0:1, :]
        cnt = jnp.zeros((K, cw), F32)
        zsum = jnp.zeros((1, cw), F32)
        cntp = jnp.zeros((N_KEYS, cw), F32)
        for _ in range(K):
            fm = jnp.max(front, axis=0, keepdims=True)
            j1 = jnp.min(jnp.where(front == fm, iota_kf, float(K)), axis=0, keepdims=True)
            sel = iota_kf == j1
            zsum = zsum + jnp.exp(fm - top)
            nxt = jnp.sum(jnp.where(sel, cnt, 0.0), axis=0, keepdims=True) + 1.0
            cnt = jnp.where(sel, cnt + 1.0, cnt)
            sv1n = jnp.sum(jnp.where(iota_kf == nxt, sv1, 0.0), axis=0, keepdims=True)
            sv1n = jnp.where(nxt >= float(K), neg, sv1n)
            sv0s = jnp.sum(jnp.where(sel, sv0, 0.0), axis=0, keepdims=True)
            front = jnp.where(sel, sv0s + sv1n, front)
            cntp = jnp.where(s0 == sv0s, cntp + 1.0, cntp)
            yield
        cnt_ref[h, ch] = cntp
        p0_ref[h, ch] = jnp.exp(s0 - sv0[0:1, :]) * (GELU_IN_SCALE / zsum)

    def body(it, carry):
        units = [unit(2 * it), unit(2 * it + 1)]
        while units:
            for u in list(units):
                if next(u, "done") == "done":
                    units.remove(u)
        return carry

    lax.fori_loop(0, PEER_HEADS * nch // 2, body, 0)


def _peer_select(x1, g2, wq, sk, tq, cw):
    t = x1.shape[0]
    nch = tq // cw
    meta = lambda dt: jax.ShapeDtypeStruct((PEER_HEADS, t // cw, N_KEYS, cw), dt)
    mspec = pl.BlockSpec((PEER_HEADS, nch, N_KEYS, cw), lambda i: (0, i, 0, 0))
    return pl.pallas_call(
        functools.partial(_peer_select_kernel, tq=tq, cw=cw),
        grid=(t // tq,),
        in_specs=[
            pl.BlockSpec((tq, D_MODEL), lambda i: (i, 0)),
            pl.BlockSpec((1, D_MODEL), lambda i: (0, 0)),
            pl.BlockSpec((D_MODEL, PEER_HEADS * PEER_DQ), lambda i: (0, 0)),
            pl.BlockSpec((2, N_KEYS, PEER_DQ // 2), lambda i: (0, 0, 0)),
        ],
        out_specs=[pl.BlockSpec((tq, D_MODEL), lambda i: (i, 0)), mspec, mspec, mspec, mspec],
        out_shape=[jax.ShapeDtypeStruct((t, D_MODEL), BF16), meta(BF16), meta(BF16), meta(F32), meta(F32)],
        scratch_shapes=[pltpu.VMEM((2 * PEER_HEADS, nch, N_KEYS, cw), F32)],
        compiler_params=_params(("arbitrary",)),
        name="peer_select",
    )(x1, g2, wq, sk)


def _peer_dense_kernel(x_ref, h2_ref, rank_ref, p1_ref, cnt_ref, p0_ref, u_ref, vt_ref, fg_ref,
                       out_ref, acc_scr, g_scr, ht_scr, *, tb, eb, ec, cw, nblk_e, final_norm):
    ngrp = eb // ec
    j = pl.program_id(1)

    @pl.when(j == 0)
    def _():
        acc_scr[...] = jnp.zeros_like(acc_scr)

    for grp in range(ngrp):
        es = slice(grp * ec, (grp + 1) * ec)
        ht_scr[grp] = lax.dot_general(u_ref[es, :], h2_ref[...], NT_DIMS, preferred_element_type=F32)
    for grp in range(ngrp):
        es = slice(grp * ec, (grp + 1) * ec)
        for k in range(ec // N_KEYS):
            r = grp * (ec // N_KEYS) + k
            ks = slice(grp * ec + k * N_KEYS, grp * ec + (k + 1) * N_KEYS)
            for ch in range(tb // cw):
                cs = slice(ch * cw, (ch + 1) * cw)
                w = jnp.zeros((N_KEYS, cw), BF16)
                for h in range(PEER_HEADS):
                    cnt_row = jnp.broadcast_to(cnt_ref[h, ch, r:r + 1, :], (N_KEYS, cw)).astype(BF16)
                    p0_row = jnp.broadcast_to(p0_ref[h, ch, r:r + 1, :], (N_KEYS, cw)).astype(BF16)
                    w = w + jnp.where(rank_ref[h, ch] < cnt_row, p1_ref[h, ch] * p0_row, jnp.zeros_like(w))
                z = ht_scr[grp, k * N_KEYS:(k + 1) * N_KEYS, cs]
                g_scr[ks, cs] = w * (z * (1.0 + lax.erf(z))).astype(BF16)
        acc_scr[...] += jnp.dot(vt_ref[:, es], g_scr[es, :], preferred_element_type=F32)

    @pl.when(j == nblk_e - 1)
    def _():
        y = x_ref[...] + acc_scr[...].T
        if final_norm:
            y = _rms(y, fg_ref[...])
        out_ref[...] = y


def _peer_dense(x1, h2, rank, p1, cnt, p0, u_bf, vt_bf, fg, tb, eb, cw, final_norm):
    t = x1.shape[0]
    nch = tb // cw
    nblk_e = N_EXPERTS // eb
    ec = min(eb, 512)
    mspec = pl.BlockSpec((PEER_HEADS, nch, N_KEYS, cw), lambda i, j: (0, i, 0, 0))
    rspec = pl.BlockSpec((PEER_HEADS, nch, eb // N_KEYS, cw), lambda i, j: (0, i, j, 0))
    return pl.pallas_call(
        functools.partial(_peer_dense_kernel, tb=tb, eb=eb, ec=ec, cw=cw, nblk_e=nblk_e,
                          final_norm=final_norm),
        grid=(t // tb, nblk_e),
        in_specs=[
            pl.BlockSpec((tb, D_MODEL), lambda i, j: (i, 0)),
            pl.BlockSpec((tb, D_MODEL), lambda i, j: (i, 0)),
            mspec, mspec, rspec, rspec,
            pl.BlockSpec((eb, D_MODEL), lambda i, j: (j, 0)),
            pl.BlockSpec((D_MODEL, eb), lambda i, j: (0, j)),
            pl.BlockSpec((1, D_MODEL), lambda i, j: (0, 0)),
        ],
        out_specs=pl.BlockSpec((tb, D_MODEL), lambda i, j: (i, 0)),
        out_shape=jax.ShapeDtypeStruct((t, D_MODEL), F32),
        scratch_shapes=[pltpu.VMEM((D_MODEL, tb), F32), pltpu.VMEM((eb, tb), BF16),
                        pltpu.VMEM((eb // ec, ec, tb), F32)],
        compiler_params=_params(("arbitrary", "arbitrary")),
        name="peer_dense",
    )(x1, h2, rank, p1, cnt, p0, u_bf, vt_bf, fg)


def _layer_weights(li, w_in, gla_w2, gla_b2, gla_norm_g, conv_w, conv_b, ml_i_b, ml_f_b, ml_norm_g,
                   w_out, peer_wq, peer_subkeys, peer_u, peer_v):
    w = w_in[li]
    o = 0
    segs = {}
    for name, width in (("aq", GLA_QK), ("ak", GLA_QK), ("av", D_MODEL), ("ag", D_MODEL),
                        ("alr", GLA_GATE_RANK), ("bq", ML_W), ("bk", ML_W), ("bv", ML_W),
                        ("bi", ML_HEADS), ("bf", ML_HEADS), ("bo", ML_W),
                        ("ga", D_MODEL), ("gb", D_MODEL)):
        segs[name] = w[:, o:o + width]
        o += width
    seg_w = WSEG_BLOCKS * D_MODEL
    starts = (0, seg_w + GLA_GATE_RANK, 2 * seg_w + GLA_GATE_RANK + 2 * ML_HEADS)
    w_segs = tuple(w[:, s:s + seg_w].astype(BF16) for s in starts)
    pad = lambda a: jnp.pad(a, ((0, 0), (0, LANES - a.shape[1])))
    w_small = jnp.concatenate([pad(segs["alr"]), pad(segs["bi"]), pad(segs["bf"])], axis=1).astype(BF16)
    w_gt = jnp.concatenate([segs["bi"], segs["bf"]], axis=1).T.astype(BF16)
    gate_b = jnp.concatenate([ml_i_b[li], ml_f_b[li]])
    return dict(
        w_segs=w_segs, w_small=w_small, w_gt=w_gt,
        w2p=jnp.pad(gla_w2[li], ((0, LANES - GLA_GATE_RANK), (0, 0))),
        b2=gla_b2[li][None, :], gng=gla_norm_g[li][None, :],
        cw=conv_w[li], cb=conv_b[li][None, :],
        gbr=jnp.concatenate([jnp.pad(ml_i_b[li], (0, LANES - ML_HEADS)),
                             jnp.pad(ml_f_b[li], (0, LANES - ML_HEADS))])[None, :],
        gbc=gate_b[:, None], mng=ml_norm_g[li][None, :],
        wout=w_out[li].astype(BF16),
        wq=peer_wq[li].astype(BF16), sk=peer_subkeys[li].astype(BF16),
        u=(peer_u[li] * GELU_IN_SCALE).astype(BF16), vt=peer_v[li].T.astype(BF16),
    )


def _pick(n, cands):
    for c in cands:
        if n % c == 0:
            return c
    return n


def _peer(x1, g2, lw, fg, final_norm):
    t = x1.shape[0]
    tq = _pick(t, (512, 256, 128))
    tb = _pick(t, (512, 256, 128))
    cw = _pick(tb, (2 * LANES, LANES))
    h2, rank, p1, cnt, p0 = _peer_select(x1, g2, lw["wq"], lw["sk"], tq, cw)
    return _peer_dense(x1, h2, rank, p1, cnt, p0, lw["u"], lw["vt"], fg, tb, 2048, cw, final_norm)


def kernel(x_prompt, x_sample, state_gla, state_mlstm_c, state_mlstm_n, state_mlstm_m, state_conv,
           norm1_g, w_in, gla_w2, gla_b2, gla_norm_g, conv_w, conv_b, ml_i_b, ml_f_b, ml_norm_g,
           w_out, norm2_g, peer_wq, peer_subkeys, peer_u, peer_v, final_g):
    depth = w_in.shape[0]
    bsz, seq, _ = x_prompt.shape
    nb = x_sample.shape[0]
    assert seq % CHUNK == 0 and x_sample.shape[1] == 1 and nb % LANES == 0
    tp = bsz * seq
    tm = _pick(tp, (2048, 1024, 512, 256, 128))
    tc = _pick(seq, (256, 128, 64))
    npar = _pick(bsz, (2, 1))
    fg = final_g[None, :]

    xp = x_prompt.reshape(tp, D_MODEL)
    xs = x_sample.reshape(nb, D_MODEL)
    p_out = [[] for _ in range(5)]
    s_out = [[] for _ in range(5)]
    s_gla = s_c = None
    for li in range(depth):
        lw = _layer_weights(li, w_in, gla_w2, gla_b2, gla_norm_g, conv_w, conv_b, ml_i_b, ml_f_b,
                            ml_norm_g, w_out, peer_wq, peer_subkeys, peer_u, peer_v)
        g1 = norm1_g[li][None, :]
        g2 = norm2_g[li][None, :]
        last = li == depth - 1

        big, sm, gt = _inproj(xp, g1, lw["w_segs"], lw["w_small"], lw["w_gt"], tm)
        gtc = gt.reshape(8, bsz, seq // CHUNK, CHUNK).transpose(1, 2, 0, 3)
        x1, gla, c, n, m, conv = _mixer_prompt(
            xp.reshape(bsz, seq, D_MODEL), big.reshape(bsz, seq, BIG_COLS),
            sm.reshape(bsz, seq, SMALL_COLS), gtc, lw, bsz, seq, tc, npar)
        xp = _peer(x1.reshape(tp, D_MODEL), g2, lw, fg, last)
        for lst, val in zip(p_out, (gla, c, n, m[:, 0, :ML_HEADS], conv)):
            lst.append(val)

        big, sm, _ = _inproj(xs, g1, lw["w_segs"], lw["w_small"], lw["w_gt"], nb)
        x1, s_gla, s_c, n, m, conv = _mixer_sample(
            xs, big, sm, state_gla, state_mlstm_c,
            state_mlstm_n[li].reshape(nb, ML_W),
            jnp.pad(state_mlstm_m[li], ((0, 0), (0, LANES - ML_HEADS))),
            state_conv[li].reshape(nb, (CONV_W - 1) * 2 * ML_W), lw, li, s_gla, s_c)
        xs = _peer(x1, g2, lw, fg, last)
        for lst, val in zip(s_out[2:], (n.reshape(nb, ML_HEADS, ML_DH), m[:, :ML_HEADS],
                                        conv.reshape(nb, CONV_W - 1, 2 * ML_W))):
            lst.append(val)

    y_prompt = xp.reshape(bsz, seq, D_MODEL)
    y_sample = xs.reshape(nb, 1, D_MODEL)
    return (y_prompt, y_sample, *[jnp.stack(v) for v in p_out],
            s_gla, s_c, *[jnp.stack(v) for v in s_out[2:]])
```

```python
import functools

import jax
import jax.numpy as jnp
from jax import lax
from jax.experimental import pallas as pl
from jax.experimental.pallas import tpu as pltpu

F32 = jnp.float32
BF16 = jnp.bfloat16
HIGHEST = lax.Precision.HIGHEST

D_MODEL = 1024
GLA_HEADS = 4
GLA_DK = 128
GLA_DV = 256
GLA_GATE_RANK = 16
GLA_TAU = 16.0
GLA_QK = GLA_HEADS * GLA_DK
ML_HEADS = 4
ML_DH = 256
ML_W = ML_HEADS * ML_DH
CONV_W = 4
CHUNK = 64
PEER_HEADS = 8
PEER_DQ = 256
N_KEYS = 128
N_EXPERTS = N_KEYS * N_KEYS
PEER_TOPK = 16
EPS = 1e-6
GELU_IN_SCALE = 0.7071067811865476

LANES = 128
BIG_COLS = 9 * D_MODEL
WSEG_BLOCKS = 3
SAMPLES_PER_STEP = 2
HEADS_PER_ROUND = 4
SMALL_COLS = 3 * LANES
VMEM_LIMIT_BYTES = 56 * 1024 * 1024

NT_DIMS = (((1,), (1,)), ((), ()))
TN_DIMS = (((0,), (0,)), ((), ()))


def _params(sem, flags=None):
    return pltpu.CompilerParams(dimension_semantics=sem, vmem_limit_bytes=VMEM_LIMIT_BYTES, flags=flags)


def _log_sigmoid(x):
    return jnp.minimum(x, 0.0) - jnp.log1p(jnp.exp(-jnp.abs(x)))


def _sigmoid(x):
    return 1.0 / (1.0 + jnp.exp(-x))


def _silu(x):
    return x * _sigmoid(x)


def _rms(x, g):
    return x * lax.rsqrt(jnp.mean(x * x, axis=-1, keepdims=True) + EPS) * g


def _lane_col(a, j):
    lane = lax.broadcasted_iota(jnp.int32, a.shape, 1)
    return jnp.sum(jnp.where(lane == j, a, 0.0), axis=1, keepdims=True)


def _inproj_kernel(x_ref, g_ref, wa_ref, wb_ref, wc_ref, wsm_ref, wgt_ref, big_ref, sm_ref, gt_ref, h_scr):
    j = pl.program_id(1)

    @pl.when(j == 0)
    def _():
        hb = _rms(x_ref[...], g_ref[...]).astype(BF16)
        h_scr[...] = hb
        sm_ref[...] = jnp.dot(hb, wsm_ref[...], preferred_element_type=F32)
        gt_ref[...] = lax.dot_general(wgt_ref[...], hb, NT_DIMS, preferred_element_type=F32)

    for s, w_ref in enumerate((wa_ref, wb_ref, wc_ref)):
        @pl.when((j >= s * WSEG_BLOCKS) & (j < (s + 1) * WSEG_BLOCKS))
        def _(w_ref=w_ref):
            big_ref[...] = jnp.dot(h_scr[...], w_ref[...], preferred_element_type=F32).astype(BF16)


def _inproj(x, g, w_segs, w_small, w_gt, tm):
    t = x.shape[0]
    tn = D_MODEL

    def wspec(s):
        return pl.BlockSpec((D_MODEL, tn), lambda i, j: (0, jnp.clip(j - s * WSEG_BLOCKS, 0, WSEG_BLOCKS - 1)))

    return pl.pallas_call(
        _inproj_kernel,
        grid=(t // tm, BIG_COLS // tn),
        in_specs=[
            pl.BlockSpec((tm, D_MODEL), lambda i, j: (i, 0)),
            pl.BlockSpec((1, D_MODEL), lambda i, j: (0, 0)),
            wspec(0), wspec(1), wspec(2),
            pl.BlockSpec((D_MODEL, SMALL_COLS), lambda i, j: (0, 0)),
            pl.BlockSpec((8, D_MODEL), lambda i, j: (0, 0)),
        ],
        out_specs=[
            pl.BlockSpec((tm, tn), lambda i, j: (i, j)),
            pl.BlockSpec((tm, SMALL_COLS), lambda i, j: (i, 0)),
            pl.BlockSpec((8, tm), lambda i, j: (0, i)),
        ],
        out_shape=[
            jax.ShapeDtypeStruct((t, BIG_COLS), BF16),
            jax.ShapeDtypeStruct((t, SMALL_COLS), F32),
            jax.ShapeDtypeStruct((8, t), F32),
        ],
        scratch_shapes=[pltpu.VMEM((tm, D_MODEL), BF16)],
        compiler_params=_params(("arbitrary", "arbitrary")),
        name="inproj",
    )(x, g, *w_segs, w_small, w_gt)


def _headnorm(o, g_row):
    return o * lax.rsqrt(jnp.mean(o * o, axis=-1, keepdims=True) + EPS) * g_row


def _mixer_prompt_kernel(x_ref, aq_ref, ak_ref, av_ref, ag_ref, bq_ref, bk_ref, bv_ref, bo_ref,
                         ga_ref, gb_ref, sm_ref, gt_ref, w2_ref, b2_ref, gng_ref, cw_ref, cb_ref,
                         gbr_ref, gbc_ref, mng_ref, wout_ref,
                         x1_ref, gla_ref, c_ref, n_ref, m_ref, conv_ref,
                         st_scr, xp_scr, qk_scr, mix_scr, *, tc, nblk, npar):
    i = pl.program_id(1)
    L = CHUNK

    @pl.when(i == 0)
    def _init():
        st_scr[...] = jnp.zeros_like(st_scr)
        c_ref[...] = jnp.zeros_like(c_ref)
        n_ref[...] = jnp.zeros_like(n_ref)
        m_ref[...] = jnp.zeros_like(m_ref)
        xp_scr[:, 0:8, :] = jnp.zeros((npar, 8, 2 * ML_W), F32)

    for p in range(npar):
        xp_scr[p, 8:8 + tc, 0:ML_W] = bq_ref[p].astype(F32)
        xp_scr[p, 8:8 + tc, ML_W:2 * ML_W] = bk_ref[p].astype(F32)
        y = cb_ref[...]
        for j in range(CONV_W):
            y = y + cw_ref[j:j + 1, :] * xp_scr[p, 5 + j:5 + j + tc, :]
        qk = _silu(y)
        qk_scr[p, :, 0:ML_W] = qk[:, 0:ML_W].astype(BF16)
        qk_scr[p, :, ML_W:2 * ML_W] = (qk[:, ML_W:2 * ML_W] * (ML_DH ** -0.5)).astype(BF16)
        xp_scr[p, 5:8, :] = xp_scr[p, tc + 5:tc + 8, :]

    @pl.when(i == nblk - 1)
    def _():
        conv_ref[...] = xp_scr[:, 5:8, :]

    row = lax.broadcasted_iota(jnp.int32, (L, L), 0)
    col = lax.broadcasted_iota(jnp.int32, (L, L), 1)
    causal = col <= row
    tril = causal.astype(F32)
    triu = (row <= col).astype(F32)
    lane_row = lax.broadcasted_iota(jnp.int32, (1, LANES), 1)

    def gates(p, c, rows):
        sm = sm_ref[p, rows, :]
        z = jnp.dot(sm[:, 0:LANES], w2_ref[...], precision=HIGHEST, preferred_element_type=F32) + b2_ref[...]
        log_a = _log_sigmoid(z) * (1.0 / GLA_TAU)
        bc = jnp.dot(tril, log_a, precision=HIGHEST, preferred_element_type=F32)
        li_c_all = sm[:, LANES:2 * LANES] + gbr_ref[:, 0:LANES]
        lf_c_all = _log_sigmoid(sm[:, 2 * LANES:3 * LANES] + gbr_ref[:, LANES:2 * LANES])
        bcum_c_all = jnp.dot(tril, lf_c_all, precision=HIGHEST, preferred_element_type=F32)
        gtb = gt_ref[p, c] + gbc_ref[...]
        bcum_r_all = jnp.dot(_log_sigmoid(gtb), triu, precision=HIGHEST, preferred_element_type=F32)
        return bc, li_c_all, bcum_c_all, gtb, bcum_r_all

    def head(p, h, rows, shared):
        bc, li_c_all, bcum_c_all, gtb, bcum_r_all = shared
        sl = slice(h * GLA_DK, (h + 1) * GLA_DK)
        vl = slice(h * GLA_DV, (h + 1) * GLA_DV)
        b = bc[:, sl]
        mid = b[L // 2 - 1:L // 2, :]
        bl = b[L - 1:L, :]
        q = aq_ref[p, rows, sl].astype(F32) * (GLA_DK ** -0.5)
        k = ak_ref[p, rows, sl].astype(F32)
        v = av_ref[p, rows, vl]
        q_in = (q * jnp.exp(b)).astype(BF16)
        q_at = (q * jnp.exp(b - mid)).astype(BF16)
        yield
        k_at = (k * jnp.exp(mid - b)).astype(BF16)
        k_out = (k * jnp.exp(bl - b)).astype(BF16)
        yield
        att = lax.dot_general(q_at, k_at, NT_DIMS, preferred_element_type=F32)
        att = jnp.where(causal, att, 0.0)
        st = st_scr[p, h]
        yield
        o = lax.dot_general(q_in, st.astype(BF16), NT_DIMS, preferred_element_type=F32)
        yield
        o = o + jnp.dot(att.astype(BF16), v, preferred_element_type=F32)
        yield
        st_scr[p, h] = st * jnp.exp(bl) + lax.dot_general(v, k_out, TN_DIMS, preferred_element_type=F32)
        yield
        o_a = _headnorm(o, gng_ref[:, vl]) * _silu(ag_ref[p, rows, vl].astype(F32))
        yield

        li_c = _lane_col(li_c_all, h)
        b_c = _lane_col(bcum_c_all, h)
        li_r = gtb[h:h + 1, :]
        b_r = bcum_r_all[ML_HEADS + h:ML_HEADS + h + 1, :]
        m_prev = jnp.sum(jnp.where(lane_row == h, m_ref[p], 0.0), axis=1, keepdims=True)
        a_int = b_c + m_prev
        dm = jnp.where(causal, b_c - b_r + li_r, -jnp.inf)
        m_t = jnp.maximum(a_int, jnp.max(dm, axis=1, keepdims=True))
        yield
        w_int = jnp.exp(a_int - m_t)
        dexp = jnp.exp(dm - m_t)
        qb = qk_scr[p, rows, vl]
        kb = qk_scr[p, rows, ML_W + h * ML_DH:ML_W + (h + 1) * ML_DH]
        vb = bv_ref[p, rows, vl]
        s = lax.dot_general(qb, kb, NT_DIMS, preferred_element_type=F32) * dexp
        yield
        cst = c_ref[p, h]
        num = w_int * jnp.dot(qb, cst.astype(BF16), preferred_element_type=F32)
        yield
        num = num + jnp.dot(s.astype(BF16), vb, preferred_element_type=F32)
        nrow = n_ref[p, h:h + 1, :]
        den = (w_int * jnp.sum(qb.astype(F32) * nrow, axis=1, keepdims=True)
               + jnp.sum(s, axis=1, keepdims=True))
        yield
        hh = num / jnp.maximum(jnp.abs(den), jnp.exp(-m_t))
        b_last = b_c[L - 1:L, :]
        g_c = b_last - b_c + li_c
        m_new = jnp.maximum(b_last + m_prev, jnp.max(g_c, axis=0, keepdims=True))
        f_c = jnp.exp(b_last + m_prev - m_new)
        w_s = jnp.exp(g_c - m_new)
        kw = kb.astype(F32) * w_s
        yield
        c_ref[p, h] = f_c * cst + lax.dot_general(kw.astype(BF16), vb, TN_DIMS, preferred_element_type=F32)
        n_ref[p, h:h + 1, :] = f_c * nrow + jnp.sum(kw, axis=0, keepdims=True)
        m_ref[p] = jnp.where(lane_row == h, m_new, m_ref[p])
        yield
        o_b = _headnorm(hh, mng_ref[:, vl]) * _sigmoid(bo_ref[p, rows, vl].astype(F32))
        yield
        mix = (_sigmoid(ga_ref[p, rows, vl].astype(F32)) * o_a
               + _sigmoid(gb_ref[p, rows, vl].astype(F32)) * o_b)
        mix_scr[p, rows, vl] = mix.astype(BF16)

    def chunk(c, carry):
        rows = pl.ds(pl.multiple_of(c * L, L), L)
        shared = [gates(p, c, rows) for p in range(npar)]
        for h0 in range(0, GLA_HEADS, HEADS_PER_ROUND):
            units = [head(p, h, rows, shared[p]) for h in range(h0, h0 + HEADS_PER_ROUND) for p in range(npar)]
            while units:
                for u in list(units):
                    if next(u, "done") == "done":
                        units.remove(u)
        return carry

    lax.fori_loop(0, tc // L, chunk, 0)

    for p in range(npar):
        x1_ref[p] = x_ref[p] + jnp.dot(mix_scr[p], wout_ref[...], preferred_element_type=F32)

    @pl.when(i == nblk - 1)
    def _fin():
        for p in range(npar):
            for h in range(GLA_HEADS):
                gla_ref[p, h] = st_scr[p, h].T


def _mixer_prompt(x, big, sm, gtc, lw, bsz, seq, tc, npar):
    nblk = seq // tc

    def seg(width, idx):
        return pl.BlockSpec((npar, tc, width), lambda b, i: (b, i, idx))

    const = lambda shape: pl.BlockSpec(shape, lambda b, i: tuple(0 for _ in shape))
    in_specs = [
        seg(D_MODEL, 0),
        seg(GLA_QK, 0), seg(GLA_QK, 1),
        seg(D_MODEL, 1), seg(D_MODEL, 2),
        seg(D_MODEL, 3), seg(D_MODEL, 4), seg(D_MODEL, 5),
        seg(D_MODEL, 6), seg(D_MODEL, 7), seg(D_MODEL, 8),
        seg(SMALL_COLS, 0),
        pl.BlockSpec((npar, tc // CHUNK, 8, CHUNK), lambda b, i: (b, i, 0, 0)),
        const((LANES, GLA_QK)), const((1, GLA_QK)), const((1, D_MODEL)),
        const((CONV_W, 2 * ML_W)), const((1, 2 * ML_W)),
        const((1, 2 * LANES)), const((8, 1)), const((1, D_MODEL)),
        const((D_MODEL, D_MODEL)),
    ]
    out_specs = [
        seg(D_MODEL, 0),
        pl.BlockSpec((npar, GLA_HEADS, GLA_DK, GLA_DV), lambda b, i: (b, 0, 0, 0)),
        pl.BlockSpec((npar, ML_HEADS, ML_DH, ML_DH), lambda b, i: (b, 0, 0, 0)),
        pl.BlockSpec((npar, ML_HEADS, ML_DH), lambda b, i: (b, 0, 0)),
        pl.BlockSpec((npar, 1, LANES), lambda b, i: (b, 0, 0)),
        pl.BlockSpec((npar, CONV_W - 1, 2 * ML_W), lambda b, i: (b, 0, 0)),
    ]
    out_shape = [
        jax.ShapeDtypeStruct((bsz, seq, D_MODEL), F32),
        jax.ShapeDtypeStruct((bsz, GLA_HEADS, GLA_DK, GLA_DV), F32),
        jax.ShapeDtypeStruct((bsz, ML_HEADS, ML_DH, ML_DH), F32),
        jax.ShapeDtypeStruct((bsz, ML_HEADS, ML_DH), F32),
        jax.ShapeDtypeStruct((bsz, 1, LANES), F32),
        jax.ShapeDtypeStruct((bsz, CONV_W - 1, 2 * ML_W), F32),
    ]
    scratch = [
        pltpu.VMEM((npar, GLA_HEADS, GLA_DV, GLA_DK), F32),
        pltpu.VMEM((npar, tc + 8, 2 * ML_W), F32),
        pltpu.VMEM((npar, tc, 2 * ML_W), BF16),
        pltpu.VMEM((npar, tc, D_MODEL), BF16),
    ]
    return pl.pallas_call(
        functools.partial(_mixer_prompt_kernel, tc=tc, nblk=nblk, npar=npar),
        grid=(bsz // npar, nblk),
        in_specs=in_specs, out_specs=out_specs, out_shape=out_shape,
        scratch_shapes=scratch,
        compiler_params=_params(("arbitrary", "arbitrary")),
        name="mixer_prompt",
    )(x, big, big, big, big, big, big, big, big, big, big, sm, gtc,
      lw["w2p"], lw["b2"], lw["gng"], lw["cw"], lw["cb"], lw["gbr"], lw["gbc"], lw["mng"], lw["wout"])


def _col_bcast(row):
    return jnp.broadcast_to(row, (LANES, LANES)).T


def _mixer_sample_kernel(*refs, nb, has_prev, sps):
    n_in = 18 if has_prev else 16
    (x_ref, big_ref, sm_ref, gla_in, c_in, n_ref, m_ref, conv_ref,
     w2_ref, b2_ref, gng_ref, cw_ref, cb_ref, gbr_ref, mng_ref, wout_ref) = refs[:16]
    x1_ref, gla_out, c_out, n_o, m_o, conv_o = refs[n_in:n_in + 6]
    a_s, qa_s, ka_s, va_s, kw_s, qb_s, vb_s, fc_s, den_s, oa_s, num_s = refs[n_in + 6:]
    gla_ref, c_ref, gla_o, c_o = gla_in.at[0], c_in.at[0], gla_out.at[0], c_out.at[0]
    b = pl.program_id(0)
    seg = lambda idx: slice(idx * D_MODEL, (idx + 1) * D_MODEL)
    lane = lax.broadcasted_iota(jnp.int32, (nb, LANES), 1)

    @pl.when(b == 0)
    def _prep():
        sm = sm_ref[...]
        z = jnp.dot(sm[:, 0:LANES], w2_ref[...], precision=HIGHEST, preferred_element_type=F32) + b2_ref[...]
        a_s[...] = jnp.exp(_log_sigmoid(z) * (1.0 / GLA_TAU))
        qa_s[...] = big_ref[:, 0:GLA_QK].astype(F32) * (GLA_DK ** -0.5)
        ka_s[...] = big_ref[:, GLA_QK:2 * GLA_QK].astype(F32)
        va_s[...] = big_ref[:, seg(1)].astype(F32)
        vb_s[...] = big_ref[:, seg(5)].astype(F32)
        xq = big_ref[:, seg(3)].astype(F32)
        xk = big_ref[:, seg(4)].astype(F32)
        w = 2 * ML_W
        y = cb_ref[...]
        for j in range(CONV_W - 1):
            y = y + cw_ref[j:j + 1, :] * conv_ref[:, j * w:(j + 1) * w]
        y = y + cw_ref[CONV_W - 1:CONV_W, :] * jnp.concatenate([xq, xk], axis=1)
        qk = _silu(y)
        qb = qk[:, 0:ML_W]
        kb = qk[:, ML_W:w] * (ML_DH ** -0.5)
        qb_s[...] = qb
        conv_o[:, 0:2 * w] = conv_ref[:, w:3 * w]
        conv_o[:, 2 * w:2 * w + ML_W] = xq
        conv_o[:, 2 * w + ML_W:3 * w] = xk

        li = sm[:, LANES:2 * LANES] + gbr_ref[:, 0:LANES]
        lf = _log_sigmoid(sm[:, 2 * LANES:3 * LANES] + gbr_ref[:, LANES:2 * LANES])
        m_old = m_ref[...]
        m_new = jnp.maximum(lf + m_old, li)
        f_c = jnp.exp(lf + m_old - m_new)
        w_s = jnp.exp(li - m_new)
        m_o[...] = m_new
        fc_s[...] = f_c
        den = jnp.zeros((nb, LANES), F32)
        for h in range(ML_HEADS):
            vl = slice(h * ML_DH, (h + 1) * ML_DH)
            kw = kb[:, vl] * _lane_col(w_s, h)
            kw_s[:, vl] = kw
            n_new = _lane_col(f_c, h) * n_ref[:, vl] + kw
            n_o[:, vl] = n_new
            den = jnp.where(lane == h, jnp.sum(qb[:, vl] * n_new, axis=1, keepdims=True), den)
        den_s[...] = den
        oa_s[...] = jnp.zeros_like(oa_s)
        num_s[...] = jnp.zeros_like(num_s)

    tile = pl.ds(pl.multiple_of(lax.shift_right_logical(b * sps, 3) * 8, 8), 8)
    lane_row = lax.broadcasted_iota(jnp.int32, (1, LANES), 1)

    for s in range(sps):
        sub = jnp.bitwise_and(b * sps + s, 7)

        def row_get(ref, cols, sub=sub):
            t = ref[tile, cols]
            pick = lax.broadcasted_iota(jnp.int32, t.shape, 0) == sub
            return jnp.sum(jnp.where(pick, t, 0.0), axis=0, keepdims=True)

        def row_set(ref, cols, val, sub=sub):
            t = ref[tile, cols]
            pick = lax.broadcasted_iota(jnp.int32, t.shape, 0) == sub
            ref[tile, cols] = jnp.where(pick, val, t)

        fc_row = row_get(fc_s, slice(0, LANES))
        for h in range(GLA_HEADS):
            sl = slice(h * GLA_DK, (h + 1) * GLA_DK)
            vl = slice(h * GLA_DV, (h + 1) * GLA_DV)
            a_c = _col_bcast(row_get(a_s, sl))
            k_c = _col_bcast(row_get(ka_s, sl))
            q_c = _col_bcast(row_get(qa_s, sl))
            v_row = row_get(va_s, vl)
            s_old = gla_ref[s, h]
            halves = []
            for p in range(GLA_DV // LANES):
                ls = slice(p * LANES, (p + 1) * LANES)
                s_new = s_old[:, ls] * a_c + k_c * v_row[:, ls]
                gla_o[s, h, :, ls] = s_new
                halves.append(jnp.sum(q_c * s_new, axis=0, keepdims=True))
            row_set(oa_s, vl, jnp.concatenate(halves, axis=1))

            f_c = jnp.sum(jnp.where(lane_row == h, fc_row, 0.0), axis=1, keepdims=True)
            vb_row = row_get(vb_s, vl)
            acc = [jnp.zeros((1, LANES), F32) for _ in range(ML_DH // LANES)]
            for r in range(ML_DH // LANES):
                rs = slice(r * LANES, (r + 1) * LANES)
                ks = slice(h * ML_DH + r * LANES, h * ML_DH + (r + 1) * LANES)
                kw_c = _col_bcast(row_get(kw_s, ks))
                qb_c = _col_bcast(row_get(qb_s, ks))
                for p in range(ML_DH // LANES):
                    ls = slice(p * LANES, (p + 1) * LANES)
                    c_new = f_c * c_ref[s, h, rs, ls] + kw_c * vb_row[:, ls]
                    c_o[s, h, rs, ls] = c_new
                    acc[p] = acc[p] + jnp.sum(qb_c * c_new, axis=0, keepdims=True)
            row_set(num_s, vl, jnp.concatenate(acc, axis=1))

    @pl.when(b == nb // sps - 1)
    def _post():
        m_new = m_o[...]
        den = den_s[...]
        for h in range(GLA_HEADS):
            vl = slice(h * GLA_DV, (h + 1) * GLA_DV)
            o_a = _headnorm(oa_s[:, vl], gng_ref[:, vl]) * _silu(big_ref[:, D_MODEL * 2 + h * GLA_DV:
                                                                       D_MODEL * 2 + (h + 1) * GLA_DV].astype(F32))
            dn = jnp.maximum(jnp.abs(_lane_col(den, h)), jnp.exp(-_lane_col(m_new, h)))
            hh = num_s[:, vl] / dn
            o_b = _headnorm(hh, mng_ref[:, vl]) * _sigmoid(
                big_ref[:, 6 * D_MODEL + h * ML_DH:6 * D_MODEL + (h + 1) * ML_DH].astype(F32))
            ga = big_ref[:, 7 * D_MODEL + h * ML_DH:7 * D_MODEL + (h + 1) * ML_DH].astype(F32)
            gb = big_ref[:, 8 * D_MODEL + h * ML_DH:8 * D_MODEL + (h + 1) * ML_DH].astype(F32)
            oa_s[:, vl] = _sigmoid(ga) * o_a + _sigmoid(gb) * o_b
        x1_ref[...] = x_ref[...] + jnp.dot(oa_s[...].astype(BF16), wout_ref[...], preferred_element_type=F32)


def _mixer_sample(x, big, sm, st_gla, st_c, st_n, st_m, st_conv, lw, li, prev_gla, prev_c):
    nb = x.shape[0]
    has_prev = prev_gla is not None
    full = lambda shape: pl.BlockSpec(shape, lambda b: tuple(0 for _ in shape))
    sps = SAMPLES_PER_STEP
    gla_spec = pl.BlockSpec((1, sps, GLA_HEADS, GLA_DK, GLA_DV), lambda b: (li, b, 0, 0, 0))
    c_spec = pl.BlockSpec((1, sps, ML_HEADS, ML_DH, ML_DH), lambda b: (li, b, 0, 0, 0))
    in_specs = [
        full((nb, D_MODEL)), full((nb, BIG_COLS)), full((nb, SMALL_COLS)),
        gla_spec, c_spec,
        full((nb, ML_W)), full((nb, LANES)), full((nb, 3 * 2 * ML_W)),
        full((LANES, GLA_QK)), full((1, GLA_QK)), full((1, D_MODEL)),
        full((CONV_W, 2 * ML_W)), full((1, 2 * ML_W)), full((1, 2 * LANES)), full((1, D_MODEL)),
        full((D_MODEL, D_MODEL)),
    ]
    operands = [x, big, sm, st_gla, st_c, st_n, st_m, st_conv,
                lw["w2p"], lw["b2"], lw["gng"], lw["cw"], lw["cb"], lw["gbr"], lw["mng"], lw["wout"]]
    aliases = {}
    if has_prev:
        in_specs += [pl.BlockSpec(memory_space=pl.ANY), pl.BlockSpec(memory_space=pl.ANY)]
        operands += [prev_gla, prev_c]
        aliases = {16: 1, 17: 2}
    out_specs = [
        full((nb, D_MODEL)),
        gla_spec, c_spec,
        full((nb, ML_W)), full((nb, LANES)), full((nb, 3 * 2 * ML_W)),
    ]
    out_shape = [
        jax.ShapeDtypeStruct((nb, D_MODEL), F32),
        jax.ShapeDtypeStruct(st_gla.shape, F32),
        jax.ShapeDtypeStruct(st_c.shape, F32),
        jax.ShapeDtypeStruct((nb, ML_W), F32),
        jax.ShapeDtypeStruct((nb, LANES), F32),
        jax.ShapeDtypeStruct((nb, 3 * 2 * ML_W), F32),
    ]
    vm = lambda cols: pltpu.VMEM((nb, cols), F32)
    scratch = [vm(GLA_QK), vm(GLA_QK), vm(GLA_QK), vm(D_MODEL), vm(ML_W), vm(ML_W), vm(ML_W),
               vm(LANES), vm(LANES), vm(D_MODEL), vm(ML_W)]
    return pl.pallas_call(
        functools.partial(_mixer_sample_kernel, nb=nb, has_prev=has_prev, sps=sps),
        grid=(nb // sps,),
        in_specs=in_specs, out_specs=out_specs, out_shape=out_shape,
        scratch_shapes=scratch,
        input_output_aliases=aliases,
        compiler_params=_params(("arbitrary",)),
        name="mixer_sample",
    )(*operands)


def _peer_select_kernel(x_ref, g_ref, wq_ref, sk_ref, h2_ref, rank_ref, p1_ref, cnt_ref, p0_ref,
                        sc_scr, *, tq, cw):
    nch = tq // cw
    hb = _rms(x_ref[...], g_ref[...]).astype(BF16)
    h2_ref[...] = hb
    q = jnp.dot(hb, wq_ref[...], preferred_element_type=F32).astype(BF16)
    for hc in range(2 * PEER_HEADS):
        for ch in range(nch):
            qs = q[ch * cw:(ch + 1) * cw, hc * LANES:(hc + 1) * LANES]
            sc_scr[hc, ch] = lax.dot_general(sk_ref[hc % 2], qs, NT_DIMS, preferred_element_type=F32)

    K = PEER_TOPK
    neg = -jnp.inf
    iota_k = lax.broadcasted_iota(jnp.int32, (K, cw), 0)
    iota_kf = iota_k.astype(F32)

    def unit(idx):
        h = idx // nch
        ch = idx % nch
        s0 = sc_scr[2 * h, ch]
        s1 = sc_scr[2 * h + 1, ch]

        sv0 = jnp.zeros((K, cw), F32)
        prev = jnp.full((1, cw), jnp.inf, F32)
        sv1 = jnp.zeros((K, cw), F32)
        rank1 = jnp.full((N_KEYS, cw), float(K), F32)
        work = s1
        for j in range(K):
            prev = jnp.max(jnp.where(s0 < prev, s0, neg), axis=0, keepdims=True)
            sv0 = jnp.where(iota_k == j, prev, sv0)
            yield
            mx = jnp.max(work, axis=0, keepdims=True)
            hit = work == mx
            rank1 = jnp.where(hit, float(j), rank1)
            work = jnp.where(hit, neg, work)
            sv1 = jnp.where(iota_k == j, mx, sv1)
            yield
        rank_ref[h, ch] = rank1.astype(BF16)
        p1_ref[h, ch] = jnp.exp(s1 - sv1[0:1, :]).astype(BF16)
        yield

        top = sv0[0:1, :] + sv1[0:1, :]
        front = sv0 + sv1[0:1, :]
        cnt = jnp.zeros((K, cw), F32)
        zsum = jnp.zeros((1, cw), F32)
        cntp = jnp.zeros((N_KEYS, cw), F32)
        for _ in range(K):
            fm = jnp.max(front, axis=0, keepdims=True)
            j1 = jnp.min(jnp.where(front == fm, iota_kf, float(K)), axis=0, keepdims=True)
            sel = iota_kf == j1
            zsum = zsum + jnp.exp(fm - top)
            nxt = jnp.sum(jnp.where(sel, cnt, 0.0), axis=0, keepdims=True) + 1.0
            cnt = jnp.where(sel, cnt + 1.0, cnt)
            sv1n = jnp.sum(jnp.where(iota_kf == nxt, sv1, 0.0), axis=0, keepdims=True)
            sv1n = jnp.where(nxt >= float(K), neg, sv1n)
            sv0s = jnp.sum(jnp.where(sel, sv0, 0.0), axis=0, keepdims=True)
            front = jnp.where(sel, sv0s + sv1n, front)
            cntp = jnp.where(s0 == sv0s, cntp + 1.0, cntp)
            yield
        cnt_ref[h, ch] = cntp
        p0_ref[h, ch] = jnp.exp(s0 - sv0[0:1, :]) * (GELU_IN_SCALE / zsum)

    def body(it, carry):
        units = [unit(2 * it), unit(2 * it + 1)]
        while units:
            for u in list(units):
                if next(u, "done") == "done":
                    units.remove(u)
        return carry

    lax.fori_loop(0, PEER_HEADS * nch // 2, body, 0)


def _peer_select(x1, g2, wq, sk, tq, cw):
    t = x1.shape[0]
    nch = tq // cw
    meta = lambda dt: jax.ShapeDtypeStruct((PEER_HEADS, t // cw, N_KEYS, cw), dt)
    mspec = pl.BlockSpec((PEER_HEADS, nch, N_KEYS, cw), lambda i: (0, i, 0, 0))
    return pl.pallas_call(
        functools.partial(_peer_select_kernel, tq=tq, cw=cw),
        grid=(t // tq,),
        in_specs=[
            pl.BlockSpec((tq, D_MODEL), lambda i: (i, 0)),
            pl.BlockSpec((1, D_MODEL), lambda i: (0, 0)),
            pl.BlockSpec((D_MODEL, PEER_HEADS * PEER_DQ), lambda i: (0, 0)),
            pl.BlockSpec((2, N_KEYS, PEER_DQ // 2), lambda i: (0, 0, 0)),
        ],
        out_specs=[pl.BlockSpec((tq, D_MODEL), lambda i: (i, 0)), mspec, mspec, mspec, mspec],
        out_shape=[jax.ShapeDtypeStruct((t, D_MODEL), BF16), meta(BF16), meta(BF16), meta(F32), meta(F32)],
        scratch_shapes=[pltpu.VMEM((2 * PEER_HEADS, nch, N_KEYS, cw), F32)],
        compiler_params=_params(("arbitrary",)),
        name="peer_select",
    )(x1, g2, wq, sk)


def _peer_dense_kernel(x_ref, h2_ref, rank_ref, p1_ref, cnt_ref, p0_ref, u_ref, vt_ref, fg_ref,
                       out_ref, acc_scr, g_scr, ht_scr, *, tb, eb, ec, cw, nblk_e, final_norm):
    ngrp = eb // ec
    j = pl.program_id(1)

    @pl.when(j == 0)
    def _():
        acc_scr[...] = jnp.zeros_like(acc_scr)

    for grp in range(ngrp):
        es = slice(grp * ec, (grp + 1) * ec)
        u_grp = pltpu.bitcast(u_ref[grp * (ec // 2):(grp + 1) * (ec // 2), :], BF16)
        ht_scr[grp] = lax.dot_general(u_grp, h2_ref[...], NT_DIMS, preferred_element_type=F32)
    for grp in range(ngrp):
        es = slice(grp * ec, (grp + 1) * ec)
        for k in range(ec // N_KEYS):
            r = grp * (ec // N_KEYS) + k
            ks = slice(grp * ec + k * N_KEYS, grp * ec + (k + 1) * N_KEYS)
            for ch in range(tb // cw):
                cs = slice(ch * cw, (ch + 1) * cw)
                w = jnp.zeros((N_KEYS, cw), BF16)
                for h in range(PEER_HEADS):
                    cnt_row = jnp.broadcast_to(cnt_ref[h, ch, r:r + 1, :], (N_KEYS, cw)).astype(BF16)
                    p0_row = jnp.broadcast_to(p0_ref[h, ch, r:r + 1, :], (N_KEYS, cw)).astype(BF16)
                    w = w + jnp.where(rank_ref[h, ch] < cnt_row, p1_ref[h, ch] * p0_row, jnp.zeros_like(w))
                z = ht_scr[grp, k * N_KEYS:(k + 1) * N_KEYS, cs]
                g_scr[ks, cs] = w * (z * (1.0 + lax.erf(z))).astype(BF16)
        acc_scr[...] += jnp.dot(pltpu.bitcast(vt_ref[:, es], BF16), g_scr[es, :],
                                preferred_element_type=F32)

    @pl.when(j == nblk_e - 1)
    def _():
        y = x_ref[...] + acc_scr[...].T
        if final_norm:
            y = _rms(y, fg_ref[...])
        out_ref[...] = y


def _peer_dense(x1, h2, rank, p1, cnt, p0, u_bf, vt_bf, fg, tb, eb, cw, final_norm):
    t = x1.shape[0]
    nch = tb // cw
    nblk_e = N_EXPERTS // eb
    ec = min(eb, 512)
    mspec = pl.BlockSpec((PEER_HEADS, nch, N_KEYS, cw), lambda i, j: (0, i, 0, 0))
    rspec = pl.BlockSpec((PEER_HEADS, nch, eb // N_KEYS, cw), lambda i, j: (0, i, j, 0))
    return pl.pallas_call(
        functools.partial(_peer_dense_kernel, tb=tb, eb=eb, ec=ec, cw=cw, nblk_e=nblk_e,
                          final_norm=final_norm),
        grid=(t // tb, nblk_e),
        in_specs=[
            pl.BlockSpec((tb, D_MODEL), lambda i, j: (i, 0)),
            pl.BlockSpec((tb, D_MODEL), lambda i, j: (i, 0)),
            mspec, mspec, rspec, rspec,
            pl.BlockSpec((eb // 2, D_MODEL), lambda i, j: (j, 0)),
            pl.BlockSpec((D_MODEL // 2, eb), lambda i, j: (0, j)),
            pl.BlockSpec((1, D_MODEL), lambda i, j: (0, 0)),
        ],
        out_specs=pl.BlockSpec((tb, D_MODEL), lambda i, j: (i, 0)),
        out_shape=jax.ShapeDtypeStruct((t, D_MODEL), F32),
        scratch_shapes=[pltpu.VMEM((D_MODEL, tb), F32), pltpu.VMEM((eb, tb), BF16),
                        pltpu.VMEM((eb // ec, ec, tb), F32)],
        compiler_params=_params(("arbitrary", "arbitrary")),
        name="peer_dense",
    )(x1, h2, rank, p1, cnt, p0, u_bf, vt_bf, fg)


def _layer_weights(li, w_in, gla_w2, gla_b2, gla_norm_g, conv_w, conv_b, ml_i_b, ml_f_b, ml_norm_g,
                   w_out, peer_wq, peer_subkeys, peer_u, peer_v):
    w = w_in[li]
    o = 0
    segs = {}
    for name, width in (("aq", GLA_QK), ("ak", GLA_QK), ("av", D_MODEL), ("ag", D_MODEL),
                        ("alr", GLA_GATE_RANK), ("bq", ML_W), ("bk", ML_W), ("bv", ML_W),
                        ("bi", ML_HEADS), ("bf", ML_HEADS), ("bo", ML_W),
                        ("ga", D_MODEL), ("gb", D_MODEL)):
        segs[name] = w[:, o:o + width]
        o += width
    seg_w = WSEG_BLOCKS * D_MODEL
    starts = (0, seg_w + GLA_GATE_RANK, 2 * seg_w + GLA_GATE_RANK + 2 * ML_HEADS)
    w_segs = tuple(w[:, s:s + seg_w].astype(BF16) for s in starts)
    pad = lambda a: jnp.pad(a, ((0, 0), (0, LANES - a.shape[1])))
    w_small = jnp.concatenate([pad(segs["alr"]), pad(segs["bi"]), pad(segs["bf"])], axis=1).astype(BF16)
    w_gt = jnp.concatenate([segs["bi"], segs["bf"]], axis=1).T.astype(BF16)
    gate_b = jnp.concatenate([ml_i_b[li], ml_f_b[li]])
    return dict(
        w_segs=w_segs, w_small=w_small, w_gt=w_gt,
        w2p=jnp.pad(gla_w2[li], ((0, LANES - GLA_GATE_RANK), (0, 0))),
        b2=gla_b2[li][None, :], gng=gla_norm_g[li][None, :],
        cw=conv_w[li], cb=conv_b[li][None, :],
        gbr=jnp.concatenate([jnp.pad(ml_i_b[li], (0, LANES - ML_HEADS)),
                             jnp.pad(ml_f_b[li], (0, LANES - ML_HEADS))])[None, :],
        gbc=gate_b[:, None], mng=ml_norm_g[li][None, :],
        wout=w_out[li].astype(BF16),
        wq=peer_wq[li].astype(BF16), sk=peer_subkeys[li].astype(BF16),
        u=_pack_row_pairs((peer_u[li] * GELU_IN_SCALE).astype(BF16)),
        vt=_pack_row_pairs(peer_v[li].T.astype(BF16)),
    )


def _pack_row_pairs(a):
    m, n = a.shape
    return lax.bitcast_convert_type(jnp.swapaxes(a.reshape(m // 2, 2, n), -1, -2), jnp.uint32)


def _pick(n, cands):
    for c in cands:
        if n % c == 0:
            return c
    return n


def _peer(x1, g2, lw, fg, final_norm):
    t = x1.shape[0]
    tq = _pick(t, (512, 256, 128))
    tb = _pick(t, (512, 256, 128))
    cw = _pick(tb, (2 * LANES, LANES))
    h2, rank, p1, cnt, p0 = _peer_select(x1, g2, lw["wq"], lw["sk"], tq, cw)
    return _peer_dense(x1, h2, rank, p1, cnt, p0, lw["u"], lw["vt"], fg, tb, 2048, cw, final_norm)


def kernel(x_prompt, x_sample, state_gla, state_mlstm_c, state_mlstm_n, state_mlstm_m, state_conv,
           norm1_g, w_in, gla_w2, gla_b2, gla_norm_g, conv_w, conv_b, ml_i_b, ml_f_b, ml_norm_g,
           w_out, norm2_g, peer_wq, peer_subkeys, peer_u, peer_v, final_g):
    depth = w_in.shape[0]
    bsz, seq, _ = x_prompt.shape
    nb = x_sample.shape[0]
    assert seq % CHUNK == 0 and x_sample.shape[1] == 1 and nb % LANES == 0
    tp = bsz * seq
    tm = _pick(tp, (2048, 1024, 512, 256, 128))
    tc = _pick(seq, (256, 128, 64))
    npar = _pick(bsz, (2, 1))
    fg = final_g[None, :]

    xp = x_prompt.reshape(tp, D_MODEL)
    xs = x_sample.reshape(nb, D_MODEL)
    p_out = [[] for _ in range(5)]
    s_out = [[] for _ in range(5)]
    s_gla = s_c = None
    for li in range(depth):
        lw = _layer_weights(li, w_in, gla_w2, gla_b2, gla_norm_g, conv_w, conv_b, ml_i_b, ml_f_b,
                            ml_norm_g, w_out, peer_wq, peer_subkeys, peer_u, peer_v)
        g1 = norm1_g[li][None, :]
        g2 = norm2_g[li][None, :]
        last = li == depth - 1

        big, sm, gt = _inproj(xp, g1, lw["w_segs"], lw["w_small"], lw["w_gt"], tm)
        gtc = gt.reshape(8, bsz, seq // CHUNK, CHUNK).transpose(1, 2, 0, 3)
        x1, gla, c, n, m, conv = _mixer_prompt(
            xp.reshape(bsz, seq, D_MODEL), big.reshape(bsz, seq, BIG_COLS),
            sm.reshape(bsz, seq, SMALL_COLS), gtc, lw, bsz, seq, tc, npar)
        xp = _peer(x1.reshape(tp, D_MODEL), g2, lw, fg, last)
        for lst, val in zip(p_out, (gla, c, n, m[:, 0, :ML_HEADS], conv)):
            lst.append(val)

        big, sm, _ = _inproj(xs, g1, lw["w_segs"], lw["w_small"], lw["w_gt"], nb)
        x1, s_gla, s_c, n, m, conv = _mixer_sample(
            xs, big, sm, state_gla, state_mlstm_c,
            state_mlstm_n[li].reshape(nb, ML_W),
            jnp.pad(state_mlstm_m[li], ((0, 0), (0, LANES - ML_HEADS))),
            state_conv[li].reshape(nb, (CONV_W - 1) * 2 * ML_W), lw, li, s_gla, s_c)
        xs = _peer(x1, g2, lw, fg, last)
        for lst, val in zip(s_out[2:], (n.reshape(nb, ML_HEADS, ML_DH), m[:, :ML_HEADS],
                                        conv.reshape(nb, CONV_W - 1, 2 * ML_W))):
            lst.append(val)

    y_prompt = xp.reshape(bsz, seq, D_MODEL)
    y_sample = xs.reshape(nb, 1, D_MODEL)
    return (y_prompt, y_sample, *[jnp.stack(v) for v in p_out],
            s_gla, s_c, *[jnp.stack(v) for v in s_out[2:]])
```

```python
import functools

import jax
import jax.numpy as jnp
from jax import lax
from jax.experimental import pallas as pl
from jax.experimental.pallas import tpu as pltpu

F32 = jnp.float32
BF16 = jnp.bfloat16
HIGHEST = lax.Precision.HIGHEST

D_MODEL = 1024
GLA_HEADS = 4
GLA_DK = 128
GLA_DV = 256
GLA_GATE_RANK = 16
GLA_TAU = 16.0
GLA_QK = GLA_HEADS * GLA_DK
ML_HEADS = 4
ML_DH = 256
ML_W = ML_HEADS * ML_DH
CONV_W = 4
CHUNK = 64
PEER_HEADS = 8
PEER_DQ = 256
N_KEYS = 128
N_EXPERTS = N_KEYS * N_KEYS
PEER_TOPK = 16
EPS = 1e-6
GELU_IN_SCALE = 0.7071067811865476

LANES = 128
BIG_COLS = 9 * D_MODEL
WSEG_BLOCKS = 3
SAMPLES_PER_STEP = 2
HEADS_PER_ROUND = 4
SMALL_COLS = 3 * LANES
VMEM_LIMIT_BYTES = 56 * 1024 * 1024

NT_DIMS = (((1,), (1,)), ((), ()))
TN_DIMS = (((0,), (0,)), ((), ()))


def _params(sem, flags=None):
    return pltpu.CompilerParams(dimension_semantics=sem, vmem_limit_bytes=VMEM_LIMIT_BYTES, flags=flags)


def _log_sigmoid(x):
    return jnp.minimum(x, 0.0) - jnp.log1p(jnp.exp(-jnp.abs(x)))


def _sigmoid(x):
    return 1.0 / (1.0 + jnp.exp(-x))


def _silu(x):
    return x * _sigmoid(x)


def _rms(x, g):
    return x * lax.rsqrt(jnp.mean(x * x, axis=-1, keepdims=True) + EPS) * g


def _lane_col(a, j):
    lane = lax.broadcasted_iota(jnp.int32, a.shape, 1)
    return jnp.sum(jnp.where(lane == j, a, 0.0), axis=1, keepdims=True)


def _inproj_kernel(x_ref, g_ref, wa_ref, wb_ref, wc_ref, wsm_ref, wgt_ref, big_ref, sm_ref, gt_ref, h_scr):
    j = pl.program_id(1)

    @pl.when(j == 0)
    def _():
        hb = _rms(x_ref[...], g_ref[...]).astype(BF16)
        h_scr[...] = hb
        sm_ref[...] = jnp.dot(hb, wsm_ref[...], preferred_element_type=F32)
        gt_ref[...] = lax.dot_general(wgt_ref[...], hb, NT_DIMS, preferred_element_type=F32)

    for s, w_ref in enumerate((wa_ref, wb_ref, wc_ref)):
        @pl.when((j >= s * WSEG_BLOCKS) & (j < (s + 1) * WSEG_BLOCKS))
        def _(w_ref=w_ref):
            big_ref[...] = jnp.dot(h_scr[...], w_ref[...], preferred_element_type=F32).astype(BF16)


def _inproj(x, g, w_segs, w_small, w_gt, tm):
    t = x.shape[0]
    tn = D_MODEL

    def wspec(s):
        return pl.BlockSpec((D_MODEL, tn), lambda i, j: (0, jnp.clip(j - s * WSEG_BLOCKS, 0, WSEG_BLOCKS - 1)))

    return pl.pallas_call(
        _inproj_kernel,
        grid=(t // tm, BIG_COLS // tn),
        in_specs=[
            pl.BlockSpec((tm, D_MODEL), lambda i, j: (i, 0)),
            pl.BlockSpec((1, D_MODEL), lambda i, j: (0, 0)),
            wspec(0), wspec(1), wspec(2),
            pl.BlockSpec((D_MODEL, SMALL_COLS), lambda i, j: (0, 0)),
            pl.BlockSpec((8, D_MODEL), lambda i, j: (0, 0)),
        ],
        out_specs=[
            pl.BlockSpec((tm, tn), lambda i, j: (i, j)),
            pl.BlockSpec((tm, SMALL_COLS), lambda i, j: (i, 0)),
            pl.BlockSpec((8, tm), lambda i, j: (0, i)),
        ],
        out_shape=[
            jax.ShapeDtypeStruct((t, BIG_COLS), BF16),
            jax.ShapeDtypeStruct((t, SMALL_COLS), F32),
            jax.ShapeDtypeStruct((8, t), F32),
        ],
        scratch_shapes=[pltpu.VMEM((tm, D_MODEL), BF16)],
        compiler_params=_params(("arbitrary", "arbitrary")),
        name="inproj",
    )(x, g, *w_segs, w_small, w_gt)


def _headnorm(o, g_row):
    return o * lax.rsqrt(jnp.mean(o * o, axis=-1, keepdims=True) + EPS) * g_row


def _mixer_prompt_kernel(x_ref, aq_ref, ak_ref, av_ref, ag_ref, bq_ref, bk_ref, bv_ref, bo_ref,
                         ga_ref, gb_ref, sm_ref, gt_ref, w2_ref, b2_ref, gng_ref, cw_ref, cb_ref,
                         gbr_ref, gbc_ref, mng_ref, wout_ref,
                         x1_ref, gla_ref, c_ref, n_ref, m_ref, conv_ref,
                         st_scr, xp_scr, qk_scr, mix_scr, *, tc, nblk, npar):
    i = pl.program_id(1)
    L = CHUNK

    @pl.when(i == 0)
    def _init():
        st_scr[...] = jnp.zeros_like(st_scr)
        c_ref[...] = jnp.zeros_like(c_ref)
        n_ref[...] = jnp.zeros_like(n_ref)
        m_ref[...] = jnp.zeros_like(m_ref)
        xp_scr[:, 0:8, :] = jnp.zeros((npar, 8, 2 * ML_W), F32)

    for p in range(npar):
        xp_scr[p, 8:8 + tc, 0:ML_W] = bq_ref[p].astype(F32)
        xp_scr[p, 8:8 + tc, ML_W:2 * ML_W] = bk_ref[p].astype(F32)
        y = cb_ref[...]
        for j in range(CONV_W):
            y = y + cw_ref[j:j + 1, :] * xp_scr[p, 5 + j:5 + j + tc, :]
        qk = _silu(y)
        qk_scr[p, :, 0:ML_W] = qk[:, 0:ML_W].astype(BF16)
        qk_scr[p, :, ML_W:2 * ML_W] = (qk[:, ML_W:2 * ML_W] * (ML_DH ** -0.5)).astype(BF16)
        xp_scr[p, 5:8, :] = xp_scr[p, tc + 5:tc + 8, :]

    @pl.when(i == nblk - 1)
    def _():
        conv_ref[...] = xp_scr[:, 5:8, :]

    row = lax.broadcasted_iota(jnp.int32, (L, L), 0)
    col = lax.broadcasted_iota(jnp.int32, (L, L), 1)
    causal = col <= row
    tril = causal.astype(F32)
    triu = (row <= col).astype(F32)
    lane_row = lax.broadcasted_iota(jnp.int32, (1, LANES), 1)

    def gates(p, c, rows):
        sm = sm_ref[p, rows, :]
        z = jnp.dot(sm[:, 0:LANES], w2_ref[...], precision=HIGHEST, preferred_element_type=F32) + b2_ref[...]
        log_a = _log_sigmoid(z) * (1.0 / GLA_TAU)
        bc = jnp.dot(tril, log_a, precision=HIGHEST, preferred_element_type=F32)
        li_c_all = sm[:, LANES:2 * LANES] + gbr_ref[:, 0:LANES]
        lf_c_all = _log_sigmoid(sm[:, 2 * LANES:3 * LANES] + gbr_ref[:, LANES:2 * LANES])
        bcum_c_all = jnp.dot(tril, lf_c_all, precision=HIGHEST, preferred_element_type=F32)
        gtb = gt_ref[p, c] + gbc_ref[...]
        bcum_r_all = jnp.dot(_log_sigmoid(gtb), triu, precision=HIGHEST, preferred_element_type=F32)
        return bc, li_c_all, bcum_c_all, gtb, bcum_r_all

    def head(p, h, rows, shared):
        bc, li_c_all, bcum_c_all, gtb, bcum_r_all = shared
        sl = slice(h * GLA_DK, (h + 1) * GLA_DK)
        vl = slice(h * GLA_DV, (h + 1) * GLA_DV)
        b = bc[:, sl]
        mid = b[L // 2 - 1:L // 2, :]
        bl = b[L - 1:L, :]
        q = aq_ref[p, rows, sl].astype(F32) * (GLA_DK ** -0.5)
        k = ak_ref[p, rows, sl].astype(F32)
        v = av_ref[p, rows, vl]
        q_in = (q * jnp.exp(b)).astype(BF16)
        q_at = (q * jnp.exp(b - mid)).astype(BF16)
        yield
        k_at = (k * jnp.exp(mid - b)).astype(BF16)
        k_out = (k * jnp.exp(bl - b)).astype(BF16)
        yield
        att = lax.dot_general(q_at, k_at, NT_DIMS, preferred_element_type=F32)
        att = jnp.where(causal, att, 0.0)
        st = st_scr[p, h]
        yield
        o = lax.dot_general(q_in, st.astype(BF16), NT_DIMS, preferred_element_type=F32)
        yield
        o = o + jnp.dot(att.astype(BF16), v, preferred_element_type=F32)
        yield
        st_scr[p, h] = st * jnp.exp(bl) + lax.dot_general(v, k_out, TN_DIMS, preferred_element_type=F32)
        yield
        o_a = _headnorm(o, gng_ref[:, vl]) * _silu(ag_ref[p, rows, vl].astype(F32))
        yield

        li_c = _lane_col(li_c_all, h)
        b_c = _lane_col(bcum_c_all, h)
        li_r = gtb[h:h + 1, :]
        b_r = bcum_r_all[ML_HEADS + h:ML_HEADS + h + 1, :]
        m_prev = jnp.sum(jnp.where(lane_row == h, m_ref[p], 0.0), axis=1, keepdims=True)
        a_int = b_c + m_prev
        dm = jnp.where(causal, b_c - b_r + li_r, -jnp.inf)
        m_t = jnp.maximum(a_int, jnp.max(dm, axis=1, keepdims=True))
        yield
        w_int = jnp.exp(a_int - m_t)
        dexp = jnp.exp(dm - m_t)
        qb = qk_scr[p, rows, vl]
        kb = qk_scr[p, rows, ML_W + h * ML_DH:ML_W + (h + 1) * ML_DH]
        vb = bv_ref[p, rows, vl]
        s = lax.dot_general(qb, kb, NT_DIMS, preferred_element_type=F32) * dexp
        yield
        cst = c_ref[p, h]
        num = w_int * jnp.dot(qb, cst.astype(BF16), preferred_element_type=F32)
        yield
        num = num + jnp.dot(s.astype(BF16), vb, preferred_element_type=F32)
        nrow = n_ref[p, h:h + 1, :]
        den = (w_int * jnp.sum(qb.astype(F32) * nrow, axis=1, keepdims=True)
               + jnp.sum(s, axis=1, keepdims=True))
        yield
        hh = num / jnp.maximum(jnp.abs(den), jnp.exp(-m_t))
        b_last = b_c[L - 1:L, :]
        g_c = b_last - b_c + li_c
        m_new = jnp.maximum(b_last + m_prev, jnp.max(g_c, axis=0, keepdims=True))
        f_c = jnp.exp(b_last + m_prev - m_new)
        w_s = jnp.exp(g_c - m_new)
        kw = kb.astype(F32) * w_s
        yield
        c_ref[p, h] = f_c * cst + lax.dot_general(kw.astype(BF16), vb, TN_DIMS, preferred_element_type=F32)
        n_ref[p, h:h + 1, :] = f_c * nrow + jnp.sum(kw, axis=0, keepdims=True)
        m_ref[p] = jnp.where(lane_row == h, m_new, m_ref[p])
        yield
        o_b = _headnorm(hh, mng_ref[:, vl]) * _sigmoid(bo_ref[p, rows, vl].astype(F32))
        yield
        mix = (_sigmoid(ga_ref[p, rows, vl].astype(F32)) * o_a
               + _sigmoid(gb_ref[p, rows, vl].astype(F32)) * o_b)
        mix_scr[p, rows, vl] = mix.astype(BF16)

    def chunk(c, carry):
        rows = pl.ds(pl.multiple_of(c * L, L), L)
        shared = [gates(p, c, rows) for p in range(npar)]
        for h0 in range(0, GLA_HEADS, HEADS_PER_ROUND):
            units = [head(p, h, rows, shared[p]) for h in range(h0, h0 + HEADS_PER_ROUND) for p in range(npar)]
            while units:
                for u in list(units):
                    if next(u, "done") == "done":
                        units.remove(u)
        return carry

    lax.fori_loop(0, tc // L, chunk, 0)

    for p in range(npar):
        x1_ref[p] = x_ref[p] + jnp.dot(mix_scr[p], wout_ref[...], preferred_element_type=F32)

    @pl.when(i == nblk - 1)
    def _fin():
        for p in range(npar):
            for h in range(GLA_HEADS):
                gla_ref[p, h] = st_scr[p, h].T


def _mixer_prompt(x, big, sm, gtc, lw, bsz, seq, tc, npar):
    nblk = seq // tc

    def seg(width, idx):
        return pl.BlockSpec((npar, tc, width), lambda b, i: (b, i, idx))

    const = lambda shape: pl.BlockSpec(shape, lambda b, i: tuple(0 for _ in shape))
    in_specs = [
        seg(D_MODEL, 0),
        seg(GLA_QK, 0), seg(GLA_QK, 1),
        seg(D_MODEL, 1), seg(D_MODEL, 2),
        seg(D_MODEL, 3), seg(D_MODEL, 4), seg(D_MODEL, 5),
        seg(D_MODEL, 6), seg(D_MODEL, 7), seg(D_MODEL, 8),
        seg(SMALL_COLS, 0),
        pl.BlockSpec((npar, tc // CHUNK, 8, CHUNK), lambda b, i: (b, i, 0, 0)),
        const((LANES, GLA_QK)), const((1, GLA_QK)), const((1, D_MODEL)),
        const((CONV_W, 2 * ML_W)), const((1, 2 * ML_W)),
        const((1, 2 * LANES)), const((8, 1)), const((1, D_MODEL)),
        const((D_MODEL, D_MODEL)),
    ]
    out_specs = [
        seg(D_MODEL, 0),
        pl.BlockSpec((npar, GLA_HEADS, GLA_DK, GLA_DV), lambda b, i: (b, 0, 0, 0)),
        pl.BlockSpec((npar, ML_HEADS, ML_DH, ML_DH), lambda b, i: (b, 0, 0, 0)),
        pl.BlockSpec((npar, ML_HEADS, ML_DH), lambda b, i: (b, 0, 0)),
        pl.BlockSpec((npar, 1, LANES), lambda b, i: (b, 0, 0)),
        pl.BlockSpec((npar, CONV_W - 1, 2 * ML_W), lambda b, i: (b, 0, 0)),
    ]
    out_shape = [
        jax.ShapeDtypeStruct((bsz, seq, D_MODEL), F32),
        jax.ShapeDtypeStruct((bsz, GLA_HEADS, GLA_DK, GLA_DV), F32),
        jax.ShapeDtypeStruct((bsz, ML_HEADS, ML_DH, ML_DH), F32),
        jax.ShapeDtypeStruct((bsz, ML_HEADS, ML_DH), F32),
        jax.ShapeDtypeStruct((bsz, 1, LANES), F32),
        jax.ShapeDtypeStruct((bsz, CONV_W - 1, 2 * ML_W), F32),
    ]
    scratch = [
        pltpu.VMEM((npar, GLA_HEADS, GLA_DV, GLA_DK), F32),
        pltpu.VMEM((npar, tc + 8, 2 * ML_W), F32),
        pltpu.VMEM((npar, tc, 2 * ML_W), BF16),
        pltpu.VMEM((npar, tc, D_MODEL), BF16),
    ]
    return pl.pallas_call(
        functools.partial(_mixer_prompt_kernel, tc=tc, nblk=nblk, npar=npar),
        grid=(bsz // npar, nblk),
        in_specs=in_specs, out_specs=out_specs, out_shape=out_shape,
        scratch_shapes=scratch,
        compiler_params=_params(("arbitrary", "arbitrary")),
        name="mixer_prompt",
    )(x, big, big, big, big, big, big, big, big, big, big, sm, gtc,
      lw["w2p"], lw["b2"], lw["gng"], lw["cw"], lw["cb"], lw["gbr"], lw["gbc"], lw["mng"], lw["wout"])


def _col_bcast(row):
    return jnp.broadcast_to(row, (LANES, LANES)).T


def _mixer_sample_kernel(*refs, nb, has_prev, sps):
    n_in = 18 if has_prev else 16
    (x_ref, big_ref, sm_ref, gla_in, c_in, n_ref, m_ref, conv_ref,
     w2_ref, b2_ref, gng_ref, cw_ref, cb_ref, gbr_ref, mng_ref, wout_ref) = refs[:16]
    x1_ref, gla_out, c_out, n_o, m_o, conv_o = refs[n_in:n_in + 6]
    a_s, qa_s, ka_s, va_s, kw_s, qb_s, vb_s, fc_s, den_s, oa_s, num_s = refs[n_in + 6:]
    gla_ref, c_ref, gla_o, c_o = gla_in.at[0], c_in.at[0], gla_out.at[0], c_out.at[0]
    b = pl.program_id(0)
    seg = lambda idx: slice(idx * D_MODEL, (idx + 1) * D_MODEL)
    lane = lax.broadcasted_iota(jnp.int32, (nb, LANES), 1)

    @pl.when(b == 0)
    def _prep():
        sm = sm_ref[...]
        z = jnp.dot(sm[:, 0:LANES], w2_ref[...], precision=HIGHEST, preferred_element_type=F32) + b2_ref[...]
        a_s[...] = jnp.exp(_log_sigmoid(z) * (1.0 / GLA_TAU))
        qa_s[...] = big_ref[:, 0:GLA_QK].astype(F32) * (GLA_DK ** -0.5)
        ka_s[...] = big_ref[:, GLA_QK:2 * GLA_QK].astype(F32)
        va_s[...] = big_ref[:, seg(1)].astype(F32)
        vb_s[...] = big_ref[:, seg(5)].astype(F32)
        xq = big_ref[:, seg(3)].astype(F32)
        xk = big_ref[:, seg(4)].astype(F32)
        w = 2 * ML_W
        y = cb_ref[...]
        for j in range(CONV_W - 1):
            y = y + cw_ref[j:j + 1, :] * conv_ref[:, j * w:(j + 1) * w]
        y = y + cw_ref[CONV_W - 1:CONV_W, :] * jnp.concatenate([xq, xk], axis=1)
        qk = _silu(y)
        qb = qk[:, 0:ML_W]
        kb = qk[:, ML_W:w] * (ML_DH ** -0.5)
        qb_s[...] = qb
        conv_o[:, 0:2 * w] = conv_ref[:, w:3 * w]
        conv_o[:, 2 * w:2 * w + ML_W] = xq
        conv_o[:, 2 * w + ML_W:3 * w] = xk

        li = sm[:, LANES:2 * LANES] + gbr_ref[:, 0:LANES]
        lf = _log_sigmoid(sm[:, 2 * LANES:3 * LANES] + gbr_ref[:, LANES:2 * LANES])
        m_old = m_ref[...]
        m_new = jnp.maximum(lf + m_old, li)
        f_c = jnp.exp(lf + m_old - m_new)
        w_s = jnp.exp(li - m_new)
        m_o[...] = m_new
        fc_s[...] = f_c
        den = jnp.zeros((nb, LANES), F32)
        for h in range(ML_HEADS):
            vl = slice(h * ML_DH, (h + 1) * ML_DH)
            kw = kb[:, vl] * _lane_col(w_s, h)
            kw_s[:, vl] = kw
            n_new = _lane_col(f_c, h) * n_ref[:, vl] + kw
            n_o[:, vl] = n_new
            den = jnp.where(lane == h, jnp.sum(qb[:, vl] * n_new, axis=1, keepdims=True), den)
        den_s[...] = den
        oa_s[...] = jnp.zeros_like(oa_s)
        num_s[...] = jnp.zeros_like(num_s)

    tile = pl.ds(pl.multiple_of(lax.shift_right_logical(b * sps, 3) * 8, 8), 8)
    lane_row = lax.broadcasted_iota(jnp.int32, (1, LANES), 1)

    for s in range(sps):
        sub = jnp.bitwise_and(b * sps + s, 7)

        def row_get(ref, cols, sub=sub):
            t = ref[tile, cols]
            pick = lax.broadcasted_iota(jnp.int32, t.shape, 0) == sub
            return jnp.sum(jnp.where(pick, t, 0.0), axis=0, keepdims=True)

        def row_set(ref, cols, val, sub=sub):
            t = ref[tile, cols]
            pick = lax.broadcasted_iota(jnp.int32, t.shape, 0) == sub
            ref[tile, cols] = jnp.where(pick, val, t)

        fc_row = row_get(fc_s, slice(0, LANES))
        for h in range(GLA_HEADS):
            sl = slice(h * GLA_DK, (h + 1) * GLA_DK)
            vl = slice(h * GLA_DV, (h + 1) * GLA_DV)
            a_c = _col_bcast(row_get(a_s, sl))
            k_c = _col_bcast(row_get(ka_s, sl))
            q_c = _col_bcast(row_get(qa_s, sl))
            v_row = row_get(va_s, vl)
            s_old = gla_ref[s, h]
            halves = []
            for p in range(GLA_DV // LANES):
                ls = slice(p * LANES, (p + 1) * LANES)
                s_new = s_old[:, ls] * a_c + k_c * v_row[:, ls]
                gla_o[s, h, :, ls] = s_new
                halves.append(jnp.sum(q_c * s_new, axis=0, keepdims=True))
            row_set(oa_s, vl, jnp.concatenate(halves, axis=1))

            f_c = jnp.sum(jnp.where(lane_row == h, fc_row, 0.0), axis=1, keepdims=True)
            vb_row = row_get(vb_s, vl)
            acc = [jnp.zeros((1, LANES), F32) for _ in range(ML_DH // LANES)]
            for r in range(ML_DH // LANES):
                rs = slice(r * LANES, (r + 1) * LANES)
                ks = slice(h * ML_DH + r * LANES, h * ML_DH + (r + 1) * LANES)
                kw_c = _col_bcast(row_get(kw_s, ks))
                qb_c = _col_bcast(row_get(qb_s, ks))
                for p in range(ML_DH // LANES):
                    ls = slice(p * LANES, (p + 1) * LANES)
                    c_new = f_c * c_ref[s, h, rs, ls] + kw_c * vb_row[:, ls]
                    c_o[s, h, rs, ls] = c_new
                    acc[p] = acc[p] + jnp.sum(qb_c * c_new, axis=0, keepdims=True)
            row_set(num_s, vl, jnp.concatenate(acc, axis=1))

    @pl.when(b == nb // sps - 1)
    def _post():
        m_new = m_o[...]
        den = den_s[...]
        for h in range(GLA_HEADS):
            vl = slice(h * GLA_DV, (h + 1) * GLA_DV)
            o_a = _headnorm(oa_s[:, vl], gng_ref[:, vl]) * _silu(big_ref[:, D_MODEL * 2 + h * GLA_DV:
                                                                       D_MODEL * 2 + (h + 1) * GLA_DV].astype(F32))
            dn = jnp.maximum(jnp.abs(_lane_col(den, h)), jnp.exp(-_lane_col(m_new, h)))
            hh = num_s[:, vl] / dn
            o_b = _headnorm(hh, mng_ref[:, vl]) * _sigmoid(
                big_ref[:, 6 * D_MODEL + h * ML_DH:6 * D_MODEL + (h + 1) * ML_DH].astype(F32))
            ga = big_ref[:, 7 * D_MODEL + h * ML_DH:7 * D_MODEL + (h + 1) * ML_DH].astype(F32)
            gb = big_ref[:, 8 * D_MODEL + h * ML_DH:8 * D_MODEL + (h + 1) * ML_DH].astype(F32)
            oa_s[:, vl] = _sigmoid(ga) * o_a + _sigmoid(gb) * o_b
        x1_ref[...] = x_ref[...] + jnp.dot(oa_s[...].astype(BF16), wout_ref[...], preferred_element_type=F32)


def _mixer_sample(x, big, sm, st_gla, st_c, st_n, st_m, st_conv, lw, li, prev_gla, prev_c):
    nb = x.shape[0]
    has_prev = prev_gla is not None
    full = lambda shape: pl.BlockSpec(shape, lambda b: tuple(0 for _ in shape))
    sps = SAMPLES_PER_STEP
    gla_spec = pl.BlockSpec((1, sps, GLA_HEADS, GLA_DK, GLA_DV), lambda b: (li, b, 0, 0, 0))
    c_spec = pl.BlockSpec((1, sps, ML_HEADS, ML_DH, ML_DH), lambda b: (li, b, 0, 0, 0))
    in_specs = [
        full((nb, D_MODEL)), full((nb, BIG_COLS)), full((nb, SMALL_COLS)),
        gla_spec, c_spec,
        full((nb, ML_W)), full((nb, LANES)), full((nb, 3 * 2 * ML_W)),
        full((LANES, GLA_QK)), full((1, GLA_QK)), full((1, D_MODEL)),
        full((CONV_W, 2 * ML_W)), full((1, 2 * ML_W)), full((1, 2 * LANES)), full((1, D_MODEL)),
        full((D_MODEL, D_MODEL)),
    ]
    operands = [x, big, sm, st_gla, st_c, st_n, st_m, st_conv,
                lw["w2p"], lw["b2"], lw["gng"], lw["cw"], lw["cb"], lw["gbr"], lw["mng"], lw["wout"]]
    aliases = {}
    if has_prev:
        in_specs += [pl.BlockSpec(memory_space=pl.ANY), pl.BlockSpec(memory_space=pl.ANY)]
        operands += [prev_gla, prev_c]
        aliases = {16: 1, 17: 2}
    out_specs = [
        full((nb, D_MODEL)),
        gla_spec, c_spec,
        full((nb, ML_W)), full((nb, LANES)), full((nb, 3 * 2 * ML_W)),
    ]
    out_shape = [
        jax.ShapeDtypeStruct((nb, D_MODEL), F32),
        jax.ShapeDtypeStruct(st_gla.shape, F32),
        jax.ShapeDtypeStruct(st_c.shape, F32),
        jax.ShapeDtypeStruct((nb, ML_W), F32),
        jax.ShapeDtypeStruct((nb, LANES), F32),
        jax.ShapeDtypeStruct((nb, 3 * 2 * ML_W), F32),
    ]
    vm = lambda cols: pltpu.VMEM((nb, cols), F32)
    scratch = [vm(GLA_QK), vm(GLA_QK), vm(GLA_QK), vm(D_MODEL), vm(ML_W), vm(ML_W), vm(ML_W),
               vm(LANES), vm(LANES), vm(D_MODEL), vm(ML_W)]
    return pl.pallas_call(
        functools.partial(_mixer_sample_kernel, nb=nb, has_prev=has_prev, sps=sps),
        grid=(nb // sps,),
        in_specs=in_specs, out_specs=out_specs, out_shape=out_shape,
        scratch_shapes=scratch,
        input_output_aliases=aliases,
        compiler_params=_params(("arbitrary",)),
        name="mixer_sample",
    )(*operands)


def _peer_select_kernel(x_ref, g_ref, wq_ref, sk_ref, h2_ref, rank_ref, p1_ref, cnt_ref, p0_ref,
                        sc_scr, *, tq, cw):
    nch = tq // cw
    hb = _rms(x_ref[...], g_ref[...]).astype(BF16)
    h2_ref[...] = hb
    q = jnp.dot(hb, wq_ref[...], preferred_element_type=F32).astype(BF16)
    for hc in range(2 * PEER_HEADS):
        for ch in range(nch):
            qs = q[ch * cw:(ch + 1) * cw, hc * LANES:(hc + 1) * LANES]
            sc_scr[hc, ch] = lax.dot_general(sk_ref[hc % 2], qs, NT_DIMS, preferred_element_type=F32)

    K = PEER_TOPK
    neg = -jnp.inf
    iota_k = lax.broadcasted_iota(jnp.int32, (K, cw), 0)
    iota_kf = iota_k.astype(F32)

    def unit(idx):
        h = idx // nch
        ch = idx % nch
        s0 = sc_scr[2 * h, ch]
        s1 = sc_scr[2 * h + 1, ch]

        sv0 = jnp.zeros((K, cw), F32)
        prev = jnp.full((1, cw), jnp.inf, F32)
        sv1 = jnp.zeros((K, cw), F32)
        rank1 = jnp.full((N_KEYS, cw), float(K), F32)
        work = s1
        for j in range(K):
            prev = jnp.max(jnp.where(s0 < prev, s0, neg), axis=0, keepdims=True)
            sv0 = jnp.where(iota_k == j, prev, sv0)
            yield
            mx = jnp.max(work, axis=0, keepdims=True)
            hit = work == mx
            rank1 = jnp.where(hit, float(j), rank1)
            work = jnp.where(hit, neg, work)
            sv1 = jnp.where(iota_k == j, mx, sv1)
            yield
        rank_ref[h, ch] = rank1.astype(BF16)
        p1_ref[h, ch] = jnp.exp(s1 - sv1[0:1, :]).astype(BF16)
        yield

        top = sv0[0:1, :] + sv1[0:1, :]
        front = sv0 + sv1[0:1, :]
        cnt = jnp.zeros((K, cw), F32)
        zsum = jnp.zeros((1, cw), F32)
        cntp = jnp.zeros((N_KEYS, cw), F32)
        for _ in range(K):
            fm = jnp.max(front, axis=0, keepdims=True)
            j1 = jnp.min(jnp.where(front == fm, iota_kf, float(K)), axis=0, keepdims=True)
            sel = iota_kf == j1
            zsum = zsum + jnp.exp(fm - top)
            nxt = jnp.sum(jnp.where(sel, cnt, 0.0), axis=0, keepdims=True) + 1.0
            cnt = jnp.where(sel, cnt + 1.0, cnt)
            sv1n = jnp.sum(jnp.where(iota_kf == nxt, sv1, 0.0), axis=0, keepdims=True)
            sv1n = jnp.where(nxt >= float(K), neg, sv1n)
            sv0s = jnp.sum(jnp.where(sel, sv0, 0.0), axis=0, keepdims=True)
            front = jnp.where(sel, sv0s + sv1n, front)
            cntp = jnp.where(s0 == sv0s, cntp + 1.0, cntp)
            yield
        cnt_ref[h, ch] = cntp
        p0_ref[h, ch] = jnp.exp(s0 - sv0[0:1, :]) * (GELU_IN_SCALE / zsum)

    def body(it, carry):
        units = [unit(2 * it), unit(2 * it + 1)]
        while units:
            for u in list(units):
                if next(u, "done") == "done":
                    units.remove(u)
        return carry

    lax.fori_loop(0, PEER_HEADS * nch // 2, body, 0)


def _peer_select(x1, g2, wq, sk, tq, cw):
    t = x1.shape[0]
    nch = tq // cw
    meta = lambda dt: jax.ShapeDtypeStruct((PEER_HEADS, t // cw, N_KEYS, cw), dt)
    mspec = pl.BlockSpec((PEER_HEADS, nch, N_KEYS, cw), lambda i: (0, i, 0, 0))
    return pl.pallas_call(
        functools.partial(_peer_select_kernel, tq=tq, cw=cw),
        grid=(t // tq,),
        in_specs=[
            pl.BlockSpec((tq, D_MODEL), lambda i: (i, 0)),
            pl.BlockSpec((1, D_MODEL), lambda i: (0, 0)),
            pl.BlockSpec((D_MODEL, PEER_HEADS * PEER_DQ), lambda i: (0, 0)),
            pl.BlockSpec((2, N_KEYS, PEER_DQ // 2), lambda i: (0, 0, 0)),
        ],
        out_specs=[pl.BlockSpec((tq, D_MODEL), lambda i: (i, 0)), mspec, mspec, mspec, mspec],
        out_shape=[jax.ShapeDtypeStruct((t, D_MODEL), BF16), meta(BF16), meta(BF16), meta(F32), meta(F32)],
        scratch_shapes=[pltpu.VMEM((2 * PEER_HEADS, nch, N_KEYS, cw), F32)],
        compiler_params=_params(("arbitrary",)),
        name="peer_select",
    )(x1, g2, wq, sk)


def _peer_dense_kernel(x_ref, h2_ref, rank_ref, p1_ref, cnt_ref, p0_ref, u_ref, vt_ref, fg_ref,
                       out_ref, acc_scr, g_scr, ht_scr, *, tb, eb, ec, cw, nblk_e, final_norm):
    ngrp = eb // ec
    j = pl.program_id(1)

    @pl.when(j == 0)
    def _():
        acc_scr[...] = jnp.zeros_like(acc_scr)

    for grp in range(ngrp):
        es = slice(grp * ec, (grp + 1) * ec)
        u_grp = pltpu.bitcast(u_ref[grp * (ec // 2):(grp + 1) * (ec // 2), :], BF16)
        ht_scr[grp] = lax.dot_general(u_grp, h2_ref[...], NT_DIMS, preferred_element_type=F32)
    for grp in range(ngrp):
        es = slice(grp * ec, (grp + 1) * ec)
        for k in range(ec // N_KEYS):
            r = grp * (ec // N_KEYS) + k
            ks = slice(grp * ec + k * N_KEYS, grp * ec + (k + 1) * N_KEYS)
            for ch in range(tb // cw):
                cs = slice(ch * cw, (ch + 1) * cw)
                w = jnp.zeros((N_KEYS, cw), BF16)
                for h in range(PEER_HEADS):
                    cnt_row = jnp.broadcast_to(cnt_ref[h, ch, r:r + 1, :], (N_KEYS, cw)).astype(BF16)
                    p0_row = jnp.broadcast_to(p0_ref[h, ch, r:r + 1, :], (N_KEYS, cw)).astype(BF16)
                    w = w + jnp.where(rank_ref[h, ch] < cnt_row, p1_ref[h, ch] * p0_row, jnp.zeros_like(w))
                z = ht_scr[grp, k * N_KEYS:(k + 1) * N_KEYS, cs]
                g_scr[ks, cs] = w * (z * (1.0 + lax.erf(z))).astype(BF16)
        acc_scr[...] += jnp.dot(pltpu.bitcast(vt_ref[:, es], BF16), g_scr[es, :],
                                preferred_element_type=F32)

    @pl.when(j == nblk_e - 1)
    def _():
        y = x_ref[...] + acc_scr[...].T
        if final_norm:
            y = _rms(y, fg_ref[...])
        out_ref[...] = y


def _peer_dense(x1, h2, rank, p1, cnt, p0, u_bf, vt_bf, fg, tb, eb, cw, final_norm):
    t = x1.shape[0]
    nch = tb // cw
    nblk_e = N_EXPERTS // eb
    ec = min(eb, 512)
    mspec = pl.BlockSpec((PEER_HEADS, nch, N_KEYS, cw), lambda i, j: (0, i, 0, 0))
    rspec = pl.BlockSpec((PEER_HEADS, nch, eb // N_KEYS, cw), lambda i, j: (0, i, j, 0))
    return pl.pallas_call(
        functools.partial(_peer_dense_kernel, tb=tb, eb=eb, ec=ec, cw=cw, nblk_e=nblk_e,
                          final_norm=final_norm),
        grid=(t // tb, nblk_e),
        in_specs=[
            pl.BlockSpec((tb, D_MODEL), lambda i, j: (i, 0)),
            pl.BlockSpec((tb, D_MODEL), lambda i, j: (i, 0)),
            mspec, mspec, rspec, rspec,
            pl.BlockSpec((eb // 2, D_MODEL), lambda i, j: (j, 0)),
            pl.BlockSpec((D_MODEL // 2, eb), lambda i, j: (0, j)),
            pl.BlockSpec((1, D_MODEL), lambda i, j: (0, 0)),
        ],
        out_specs=pl.BlockSpec((tb, D_MODEL), lambda i, j: (i, 0)),
        out_shape=jax.ShapeDtypeStruct((t, D_MODEL), F32),
        scratch_shapes=[pltpu.VMEM((D_MODEL, tb), F32), pltpu.VMEM((eb, tb), BF16),
                        pltpu.VMEM((eb // ec, ec, tb), F32)],
        compiler_params=_params(("arbitrary", "arbitrary")),
        name="peer_dense",
    )(x1, h2, rank, p1, cnt, p0, u_bf, vt_bf, fg)


def _layer_weights(li, w_in, gla_w2, gla_b2, gla_norm_g, conv_w, conv_b, ml_i_b, ml_f_b, ml_norm_g,
                   w_out, peer_wq, peer_subkeys, peer_u, peer_v):
    w = w_in[li]
    o = 0
    segs = {}
    for name, width in (("aq", GLA_QK), ("ak", GLA_QK), ("av", D_MODEL), ("ag", D_MODEL),
                        ("alr", GLA_GATE_RANK), ("bq", ML_W), ("bk", ML_W), ("bv", ML_W),
                        ("bi", ML_HEADS), ("bf", ML_HEADS), ("bo", ML_W),
                        ("ga", D_MODEL), ("gb", D_MODEL)):
        segs[name] = w[:, o:o + width]
        o += width
    seg_w = WSEG_BLOCKS * D_MODEL
    starts = (0, seg_w + GLA_GATE_RANK, 2 * seg_w + GLA_GATE_RANK + 2 * ML_HEADS)
    w_segs = tuple(w[:, s:s + seg_w].astype(BF16) for s in starts)
    pad = lambda a: jnp.pad(a, ((0, 0), (0, LANES - a.shape[1])))
    w_small = jnp.concatenate([pad(segs["alr"]), pad(segs["bi"]), pad(segs["bf"])], axis=1).astype(BF16)
    w_gt = jnp.concatenate([segs["bi"], segs["bf"]], axis=1).T.astype(BF16)
    gate_b = jnp.concatenate([ml_i_b[li], ml_f_b[li]])
    u_packed, vt_packed = _table_prep(peer_u, peer_v, li)
    return dict(
        w_segs=w_segs, w_small=w_small, w_gt=w_gt,
        w2p=jnp.pad(gla_w2[li], ((0, LANES - GLA_GATE_RANK), (0, 0))),
        b2=gla_b2[li][None, :], gng=gla_norm_g[li][None, :],
        cw=conv_w[li], cb=conv_b[li][None, :],
        gbr=jnp.concatenate([jnp.pad(ml_i_b[li], (0, LANES - ML_HEADS)),
                             jnp.pad(ml_f_b[li], (0, LANES - ML_HEADS))])[None, :],
        gbc=gate_b[:, None], mng=ml_norm_g[li][None, :],
        wout=w_out[li].astype(BF16),
        wq=peer_wq[li].astype(BF16), sk=peer_subkeys[li].astype(BF16),
        u=u_packed, vt=vt_packed,
    )


def _table_prep_kernel(u_ref, v_ref, up_ref, vtp_ref):
    up_ref[...] = pltpu.bitcast((u_ref[0] * GELU_IN_SCALE).astype(BF16), jnp.uint32)
    vtp_ref[...] = pltpu.bitcast(v_ref[0].T.astype(BF16), jnp.uint32)


def _table_prep(u_tabs, v_tabs, li, eb=1024):
    _, ne, d = u_tabs.shape
    return pl.pallas_call(
        _table_prep_kernel,
        grid=(ne // eb,),
        in_specs=[pl.BlockSpec((1, eb, d), lambda i: (li, i, 0)), pl.BlockSpec((1, eb, d), lambda i: (li, i, 0))],
        out_specs=[pl.BlockSpec((eb // 2, d), lambda i: (i, 0)), pl.BlockSpec((d // 2, eb), lambda i: (0, i))],
        out_shape=[jax.ShapeDtypeStruct((ne // 2, d), jnp.uint32),
                   jax.ShapeDtypeStruct((d // 2, ne), jnp.uint32)],
        compiler_params=_params(("arbitrary",)),
        name="table_prep",
    )(u_tabs, v_tabs)


def _pick(n, cands):
    for c in cands:
        if n % c == 0:
            return c
    return n


def _peer(x1, g2, lw, fg, final_norm):
    t = x1.shape[0]
    tq = _pick(t, (512, 256, 128))
    tb = _pick(t, (512, 256, 128))
    cw = _pick(tb, (2 * LANES, LANES))
    h2, rank, p1, cnt, p0 = _peer_select(x1, g2, lw["wq"], lw["sk"], tq, cw)
    return _peer_dense(x1, h2, rank, p1, cnt, p0, lw["u"], lw["vt"], fg, tb, 2048, cw, final_norm)


def kernel(x_prompt, x_sample, state_gla, state_mlstm_c, state_mlstm_n, state_mlstm_m, state_conv,
           norm1_g, w_in, gla_w2, gla_b2, gla_norm_g, conv_w, conv_b, ml_i_b, ml_f_b, ml_norm_g,
           w_out, norm2_g, peer_wq, peer_subkeys, peer_u, peer_v, final_g):
    depth = w_in.shape[0]
    bsz, seq, _ = x_prompt.shape
    nb = x_sample.shape[0]
    assert seq % CHUNK == 0 and x_sample.shape[1] == 1 and nb % LANES == 0
    tp = bsz * seq
    tm = _pick(tp, (2048, 1024, 512, 256, 128))
    tc = _pick(seq, (256, 128, 64))
    npar = _pick(bsz, (2, 1))
    fg = final_g[None, :]

    xp = x_prompt.reshape(tp, D_MODEL)
    xs = x_sample.reshape(nb, D_MODEL)
    p_out = [[] for _ in range(5)]
    s_out = [[] for _ in range(5)]
    s_gla = s_c = None
    for li in range(depth):
        lw = _layer_weights(li, w_in, gla_w2, gla_b2, gla_norm_g, conv_w, conv_b, ml_i_b, ml_f_b,
                            ml_norm_g, w_out, peer_wq, peer_subkeys, peer_u, peer_v)
        g1 = norm1_g[li][None, :]
        g2 = norm2_g[li][None, :]
        last = li == depth - 1

        big, sm, gt = _inproj(xp, g1, lw["w_segs"], lw["w_small"], lw["w_gt"], tm)
        gtc = gt.reshape(8, bsz, seq // CHUNK, CHUNK).transpose(1, 2, 0, 3)
        x1, gla, c, n, m, conv = _mixer_prompt(
            xp.reshape(bsz, seq, D_MODEL), big.reshape(bsz, seq, BIG_COLS),
            sm.reshape(bsz, seq, SMALL_COLS), gtc, lw, bsz, seq, tc, npar)
        xp = _peer(x1.reshape(tp, D_MODEL), g2, lw, fg, last)
        for lst, val in zip(p_out, (gla, c, n, m[:, 0, :ML_HEADS], conv)):
            lst.append(val)

        big, sm, _ = _inproj(xs, g1, lw["w_segs"], lw["w_small"], lw["w_gt"], nb)
        x1, s_gla, s_c, n, m, conv = _mixer_sample(
            xs, big, sm, state_gla, state_mlstm_c,
            state_mlstm_n[li].reshape(nb, ML_W),
            jnp.pad(state_mlstm_m[li], ((0, 0), (0, LANES - ML_HEADS))),
            state_conv[li].reshape(nb, (CONV_W - 1) * 2 * ML_W), lw, li, s_gla, s_c)
        xs = _peer(x1, g2, lw, fg, last)
        for lst, val in zip(s_out[2:], (n.reshape(nb, ML_HEADS, ML_DH), m[:, :ML_HEADS],
                                        conv.reshape(nb, CONV_W - 1, 2 * ML_W))):
            lst.append(val)

    y_prompt = xp.reshape(bsz, seq, D_MODEL)
    y_sample = xs.reshape(nb, 1, D_MODEL)
    return (y_prompt, y_sample, *[jnp.stack(v) for v in p_out],
            s_gla, s_c, *[jnp.stack(v) for v in s_out[2:]])
```

```python
import functools

import jax
import jax.numpy as jnp
from jax import lax
from jax.experimental import pallas as pl
from jax.experimental.pallas import tpu as pltpu

F32 = jnp.float32
BF16 = jnp.bfloat16
HIGHEST = lax.Precision.HIGHEST

D_MODEL = 1024
GLA_HEADS = 4
GLA_DK = 128
GLA_DV = 256
GLA_GATE_RANK = 16
GLA_TAU = 16.0
GLA_QK = GLA_HEADS * GLA_DK
ML_HEADS = 4
ML_DH = 256
ML_W = ML_HEADS * ML_DH
CONV_W = 4
CHUNK = 64
PEER_HEADS = 8
PEER_DQ = 256
N_KEYS = 128
N_EXPERTS = N_KEYS * N_KEYS
PEER_TOPK = 16
EPS = 1e-6
GELU_IN_SCALE = 0.7071067811865476
RANK_MARK = 2.0 ** 100
RANK_MARK_STEP = 2.0 ** 96

LANES = 128
BIG_COLS = 9 * D_MODEL
WSEG_BLOCKS = 3
SAMPLES_PER_STEP = 2
HEADS_PER_ROUND = 4
SMALL_COLS = 3 * LANES
VMEM_LIMIT_BYTES = 56 * 1024 * 1024

NT_DIMS = (((1,), (1,)), ((), ()))
TN_DIMS = (((0,), (0,)), ((), ()))


def _params(sem, flags=None):
    return pltpu.CompilerParams(dimension_semantics=sem, vmem_limit_bytes=VMEM_LIMIT_BYTES, flags=flags)


def _log_sigmoid(x):
    return jnp.minimum(x, 0.0) - jnp.log1p(jnp.exp(-jnp.abs(x)))


def _sigmoid(x):
    return 1.0 / (1.0 + jnp.exp(-x))


def _silu(x):
    return x * _sigmoid(x)


def _rms(x, g):
    return x * lax.rsqrt(jnp.mean(x * x, axis=-1, keepdims=True) + EPS) * g


def _lane_col(a, j):
    lane = lax.broadcasted_iota(jnp.int32, a.shape, 1)
    return jnp.sum(jnp.where(lane == j, a, 0.0), axis=1, keepdims=True)


def _inproj_kernel(x_ref, g_ref, wa_ref, wb_ref, wc_ref, wsm_ref, wgt_ref, big_ref, sm_ref, gt_ref, h_scr):
    j = pl.program_id(1)

    @pl.when(j == 0)
    def _():
        hb = _rms(x_ref[...], g_ref[...]).astype(BF16)
        h_scr[...] = hb
        sm_ref[...] = jnp.dot(hb, wsm_ref[...], preferred_element_type=F32)
        gt_ref[...] = lax.dot_general(wgt_ref[...], hb, NT_DIMS, preferred_element_type=F32)

    for s, w_ref in enumerate((wa_ref, wb_ref, wc_ref)):
        @pl.when((j >= s * WSEG_BLOCKS) & (j < (s + 1) * WSEG_BLOCKS))
        def _(w_ref=w_ref):
            big_ref[...] = jnp.dot(h_scr[...], w_ref[...], preferred_element_type=F32).astype(BF16)


def _inproj(x, g, w_segs, w_small, w_gt, tm):
    t = x.shape[0]
    tn = D_MODEL

    def wspec(s):
        return pl.BlockSpec((D_MODEL, tn), lambda i, j: (0, jnp.clip(j - s * WSEG_BLOCKS, 0, WSEG_BLOCKS - 1)))

    return pl.pallas_call(
        _inproj_kernel,
        grid=(t // tm, BIG_COLS // tn),
        in_specs=[
            pl.BlockSpec((tm, D_MODEL), lambda i, j: (i, 0)),
            pl.BlockSpec((1, D_MODEL), lambda i, j: (0, 0)),
            wspec(0), wspec(1), wspec(2),
            pl.BlockSpec((D_MODEL, SMALL_COLS), lambda i, j: (0, 0)),
            pl.BlockSpec((8, D_MODEL), lambda i, j: (0, 0)),
        ],
        out_specs=[
            pl.BlockSpec((tm, tn), lambda i, j: (i, j)),
            pl.BlockSpec((tm, SMALL_COLS), lambda i, j: (i, 0)),
            pl.BlockSpec((8, tm), lambda i, j: (0, i)),
        ],
        out_shape=[
            jax.ShapeDtypeStruct((t, BIG_COLS), BF16),
            jax.ShapeDtypeStruct((t, SMALL_COLS), F32),
            jax.ShapeDtypeStruct((8, t), F32),
        ],
        scratch_shapes=[pltpu.VMEM((tm, D_MODEL), BF16)],
        compiler_params=_params(("arbitrary", "arbitrary")),
        name="inproj",
    )(x, g, *w_segs, w_small, w_gt)


def _headnorm(o, g_row):
    return o * lax.rsqrt(jnp.mean(o * o, axis=-1, keepdims=True) + EPS) * g_row


def _mixer_prompt_kernel(x_ref, aq_ref, ak_ref, av_ref, ag_ref, bq_ref, bk_ref, bv_ref, bo_ref,
                         ga_ref, gb_ref, sm_ref, gt_ref, w2_ref, b2_ref, gng_ref, cw_ref, cb_ref,
                         gbr_ref, gbc_ref, mng_ref, wout_ref,
                         x1_ref, gla_ref, c_ref, n_ref, m_ref, conv_ref,
                         st_scr, xp_scr, qk_scr, mix_scr, *, tc, nblk, npar):
    i = pl.program_id(1)
    L = CHUNK

    @pl.when(i == 0)
    def _init():
        st_scr[...] = jnp.zeros_like(st_scr)
        c_ref[...] = jnp.zeros_like(c_ref)
        n_ref[...] = jnp.zeros_like(n_ref)
        m_ref[...] = jnp.zeros_like(m_ref)
        xp_scr[:, 0:8, :] = jnp.zeros((npar, 8, 2 * ML_W), F32)

    for p in range(npar):
        xp_scr[p, 8:8 + tc, 0:ML_W] = bq_ref[p].astype(F32)
        xp_scr[p, 8:8 + tc, ML_W:2 * ML_W] = bk_ref[p].astype(F32)
        y = cb_ref[...]
        for j in range(CONV_W):
            y = y + cw_ref[j:j + 1, :] * xp_scr[p, 5 + j:5 + j + tc, :]
        qk = _silu(y)
        qk_scr[p, :, 0:ML_W] = qk[:, 0:ML_W].astype(BF16)
        qk_scr[p, :, ML_W:2 * ML_W] = (qk[:, ML_W:2 * ML_W] * (ML_DH ** -0.5)).astype(BF16)
        xp_scr[p, 5:8, :] = xp_scr[p, tc + 5:tc + 8, :]

    @pl.when(i == nblk - 1)
    def _():
        conv_ref[...] = xp_scr[:, 5:8, :]

    row = lax.broadcasted_iota(jnp.int32, (L, L), 0)
    col = lax.broadcasted_iota(jnp.int32, (L, L), 1)
    causal = col <= row
    tril = causal.astype(F32)
    triu = (row <= col).astype(F32)
    lane_row = lax.broadcasted_iota(jnp.int32, (1, LANES), 1)

    def gates(p, c, rows):
        sm = sm_ref[p, rows, :]
        z = jnp.dot(sm[:, 0:LANES], w2_ref[...], precision=HIGHEST, preferred_element_type=F32) + b2_ref[...]
        log_a = _log_sigmoid(z) * (1.0 / GLA_TAU)
        bc = jnp.dot(tril, log_a, precision=HIGHEST, preferred_element_type=F32)
        li_c_all = sm[:, LANES:2 * LANES] + gbr_ref[:, 0:LANES]
        lf_c_all = _log_sigmoid(sm[:, 2 * LANES:3 * LANES] + gbr_ref[:, LANES:2 * LANES])
        bcum_c_all = jnp.dot(tril, lf_c_all, precision=HIGHEST, preferred_element_type=F32)
        gtb = gt_ref[p, c] + gbc_ref[...]
        bcum_r_all = jnp.dot(_log_sigmoid(gtb), triu, precision=HIGHEST, preferred_element_type=F32)
        return bc, li_c_all, bcum_c_all, gtb, bcum_r_all

    def head(p, h, rows, shared):
        bc, li_c_all, bcum_c_all, gtb, bcum_r_all = shared
        sl = slice(h * GLA_DK, (h + 1) * GLA_DK)
        vl = slice(h * GLA_DV, (h + 1) * GLA_DV)
        b = bc[:, sl]
        mid = b[L // 2 - 1:L // 2, :]
        bl = b[L - 1:L, :]
        q = aq_ref[p, rows, sl].astype(F32) * (GLA_DK ** -0.5)
        k = ak_ref[p, rows, sl].astype(F32)
        v = av_ref[p, rows, vl]
        q_in = (q * jnp.exp(b)).astype(BF16)
        q_at = (q * jnp.exp(b - mid)).astype(BF16)
        yield
        k_at = (k * jnp.exp(mid - b)).astype(BF16)
        k_out = (k * jnp.exp(bl - b)).astype(BF16)
        yield
        att = lax.dot_general(q_at, k_at, NT_DIMS, preferred_element_type=F32)
        att = jnp.where(causal, att, 0.0)
        st = st_scr[p, h]
        yield
        o = lax.dot_general(q_in, st.astype(BF16), NT_DIMS, preferred_element_type=F32)
        yield
        o = o + jnp.dot(att.astype(BF16), v, preferred_element_type=F32)
        yield
        st_scr[p, h] = st * jnp.exp(bl) + lax.dot_general(v, k_out, TN_DIMS, preferred_element_type=F32)
        yield
        o_a = _headnorm(o, gng_ref[:, vl]) * _silu(ag_ref[p, rows, vl].astype(F32))
        yield

        li_c = _lane_col(li_c_all, h)
        b_c = _lane_col(bcum_c_all, h)
        li_r = gtb[h:h + 1, :]
        b_r = bcum_r_all[ML_HEADS + h:ML_HEADS + h + 1, :]
        m_prev = jnp.sum(jnp.where(lane_row == h, m_ref[p], 0.0), axis=1, keepdims=True)
        a_int = b_c + m_prev
        dm = jnp.where(causal, b_c - b_r + li_r, -jnp.inf)
        m_t = jnp.maximum(a_int, jnp.max(dm, axis=1, keepdims=True))
        yield
        w_int = jnp.exp(a_int - m_t)
        dexp = jnp.exp(dm - m_t)
        qb = qk_scr[p, rows, vl]
        kb = qk_scr[p, rows, ML_W + h * ML_DH:ML_W + (h + 1) * ML_DH]
        vb = bv_ref[p, rows, vl]
        s = lax.dot_general(qb, kb, NT_DIMS, preferred_element_type=F32) * dexp
        yield
        cst = c_ref[p, h]
        num = w_int * jnp.dot(qb, cst.astype(BF16), preferred_element_type=F32)
        yield
        num = num + jnp.dot(s.astype(BF16), vb, preferred_element_type=F32)
        nrow = n_ref[p, h:h + 1, :]
        den = (w_int * jnp.sum(qb.astype(F32) * nrow, axis=1, keepdims=True)
               + jnp.sum(s, axis=1, keepdims=True))
        yield
        hh = num / jnp.maximum(jnp.abs(den), jnp.exp(-m_t))
        b_last = b_c[L - 1:L, :]
        g_c = b_last - b_c + li_c
        m_new = jnp.maximum(b_last + m_prev, jnp.max(g_c, axis=0, keepdims=True))
        f_c = jnp.exp(b_last + m_prev - m_new)
        w_s = jnp.exp(g_c - m_new)
        kw = kb.astype(F32) * w_s
        yield
        c_ref[p, h] = f_c * cst + lax.dot_general(kw.astype(BF16), vb, TN_DIMS, preferred_element_type=F32)
        n_ref[p, h:h + 1, :] = f_c * nrow + jnp.sum(kw, axis=0, keepdims=True)
        m_ref[p] = jnp.where(lane_row == h, m_new, m_ref[p])
        yield
        o_b = _headnorm(hh, mng_ref[:, vl]) * _sigmoid(bo_ref[p, rows, vl].astype(F32))
        yield
        mix = (_sigmoid(ga_ref[p, rows, vl].astype(F32)) * o_a
               + _sigmoid(gb_ref[p, rows, vl].astype(F32)) * o_b)
        mix_scr[p, rows, vl] = mix.astype(BF16)

    def chunk(c, carry):
        rows = pl.ds(pl.multiple_of(c * L, L), L)
        shared = [gates(p, c, rows) for p in range(npar)]
        for h0 in range(0, GLA_HEADS, HEADS_PER_ROUND):
            units = [head(p, h, rows, shared[p]) for h in range(h0, h0 + HEADS_PER_ROUND) for p in range(npar)]
            while units:
                for u in list(units):
                    if next(u, "done") == "done":
                        units.remove(u)
        return carry

    lax.fori_loop(0, tc // L, chunk, 0)

    for p in range(npar):
        x1_ref[p] = x_ref[p] + jnp.dot(mix_scr[p], wout_ref[...], preferred_element_type=F32)

    @pl.when(i == nblk - 1)
    def _fin():
        for p in range(npar):
            for h in range(GLA_HEADS):
                gla_ref[p, h] = st_scr[p, h].T


def _mixer_prompt(x, big, sm, gtc, lw, bsz, seq, tc, npar):
    nblk = seq // tc

    def seg(width, idx):
        return pl.BlockSpec((npar, tc, width), lambda b, i: (b, i, idx))

    const = lambda shape: pl.BlockSpec(shape, lambda b, i: tuple(0 for _ in shape))
    in_specs = [
        seg(D_MODEL, 0),
        seg(GLA_QK, 0), seg(GLA_QK, 1),
        seg(D_MODEL, 1), seg(D_MODEL, 2),
        seg(D_MODEL, 3), seg(D_MODEL, 4), seg(D_MODEL, 5),
        seg(D_MODEL, 6), seg(D_MODEL, 7), seg(D_MODEL, 8),
        seg(SMALL_COLS, 0),
        pl.BlockSpec((npar, tc // CHUNK, 8, CHUNK), lambda b, i: (b, i, 0, 0)),
        const((LANES, GLA_QK)), const((1, GLA_QK)), const((1, D_MODEL)),
        const((CONV_W, 2 * ML_W)), const((1, 2 * ML_W)),
        const((1, 2 * LANES)), const((8, 1)), const((1, D_MODEL)),
        const((D_MODEL, D_MODEL)),
    ]
    out_specs = [
        seg(D_MODEL, 0),
        pl.BlockSpec((npar, GLA_HEADS, GLA_DK, GLA_DV), lambda b, i: (b, 0, 0, 0)),
        pl.BlockSpec((npar, ML_HEADS, ML_DH, ML_DH), lambda b, i: (b, 0, 0, 0)),
        pl.BlockSpec((npar, ML_HEADS, ML_DH), lambda b, i: (b, 0, 0)),
        pl.BlockSpec((npar, 1, LANES), lambda b, i: (b, 0, 0)),
        pl.BlockSpec((npar, CONV_W - 1, 2 * ML_W), lambda b, i: (b, 0, 0)),
    ]
    out_shape = [
        jax.ShapeDtypeStruct((bsz, seq, D_MODEL), F32),
        jax.ShapeDtypeStruct((bsz, GLA_HEADS, GLA_DK, GLA_DV), F32),
        jax.ShapeDtypeStruct((bsz, ML_HEADS, ML_DH, ML_DH), F32),
        jax.ShapeDtypeStruct((bsz, ML_HEADS, ML_DH), F32),
        jax.ShapeDtypeStruct((bsz, 1, LANES), F32),
        jax.ShapeDtypeStruct((bsz, CONV_W - 1, 2 * ML_W), F32),
    ]
    scratch = [
        pltpu.VMEM((npar, GLA_HEADS, GLA_DV, GLA_DK), F32),
        pltpu.VMEM((npar, tc + 8, 2 * ML_W), F32),
        pltpu.VMEM((npar, tc, 2 * ML_W), BF16),
        pltpu.VMEM((npar, tc, D_MODEL), BF16),
    ]
    return pl.pallas_call(
        functools.partial(_mixer_prompt_kernel, tc=tc, nblk=nblk, npar=npar),
        grid=(bsz // npar, nblk),
        in_specs=in_specs, out_specs=out_specs, out_shape=out_shape,
        scratch_shapes=scratch,
        compiler_params=_params(("arbitrary", "arbitrary")),
        name="mixer_prompt",
    )(x, big, big, big, big, big, big, big, big, big, big, sm, gtc,
      lw["w2p"], lw["b2"], lw["gng"], lw["cw"], lw["cb"], lw["gbr"], lw["gbc"], lw["mng"], lw["wout"])


def _col_bcast(row):
    return jnp.broadcast_to(row, (LANES, LANES)).T


def _mixer_sample_kernel(*refs, nb, has_prev, sps):
    n_in = 18 if has_prev else 16
    (x_ref, big_ref, sm_ref, gla_in, c_in, n_ref, m_ref, conv_ref,
     w2_ref, b2_ref, gng_ref, cw_ref, cb_ref, gbr_ref, mng_ref, wout_ref) = refs[:16]
    x1_ref, gla_out, c_out, n_o, m_o, conv_o = refs[n_in:n_in + 6]
    a_s, qa_s, ka_s, va_s, kw_s, qb_s, vb_s, fc_s, den_s, oa_s, num_s = refs[n_in + 6:]
    gla_ref, c_ref, gla_o, c_o = gla_in.at[0], c_in.at[0], gla_out.at[0], c_out.at[0]
    b = pl.program_id(0)
    seg = lambda idx: slice(idx * D_MODEL, (idx + 1) * D_MODEL)
    lane = lax.broadcasted_iota(jnp.int32, (nb, LANES), 1)

    @pl.when(b == 0)
    def _prep():
        sm = sm_ref[...]
        z = jnp.dot(sm[:, 0:LANES], w2_ref[...], precision=HIGHEST, preferred_element_type=F32) + b2_ref[...]
        a_s[...] = jnp.exp(_log_sigmoid(z) * (1.0 / GLA_TAU))
        qa_s[...] = big_ref[:, 0:GLA_QK].astype(F32) * (GLA_DK ** -0.5)
        ka_s[...] = big_ref[:, GLA_QK:2 * GLA_QK].astype(F32)
        va_s[...] = big_ref[:, seg(1)].astype(F32)
        vb_s[...] = big_ref[:, seg(5)].astype(F32)
        xq = big_ref[:, seg(3)].astype(F32)
        xk = big_ref[:, seg(4)].astype(F32)
        w = 2 * ML_W
        y = cb_ref[...]
        for j in range(CONV_W - 1):
            y = y + cw_ref[j:j + 1, :] * conv_ref[:, j * w:(j + 1) * w]
        y = y + cw_ref[CONV_W - 1:CONV_W, :] * jnp.concatenate([xq, xk], axis=1)
        qk = _silu(y)
        qb = qk[:, 0:ML_W]
        kb = qk[:, ML_W:w] * (ML_DH ** -0.5)
        qb_s[...] = qb
        conv_o[:, 0:2 * w] = conv_ref[:, w:3 * w]
        conv_o[:, 2 * w:2 * w + ML_W] = xq
        conv_o[:, 2 * w + ML_W:3 * w] = xk

        li = sm[:, LANES:2 * LANES] + gbr_ref[:, 0:LANES]
        lf = _log_sigmoid(sm[:, 2 * LANES:3 * LANES] + gbr_ref[:, LANES:2 * LANES])
        m_old = m_ref[...]
        m_new = jnp.maximum(lf + m_old, li)
        f_c = jnp.exp(lf + m_old - m_new)
        w_s = jnp.exp(li - m_new)
        m_o[...] = m_new
        fc_s[...] = f_c
        den = jnp.zeros((nb, LANES), F32)
        for h in range(ML_HEADS):
            vl = slice(h * ML_DH, (h + 1) * ML_DH)
            kw = kb[:, vl] * _lane_col(w_s, h)
            kw_s[:, vl] = kw
            n_new = _lane_col(f_c, h) * n_ref[:, vl] + kw
            n_o[:, vl] = n_new
            den = jnp.where(lane == h, jnp.sum(qb[:, vl] * n_new, axis=1, keepdims=True), den)
        den_s[...] = den
        oa_s[...] = jnp.zeros_like(oa_s)
        num_s[...] = jnp.zeros_like(num_s)

    tile = pl.ds(pl.multiple_of(lax.shift_right_logical(b * sps, 3) * 8, 8), 8)
    lane_row = lax.broadcasted_iota(jnp.int32, (1, LANES), 1)

    for s in range(sps):
        sub = jnp.bitwise_and(b * sps + s, 7)

        def row_get(ref, cols, sub=sub):
            t = ref[tile, cols]
            pick = lax.broadcasted_iota(jnp.int32, t.shape, 0) == sub
            return jnp.sum(jnp.where(pick, t, 0.0), axis=0, keepdims=True)

        def row_set(ref, cols, val, sub=sub):
            t = ref[tile, cols]
            pick = lax.broadcasted_iota(jnp.int32, t.shape, 0) == sub
            ref[tile, cols] = jnp.where(pick, val, t)

        fc_row = row_get(fc_s, slice(0, LANES))
        for h in range(GLA_HEADS):
            sl = slice(h * GLA_DK, (h + 1) * GLA_DK)
            vl = slice(h * GLA_DV, (h + 1) * GLA_DV)
            a_c = _col_bcast(row_get(a_s, sl))
            k_c = _col_bcast(row_get(ka_s, sl))
            q_c = _col_bcast(row_get(qa_s, sl))
            v_row = row_get(va_s, vl)
            s_old = gla_ref[s, h]
            halves = []
            for p in range(GLA_DV // LANES):
                ls = slice(p * LANES, (p + 1) * LANES)
                s_new = s_old[:, ls] * a_c + k_c * v_row[:, ls]
                gla_o[s, h, :, ls] = s_new
                halves.append(jnp.sum(q_c * s_new, axis=0, keepdims=True))
            row_set(oa_s, vl, jnp.concatenate(halves, axis=1))

            f_c = jnp.sum(jnp.where(lane_row == h, fc_row, 0.0), axis=1, keepdims=True)
            vb_row = row_get(vb_s, vl)
            acc = [jnp.zeros((1, LANES), F32) for _ in range(ML_DH // LANES)]
            for r in range(ML_DH // LANES):
                rs = slice(r * LANES, (r + 1) * LANES)
                ks = slice(h * ML_DH + r * LANES, h * ML_DH + (r + 1) * LANES)
                kw_c = _col_bcast(row_get(kw_s, ks))
                qb_c = _col_bcast(row_get(qb_s, ks))
                for p in range(ML_DH // LANES):
                    ls = slice(p * LANES, (p + 1) * LANES)
                    c_new = f_c * c_ref[s, h, rs, ls] + kw_c * vb_row[:, ls]
                    c_o[s, h, rs, ls] = c_new
                    acc[p] = acc[p] + jnp.sum(qb_c * c_new, axis=0, keepdims=True)
            row_set(num_s, vl, jnp.concatenate(acc, axis=1))

    @pl.when(b == nb // sps - 1)
    def _post():
        m_new = m_o[...]
        den = den_s[...]
        for h in range(GLA_HEADS):
            vl = slice(h * GLA_DV, (h + 1) * GLA_DV)
            o_a = _headnorm(oa_s[:, vl], gng_ref[:, vl]) * _silu(big_ref[:, D_MODEL * 2 + h * GLA_DV:
                                                                       D_MODEL * 2 + (h + 1) * GLA_DV].astype(F32))
            dn = jnp.maximum(jnp.abs(_lane_col(den, h)), jnp.exp(-_lane_col(m_new, h)))
            hh = num_s[:, vl] / dn
            o_b = _headnorm(hh, mng_ref[:, vl]) * _sigmoid(
                big_ref[:, 6 * D_MODEL + h * ML_DH:6 * D_MODEL + (h + 1) * ML_DH].astype(F32))
            ga = big_ref[:, 7 * D_MODEL + h * ML_DH:7 * D_MODEL + (h + 1) * ML_DH].astype(F32)
            gb = big_ref[:, 8 * D_MODEL + h * ML_DH:8 * D_MODEL + (h + 1) * ML_DH].astype(F32)
            oa_s[:, vl] = _sigmoid(ga) * o_a + _sigmoid(gb) * o_b
        x1_ref[...] = x_ref[...] + jnp.dot(oa_s[...].astype(BF16), wout_ref[...], preferred_element_type=F32)


def _mixer_sample(x, big, sm, st_gla, st_c, st_n, st_m, st_conv, lw, li, prev_gla, prev_c):
    nb = x.shape[0]
    has_prev = prev_gla is not None
    full = lambda shape: pl.BlockSpec(shape, lambda b: tuple(0 for _ in shape))
    sps = SAMPLES_PER_STEP
    gla_spec = pl.BlockSpec((1, sps, GLA_HEADS, GLA_DK, GLA_DV), lambda b: (li, b, 0, 0, 0))
    c_spec = pl.BlockSpec((1, sps, ML_HEADS, ML_DH, ML_DH), lambda b: (li, b, 0, 0, 0))
    in_specs = [
        full((nb, D_MODEL)), full((nb, BIG_COLS)), full((nb, SMALL_COLS)),
        gla_spec, c_spec,
        full((nb, ML_W)), full((nb, LANES)), full((nb, 3 * 2 * ML_W)),
        full((LANES, GLA_QK)), full((1, GLA_QK)), full((1, D_MODEL)),
        full((CONV_W, 2 * ML_W)), full((1, 2 * ML_W)), full((1, 2 * LANES)), full((1, D_MODEL)),
        full((D_MODEL, D_MODEL)),
    ]
    operands = [x, big, sm, st_gla, st_c, st_n, st_m, st_conv,
                lw["w2p"], lw["b2"], lw["gng"], lw["cw"], lw["cb"], lw["gbr"], lw["mng"], lw["wout"]]
    aliases = {}
    if has_prev:
        in_specs += [pl.BlockSpec(memory_space=pl.ANY), pl.BlockSpec(memory_space=pl.ANY)]
        operands += [prev_gla, prev_c]
        aliases = {16: 1, 17: 2}
    out_specs = [
        full((nb, D_MODEL)),
        gla_spec, c_spec,
        full((nb, ML_W)), full((nb, LANES)), full((nb, 3 * 2 * ML_W)),
    ]
    out_shape = [
        jax.ShapeDtypeStruct((nb, D_MODEL), F32),
        jax.ShapeDtypeStruct(st_gla.shape, F32),
        jax.ShapeDtypeStruct(st_c.shape, F32),
        jax.ShapeDtypeStruct((nb, ML_W), F32),
        jax.ShapeDtypeStruct((nb, LANES), F32),
        jax.ShapeDtypeStruct((nb, 3 * 2 * ML_W), F32),
    ]
    vm = lambda cols: pltpu.VMEM((nb, cols), F32)
    scratch = [vm(GLA_QK), vm(GLA_QK), vm(GLA_QK), vm(D_MODEL), vm(ML_W), vm(ML_W), vm(ML_W),
               vm(LANES), vm(LANES), vm(D_MODEL), vm(ML_W)]
    return pl.pallas_call(
        functools.partial(_mixer_sample_kernel, nb=nb, has_prev=has_prev, sps=sps),
        grid=(nb // sps,),
        in_specs=in_specs, out_specs=out_specs, out_shape=out_shape,
        scratch_shapes=scratch,
        input_output_aliases=aliases,
        compiler_params=_params(("arbitrary",)),
        name="mixer_sample",
    )(*operands)


def _peer_select_kernel(x_ref, g_ref, wq_ref, sk_ref, h2_ref, rank_ref, p1_ref, cnt_ref, p0_ref,
                        sc_scr, *, tq, cw):
    nch = tq // cw
    hb = _rms(x_ref[...], g_ref[...]).astype(BF16)
    h2_ref[...] = hb
    q = jnp.dot(hb, wq_ref[...], preferred_element_type=F32).astype(BF16)
    for hc in range(2 * PEER_HEADS):
        for ch in range(nch):
            qs = q[ch * cw:(ch + 1) * cw, hc * LANES:(hc + 1) * LANES]
            sc_scr[hc, ch] = lax.dot_general(sk_ref[hc % 2], qs, NT_DIMS, preferred_element_type=F32)

    K = PEER_TOPK
    neg = -jnp.inf
    iota_k = lax.broadcasted_iota(jnp.int32, (K, cw), 0)
    iota_kf = iota_k.astype(F32)

    def unit(idx):
        h = idx // nch
        ch = idx % nch
        s0 = sc_scr[2 * h, ch]
        s1 = sc_scr[2 * h + 1, ch]

        sv0 = jnp.zeros((K, cw), F32)
        sv1 = jnp.zeros((K, cw), F32)
        work0, work1 = s0, s1
        for j in range(K):
            marker = -(RANK_MARK + j * RANK_MARK_STEP)
            mx0 = jnp.max(work0, axis=0, keepdims=True)
            work0 = jnp.where(work0 == mx0, marker, work0)
            sv0 = jnp.where(iota_k == j, mx0, sv0)
            yield
            mx1 = jnp.max(work1, axis=0, keepdims=True)
            work1 = jnp.where(work1 == mx1, marker, work1)
            sv1 = jnp.where(iota_k == j, mx1, sv1)
            yield
        rank1 = jnp.where(work1 <= -RANK_MARK, (-RANK_MARK - work1) * (1.0 / RANK_MARK_STEP), float(K))
        rank_ref[h, ch] = rank1.astype(BF16)
        p1_ref[h, ch] = jnp.exp(s1 - sv1[0:1, :]).astype(BF16)
        yield

        top = sv0[0:1, :] + sv1[0:1, :]
        front = sv0 + sv1[0:1, :]
        cnt = jnp.zeros((K, cw), F32)
        zsum = jnp.zeros((1, cw), F32)
        for _ in range(K):
            fm = jnp.max(front, axis=0, keepdims=True)
            j1 = jnp.min(jnp.where(front == fm, iota_kf, float(K)), axis=0, keepdims=True)
            sel = iota_kf == j1
            zsum = zsum + jnp.exp(fm - top)
            nxt = jnp.sum(jnp.where(sel, cnt, 0.0), axis=0, keepdims=True) + 1.0
            cnt = jnp.where(sel, cnt + 1.0, cnt)
            sv1n = jnp.sum(jnp.where(iota_kf == nxt, sv1, 0.0), axis=0, keepdims=True)
            sv1n = jnp.where(nxt >= float(K), neg, sv1n)
            sv0s = jnp.sum(jnp.where(sel, sv0, 0.0), axis=0, keepdims=True)
            front = jnp.where(sel, sv0s + sv1n, front)
            yield
        cntp = jnp.zeros((N_KEYS, cw), F32)
        for j in range(K):
            cntp = jnp.where(work0 == -(RANK_MARK + j * RANK_MARK_STEP), cnt[j:j + 1, :], cntp)
            if j % 4 == 3:
                yield
        cnt_ref[h, ch] = cntp
        p0_ref[h, ch] = jnp.exp(s0 - sv0[0:1, :]) * (GELU_IN_SCALE / zsum)

    def body(it, carry):
        units = [unit(2 * it), unit(2 * it + 1)]
        while units:
            for u in list(units):
                if next(u, "done") == "done":
                    units.remove(u)
        return carry

    lax.fori_loop(0, PEER_HEADS * nch // 2, body, 0)


def _peer_select(x1, g2, wq, sk, tq, cw):
    t = x1.shape[0]
    nch = tq // cw
    meta = lambda dt: jax.ShapeDtypeStruct((PEER_HEADS, t // cw, N_KEYS, cw), dt)
    mspec = pl.BlockSpec((PEER_HEADS, nch, N_KEYS, cw), lambda i: (0, i, 0, 0))
    return pl.pallas_call(
        functools.partial(_peer_select_kernel, tq=tq, cw=cw),
        grid=(t // tq,),
        in_specs=[
            pl.BlockSpec((tq, D_MODEL), lambda i: (i, 0)),
            pl.BlockSpec((1, D_MODEL), lambda i: (0, 0)),
            pl.BlockSpec((D_MODEL, PEER_HEADS * PEER_DQ), lambda i: (0, 0)),
            pl.BlockSpec((2, N_KEYS, PEER_DQ // 2), lambda i: (0, 0, 0)),
        ],
        out_specs=[pl.BlockSpec((tq, D_MODEL), lambda i: (i, 0)), mspec, mspec, mspec, mspec],
        out_shape=[jax.ShapeDtypeStruct((t, D_MODEL), BF16), meta(BF16), meta(BF16), meta(F32), meta(F32)],
        scratch_shapes=[pltpu.VMEM((2 * PEER_HEADS, nch, N_KEYS, cw), F32)],
        compiler_params=_params(("arbitrary",)),
        name="peer_select",
    )(x1, g2, wq, sk)


def _peer_dense_kernel(x_ref, h2_ref, rank_ref, p1_ref, cnt_ref, p0_ref, u_ref, vt_ref, fg_ref,
                       out_ref, acc_scr, g_scr, ht_scr, *, tb, eb, ec, cw, nblk_e, final_norm):
    ngrp = eb // ec
    j = pl.program_id(1)

    @pl.when(j == 0)
    def _():
        acc_scr[...] = jnp.zeros_like(acc_scr)

    for grp in range(ngrp):
        es = slice(grp * ec, (grp + 1) * ec)
        u_grp = pltpu.bitcast(u_ref[grp * (ec // 2):(grp + 1) * (ec // 2), :], BF16)
        ht_scr[grp] = lax.dot_general(u_grp, h2_ref[...], NT_DIMS, preferred_element_type=F32)
    for grp in range(ngrp):
        es = slice(grp * ec, (grp + 1) * ec)
        for k in range(ec // N_KEYS):
            r = grp * (ec // N_KEYS) + k
            ks = slice(grp * ec + k * N_KEYS, grp * ec + (k + 1) * N_KEYS)
            for ch in range(tb // cw):
                cs = slice(ch * cw, (ch + 1) * cw)
                w = jnp.zeros((N_KEYS, cw), BF16)
                for h in range(PEER_HEADS):
                    cnt_row = jnp.broadcast_to(cnt_ref[h, ch, r:r + 1, :], (N_KEYS, cw)).astype(BF16)
                    p0_row = jnp.broadcast_to(p0_ref[h, ch, r:r + 1, :], (N_KEYS, cw)).astype(BF16)
                    w = w + jnp.where(rank_ref[h, ch] < cnt_row, p1_ref[h, ch] * p0_row, jnp.zeros_like(w))
                z = ht_scr[grp, k * N_KEYS:(k + 1) * N_KEYS, cs]
                g_scr[ks, cs] = w * (z * (1.0 + lax.erf(z))).astype(BF16)
        acc_scr[...] += jnp.dot(pltpu.bitcast(vt_ref[:, es], BF16), g_scr[es, :],
                                preferred_element_type=F32)

    @pl.when(j == nblk_e - 1)
    def _():
        y = x_ref[...] + acc_scr[...].T
        if final_norm:
            y = _rms(y, fg_ref[...])
        out_ref[...] = y


def _peer_dense(x1, h2, rank, p1, cnt, p0, u_bf, vt_bf, fg, tb, eb, cw, final_norm):
    t = x1.shape[0]
    nch = tb // cw
    nblk_e = N_EXPERTS // eb
    ec = min(eb, 512)
    mspec = pl.BlockSpec((PEER_HEADS, nch, N_KEYS, cw), lambda i, j: (0, i, 0, 0))
    rspec = pl.BlockSpec((PEER_HEADS, nch, eb // N_KEYS, cw), lambda i, j: (0, i, j, 0))
    return pl.pallas_call(
        functools.partial(_peer_dense_kernel, tb=tb, eb=eb, ec=ec, cw=cw, nblk_e=nblk_e,
                          final_norm=final_norm),
        grid=(t // tb, nblk_e),
        in_specs=[
            pl.BlockSpec((tb, D_MODEL), lambda i, j: (i, 0)),
            pl.BlockSpec((tb, D_MODEL), lambda i, j: (i, 0)),
            mspec, mspec, rspec, rspec,
            pl.BlockSpec((eb // 2, D_MODEL), lambda i, j: (j, 0)),
            pl.BlockSpec((D_MODEL // 2, eb), lambda i, j: (0, j)),
            pl.BlockSpec((1, D_MODEL), lambda i, j: (0, 0)),
        ],
        out_specs=pl.BlockSpec((tb, D_MODEL), lambda i, j: (i, 0)),
        out_shape=jax.ShapeDtypeStruct((t, D_MODEL), F32),
        scratch_shapes=[pltpu.VMEM((D_MODEL, tb), F32), pltpu.VMEM((eb, tb), BF16),
                        pltpu.VMEM((eb // ec, ec, tb), F32)],
        compiler_params=_params(("arbitrary", "arbitrary")),
        name="peer_dense",
    )(x1, h2, rank, p1, cnt, p0, u_bf, vt_bf, fg)


def _layer_weights(li, w_in, gla_w2, gla_b2, gla_norm_g, conv_w, conv_b, ml_i_b, ml_f_b, ml_norm_g,
                   w_out, peer_wq, peer_subkeys, peer_u, peer_v):
    w = w_in[li]
    o = 0
    segs = {}
    for name, width in (("aq", GLA_QK), ("ak", GLA_QK), ("av", D_MODEL), ("ag", D_MODEL),
                        ("alr", GLA_GATE_RANK), ("bq", ML_W), ("bk", ML_W), ("bv", ML_W),
                        ("bi", ML_HEADS), ("bf", ML_HEADS), ("bo", ML_W),
                        ("ga", D_MODEL), ("gb", D_MODEL)):
        segs[name] = w[:, o:o + width]
        o += width
    seg_w = WSEG_BLOCKS * D_MODEL
    starts = (0, seg_w + GLA_GATE_RANK, 2 * seg_w + GLA_GATE_RANK + 2 * ML_HEADS)
    w_segs = tuple(w[:, s:s + seg_w].astype(BF16) for s in starts)
    pad = lambda a: jnp.pad(a, ((0, 0), (0, LANES - a.shape[1])))
    w_small = jnp.concatenate([pad(segs["alr"]), pad(segs["bi"]), pad(segs["bf"])], axis=1).astype(BF16)
    w_gt = jnp.concatenate([segs["bi"], segs["bf"]], axis=1).T.astype(BF16)
    gate_b = jnp.concatenate([ml_i_b[li], ml_f_b[li]])
    u_packed, vt_packed = _table_prep(peer_u, peer_v, li)
    return dict(
        w_segs=w_segs, w_small=w_small, w_gt=w_gt,
        w2p=jnp.pad(gla_w2[li], ((0, LANES - GLA_GATE_RANK), (0, 0))),
        b2=gla_b2[li][None, :], gng=gla_norm_g[li][None, :],
        cw=conv_w[li], cb=conv_b[li][None, :],
        gbr=jnp.concatenate([jnp.pad(ml_i_b[li], (0, LANES - ML_HEADS)),
                             jnp.pad(ml_f_b[li], (0, LANES - ML_HEADS))])[None, :],
        gbc=gate_b[:, None], mng=ml_norm_g[li][None, :],
        wout=w_out[li].astype(BF16),
        wq=peer_wq[li].astype(BF16), sk=peer_subkeys[li].astype(BF16),
        u=u_packed, vt=vt_packed,
    )


def _table_prep_kernel(u_ref, v_ref, up_ref, vtp_ref):
    up_ref[...] = pltpu.bitcast((u_ref[0] * GELU_IN_SCALE).astype(BF16), jnp.uint32)
    vtp_ref[...] = pltpu.bitcast(v_ref[0].T.astype(BF16), jnp.uint32)


def _table_prep(u_tabs, v_tabs, li, eb=1024):
    _, ne, d = u_tabs.shape
    return pl.pallas_call(
        _table_prep_kernel,
        grid=(ne // eb,),
        in_specs=[pl.BlockSpec((1, eb, d), lambda i: (li, i, 0)), pl.BlockSpec((1, eb, d), lambda i: (li, i, 0))],
        out_specs=[pl.BlockSpec((eb // 2, d), lambda i: (i, 0)), pl.BlockSpec((d // 2, eb), lambda i: (0, i))],
        out_shape=[jax.ShapeDtypeStruct((ne // 2, d), jnp.uint32),
                   jax.ShapeDtypeStruct((d // 2, ne), jnp.uint32)],
        compiler_params=_params(("arbitrary",)),
        name="table_prep",
    )(u_tabs, v_tabs)


def _pick(n, cands):
    for c in cands:
        if n % c == 0:
            return c
    return n


def _peer(x1, g2, lw, fg, final_norm):
    t = x1.shape[0]
    tq = _pick(t, (512, 256, 128))
    tb = _pick(t, (512, 256, 128))
    cw = _pick(tb, (2 * LANES, LANES))
    h2, rank, p1, cnt, p0 = _peer_select(x1, g2, lw["wq"], lw["sk"], tq, cw)
    return _peer_dense(x1, h2, rank, p1, cnt, p0, lw["u"], lw["vt"], fg, tb, 2048, cw, final_norm)


def kernel(x_prompt, x_sample, state_gla, state_mlstm_c, state_mlstm_n, state_mlstm_m, state_conv,
           norm1_g, w_in, gla_w2, gla_b2, gla_norm_g, conv_w, conv_b, ml_i_b, ml_f_b, ml_norm_g,
           w_out, norm2_g, peer_wq, peer_subkeys, peer_u, peer_v, final_g):
    depth = w_in.shape[0]
    bsz, seq, _ = x_prompt.shape
    nb = x_sample.shape[0]
    assert seq % CHUNK == 0 and x_sample.shape[1] == 1 and nb % LANES == 0
    tp = bsz * seq
    tm = _pick(tp, (2048, 1024, 512, 256, 128))
    tc = _pick(seq, (256, 128, 64))
    npar = _pick(bsz, (2, 1))
    fg = final_g[None, :]

    xp = x_prompt.reshape(tp, D_MODEL)
    xs = x_sample.reshape(nb, D_MODEL)
    p_out = [[] for _ in range(5)]
    s_out = [[] for _ in range(5)]
    s_gla = s_c = None
    for li in range(depth):
        lw = _layer_weights(li, w_in, gla_w2, gla_b2, gla_norm_g, conv_w, conv_b, ml_i_b, ml_f_b,
                            ml_norm_g, w_out, peer_wq, peer_subkeys, peer_u, peer_v)
        g1 = norm1_g[li][None, :]
        g2 = norm2_g[li][None, :]
        last = li == depth - 1

        big, sm, gt = _inproj(xp, g1, lw["w_segs"], lw["w_small"], lw["w_gt"], tm)
        gtc = gt.reshape(8, bsz, seq // CHUNK, CHUNK).transpose(1, 2, 0, 3)
        x1, gla, c, n, m, conv = _mixer_prompt(
            xp.reshape(bsz, seq, D_MODEL), big.reshape(bsz, seq, BIG_COLS),
            sm.reshape(bsz, seq, SMALL_COLS), gtc, lw, bsz, seq, tc, npar)
        xp = _peer(x1.reshape(tp, D_MODEL), g2, lw, fg, last)
        for lst, val in zip(p_out, (gla, c, n, m[:, 0, :ML_HEADS], conv)):
            lst.append(val)

        big, sm, _ = _inproj(xs, g1, lw["w_segs"], lw["w_small"], lw["w_gt"], nb)
        x1, s_gla, s_c, n, m, conv = _mixer_sample(
            xs, big, sm, state_gla, state_mlstm_c,
            state_mlstm_n[li].reshape(nb, ML_W),
            jnp.pad(state_mlstm_m[li], ((0, 0), (0, LANES - ML_HEADS))),
            state_conv[li].reshape(nb, (CONV_W - 1) * 2 * ML_W), lw, li, s_gla, s_c)
        xs = _peer(x1, g2, lw, fg, last)
        for lst, val in zip(s_out[2:], (n.reshape(nb, ML_HEADS, ML_DH), m[:, :ML_HEADS],
                                        conv.reshape(nb, CONV_W - 1, 2 * ML_W))):
            lst.append(val)

    y_prompt = xp.reshape(bsz, seq, D_MODEL)
    y_sample = xs.reshape(nb, 1, D_MODEL)
    return (y_prompt, y_sample, *[jnp.stack(v) for v in p_out],
            s_gla, s_c, *[jnp.stack(v) for v in s_out[2:]])
```

```python
import functools

import jax
import jax.numpy as jnp
from jax import lax
from jax.experimental import pallas as pl
from jax.experimental.pallas import tpu as pltpu

F32 = jnp.float32
BF16 = jnp.bfloat16
HIGHEST = lax.Precision.HIGHEST

D_MODEL = 1024
GLA_HEADS = 4
GLA_DK = 128
GLA_DV = 256
GLA_GATE_RANK = 16
GLA_TAU = 16.0
GLA_QK = GLA_HEADS * GLA_DK
ML_HEADS = 4
ML_DH = 256
ML_W = ML_HEADS * ML_DH
CONV_W = 4
CHUNK = 64
PEER_HEADS = 8
PEER_DQ = 256
N_KEYS = 128
N_EXPERTS = N_KEYS * N_KEYS
PEER_TOPK = 16
EPS = 1e-6
GELU_IN_SCALE = 0.7071067811865476
RANK_MARK = 2.0 ** 100
RANK_MARK_STEP = 2.0 ** 96

LANES = 128
BIG_COLS = 9 * D_MODEL
WSEG_BLOCKS = 3
SAMPLES_PER_STEP = 2
HEADS_PER_ROUND = 4
SMALL_COLS = 3 * LANES
VMEM_LIMIT_BYTES = 56 * 1024 * 1024

NT_DIMS = (((1,), (1,)), ((), ()))
TN_DIMS = (((0,), (0,)), ((), ()))


def _params(sem, flags=None):
    return pltpu.CompilerParams(dimension_semantics=sem, vmem_limit_bytes=VMEM_LIMIT_BYTES, flags=flags)


def _log_sigmoid(x):
    return jnp.minimum(x, 0.0) - jnp.log1p(jnp.exp(-jnp.abs(x)))


def _sigmoid(x):
    return 1.0 / (1.0 + jnp.exp(-x))


def _silu(x):
    return x * _sigmoid(x)


def _rms(x, g):
    return x * lax.rsqrt(jnp.mean(x * x, axis=-1, keepdims=True) + EPS) * g


def _lane_col(a, j):
    lane = lax.broadcasted_iota(jnp.int32, a.shape, 1)
    return jnp.sum(jnp.where(lane == j, a, 0.0), axis=1, keepdims=True)


def _inproj_kernel(x_ref, g_ref, wa_ref, wb_ref, wc_ref, wsm_ref, wgt_ref, big_ref, sm_ref, gt_ref, h_scr):
    j = pl.program_id(1)

    @pl.when(j == 0)
    def _():
        hb = _rms(x_ref[...], g_ref[...]).astype(BF16)
        h_scr[...] = hb
        sm_ref[...] = jnp.dot(hb, wsm_ref[...], preferred_element_type=F32)
        gt_ref[...] = lax.dot_general(wgt_ref[...], hb, NT_DIMS, preferred_element_type=F32)

    for s, w_ref in enumerate((wa_ref, wb_ref, wc_ref)):
        @pl.when((j >= s * WSEG_BLOCKS) & (j < (s + 1) * WSEG_BLOCKS))
        def _(w_ref=w_ref):
            big_ref[...] = jnp.dot(h_scr[...], w_ref[...], preferred_element_type=F32).astype(BF16)


def _inproj(x, g, w_segs, w_small, w_gt, tm):
    t = x.shape[0]
    tn = D_MODEL

    def wspec(s):
        return pl.BlockSpec((D_MODEL, tn), lambda i, j: (0, jnp.clip(j - s * WSEG_BLOCKS, 0, WSEG_BLOCKS - 1)))

    return pl.pallas_call(
        _inproj_kernel,
        grid=(t // tm, BIG_COLS // tn),
        in_specs=[
            pl.BlockSpec((tm, D_MODEL), lambda i, j: (i, 0)),
            pl.BlockSpec((1, D_MODEL), lambda i, j: (0, 0)),
            wspec(0), wspec(1), wspec(2),
            pl.BlockSpec((D_MODEL, SMALL_COLS), lambda i, j: (0, 0)),
            pl.BlockSpec((8, D_MODEL), lambda i, j: (0, 0)),
        ],
        out_specs=[
            pl.BlockSpec((tm, tn), lambda i, j: (i, j)),
            pl.BlockSpec((tm, SMALL_COLS), lambda i, j: (i, 0)),
            pl.BlockSpec((8, tm), lambda i, j: (0, i)),
        ],
        out_shape=[
            jax.ShapeDtypeStruct((t, BIG_COLS), BF16),
            jax.ShapeDtypeStruct((t, SMALL_COLS), F32),
            jax.ShapeDtypeStruct((8, t), F32),
        ],
        scratch_shapes=[pltpu.VMEM((tm, D_MODEL), BF16)],
        compiler_params=_params(("arbitrary", "arbitrary")),
        name="inproj",
    )(x, g, *w_segs, w_small, w_gt)


def _headnorm(o, g_row):
    return o * lax.rsqrt(jnp.mean(o * o, axis=-1, keepdims=True) + EPS) * g_row


def _mixer_prompt_kernel(x_ref, aq_ref, ak_ref, av_ref, ag_ref, bq_ref, bk_ref, bv_ref, bo_ref,
                         ga_ref, gb_ref, sm_ref, gt_ref, w2_ref, b2_ref, gng_ref, cw_ref, cb_ref,
                         gbr_ref, gbc_ref, mng_ref, wout_ref,
                         x1_ref, gla_ref, c_ref, n_ref, m_ref, conv_ref,
                         st_scr, xp_scr, qk_scr, mix_scr, *, tc, nblk, npar):
    i = pl.program_id(1)
    L = CHUNK

    @pl.when(i == 0)
    def _init():
        st_scr[...] = jnp.zeros_like(st_scr)
        c_ref[...] = jnp.zeros_like(c_ref)
        n_ref[...] = jnp.zeros_like(n_ref)
        m_ref[...] = jnp.zeros_like(m_ref)
        xp_scr[:, 0:8, :] = jnp.zeros((npar, 8, 2 * ML_W), F32)

    for p in range(npar):
        xp_scr[p, 8:8 + tc, 0:ML_W] = bq_ref[p].astype(F32)
        xp_scr[p, 8:8 + tc, ML_W:2 * ML_W] = bk_ref[p].astype(F32)
        y = cb_ref[...]
        for j in range(CONV_W):
            y = y + cw_ref[j:j + 1, :] * xp_scr[p, 5 + j:5 + j + tc, :]
        qk = _silu(y)
        qk_scr[p, :, 0:ML_W] = qk[:, 0:ML_W].astype(BF16)
        qk_scr[p, :, ML_W:2 * ML_W] = (qk[:, ML_W:2 * ML_W] * (ML_DH ** -0.5)).astype(BF16)
        xp_scr[p, 5:8, :] = xp_scr[p, tc + 5:tc + 8, :]

    @pl.when(i == nblk - 1)
    def _():
        conv_ref[...] = xp_scr[:, 5:8, :]

    row = lax.broadcasted_iota(jnp.int32, (L, L), 0)
    col = lax.broadcasted_iota(jnp.int32, (L, L), 1)
    causal = col <= row
    tril = causal.astype(F32)
    triu = (row <= col).astype(F32)
    lane_row = lax.broadcasted_iota(jnp.int32, (1, LANES), 1)

    def gates(p, c, rows):
        sm = sm_ref[p, rows, :]
        z = jnp.dot(sm[:, 0:LANES], w2_ref[...], precision=HIGHEST, preferred_element_type=F32) + b2_ref[...]
        log_a = _log_sigmoid(z) * (1.0 / GLA_TAU)
        bc = jnp.dot(tril, log_a, precision=HIGHEST, preferred_element_type=F32)
        li_c_all = sm[:, LANES:2 * LANES] + gbr_ref[:, 0:LANES]
        lf_c_all = _log_sigmoid(sm[:, 2 * LANES:3 * LANES] + gbr_ref[:, LANES:2 * LANES])
        bcum_c_all = jnp.dot(tril, lf_c_all, precision=HIGHEST, preferred_element_type=F32)
        gtb = gt_ref[p, c] + gbc_ref[...]
        bcum_r_all = jnp.dot(_log_sigmoid(gtb), triu, precision=HIGHEST, preferred_element_type=F32)
        return bc, li_c_all, bcum_c_all, gtb, bcum_r_all

    def head(p, h, rows, shared):
        bc, li_c_all, bcum_c_all, gtb, bcum_r_all = shared
        sl = slice(h * GLA_DK, (h + 1) * GLA_DK)
        vl = slice(h * GLA_DV, (h + 1) * GLA_DV)
        b = bc[:, sl]
        mid = b[L // 2 - 1:L // 2, :]
        bl = b[L - 1:L, :]
        q = aq_ref[p, rows, sl].astype(F32) * (GLA_DK ** -0.5)
        k = ak_ref[p, rows, sl].astype(F32)
        v = av_ref[p, rows, vl]
        q_in = (q * jnp.exp(b)).astype(BF16)
        q_at = (q * jnp.exp(b - mid)).astype(BF16)
        yield
        k_at = (k * jnp.exp(mid - b)).astype(BF16)
        k_out = (k * jnp.exp(bl - b)).astype(BF16)
        yield
        att = lax.dot_general(q_at, k_at, NT_DIMS, preferred_element_type=F32)
        att = jnp.where(causal, att, 0.0)
        st = st_scr[p, h]
        yield
        o = lax.dot_general(q_in, st.astype(BF16), NT_DIMS, preferred_element_type=F32)
        yield
        o = o + jnp.dot(att.astype(BF16), v, preferred_element_type=F32)
        yield
        st_scr[p, h] = st * jnp.exp(bl) + lax.dot_general(v, k_out, TN_DIMS, preferred_element_type=F32)
        yield
        o_a = _headnorm(o, gng_ref[:, vl]) * _silu(ag_ref[p, rows, vl].astype(F32))
        yield

        li_c = _lane_col(li_c_all, h)
        b_c = _lane_col(bcum_c_all, h)
        li_r = gtb[h:h + 1, :]
        b_r = bcum_r_all[ML_HEADS + h:ML_HEADS + h + 1, :]
        m_prev = jnp.sum(jnp.where(lane_row == h, m_ref[p], 0.0), axis=1, keepdims=True)
        a_int = b_c + m_prev
        dm = jnp.where(causal, b_c - b_r + li_r, -jnp.inf)
        m_t = jnp.maximum(a_int, jnp.max(dm, axis=1, keepdims=True))
        yield
        w_int = jnp.exp(a_int - m_t)
        dexp = jnp.exp(dm - m_t)
        qb = qk_scr[p, rows, vl]
        kb = qk_scr[p, rows, ML_W + h * ML_DH:ML_W + (h + 1) * ML_DH]
        vb = bv_ref[p, rows, vl]
        s = lax.dot_general(qb, kb, NT_DIMS, preferred_element_type=F32) * dexp
        yield
        cst = c_ref[p, h]
        num = w_int * jnp.dot(qb, cst.astype(BF16), preferred_element_type=F32)
        yield
        num = num + jnp.dot(s.astype(BF16), vb, preferred_element_type=F32)
        nrow = n_ref[p, h:h + 1, :]
        den = (w_int * jnp.sum(qb.astype(F32) * nrow, axis=1, keepdims=True)
               + jnp.sum(s, axis=1, keepdims=True))
        yield
        hh = num / jnp.maximum(jnp.abs(den), jnp.exp(-m_t))
        b_last = b_c[L - 1:L, :]
        g_c = b_last - b_c + li_c
        m_new = jnp.maximum(b_last + m_prev, jnp.max(g_c, axis=0, keepdims=True))
        f_c = jnp.exp(b_last + m_prev - m_new)
        w_s = jnp.exp(g_c - m_new)
        kw = kb.astype(F32) * w_s
        yield
        c_ref[p, h] = f_c * cst + lax.dot_general(kw.astype(BF16), vb, TN_DIMS, preferred_element_type=F32)
        n_ref[p, h:h + 1, :] = f_c * nrow + jnp.sum(kw, axis=0, keepdims=True)
        m_ref[p] = jnp.where(lane_row == h, m_new, m_ref[p])
        yield
        o_b = _headnorm(hh, mng_ref[:, vl]) * _sigmoid(bo_ref[p, rows, vl].astype(F32))
        yield
        mix = (_sigmoid(ga_ref[p, rows, vl].astype(F32)) * o_a
               + _sigmoid(gb_ref[p, rows, vl].astype(F32)) * o_b)
        mix_scr[p, rows, vl] = mix.astype(BF16)

    def chunk(c, carry):
        rows = pl.ds(pl.multiple_of(c * L, L), L)
        shared = [gates(p, c, rows) for p in range(npar)]
        for h0 in range(0, GLA_HEADS, HEADS_PER_ROUND):
            units = [head(p, h, rows, shared[p]) for h in range(h0, h0 + HEADS_PER_ROUND) for p in range(npar)]
            while units:
                for u in list(units):
                    if next(u, "done") == "done":
                        units.remove(u)
        return carry

    lax.fori_loop(0, tc // L, chunk, 0)

    for p in range(npar):
        x1_ref[p] = x_ref[p] + jnp.dot(mix_scr[p], wout_ref[...], preferred_element_type=F32)

    @pl.when(i == nblk - 1)
    def _fin():
        for p in range(npar):
            for h in range(GLA_HEADS):
                gla_ref[p, h] = st_scr[p, h].T


def _mixer_prompt(x, big, sm, gtc, lw, bsz, seq, tc, npar):
    nblk = seq // tc

    def seg(width, idx):
        return pl.BlockSpec((npar, tc, width), lambda b, i: (b, i, idx))

    const = lambda shape: pl.BlockSpec(shape, lambda b, i: tuple(0 for _ in shape))
    in_specs = [
        seg(D_MODEL, 0),
        seg(GLA_QK, 0), seg(GLA_QK, 1),
        seg(D_MODEL, 1), seg(D_MODEL, 2),
        seg(D_MODEL, 3), seg(D_MODEL, 4), seg(D_MODEL, 5),
        seg(D_MODEL, 6), seg(D_MODEL, 7), seg(D_MODEL, 8),
        seg(SMALL_COLS, 0),
        pl.BlockSpec((npar, tc // CHUNK, 8, CHUNK), lambda b, i: (b, i, 0, 0)),
        const((LANES, GLA_QK)), const((1, GLA_QK)), const((1, D_MODEL)),
        const((CONV_W, 2 * ML_W)), const((1, 2 * ML_W)),
        const((1, 2 * LANES)), const((8, 1)), const((1, D_MODEL)),
        const((D_MODEL, D_MODEL)),
    ]
    out_specs = [
        seg(D_MODEL, 0),
        pl.BlockSpec((npar, GLA_HEADS, GLA_DK, GLA_DV), lambda b, i: (b, 0, 0, 0)),
        pl.BlockSpec((npar, ML_HEADS, ML_DH, ML_DH), lambda b, i: (b, 0, 0, 0)),
        pl.BlockSpec((npar, ML_HEADS, ML_DH), lambda b, i: (b, 0, 0)),
        pl.BlockSpec((npar, 1, LANES), lambda b, i: (b, 0, 0)),
        pl.BlockSpec((npar, CONV_W - 1, 2 * ML_W), lambda b, i: (b, 0, 0)),
    ]
    out_shape = [
        jax.ShapeDtypeStruct((bsz, seq, D_MODEL), F32),
        jax.ShapeDtypeStruct((bsz, GLA_HEADS, GLA_DK, GLA_DV), F32),
        jax.ShapeDtypeStruct((bsz, ML_HEADS, ML_DH, ML_DH), F32),
        jax.ShapeDtypeStruct((bsz, ML_HEADS, ML_DH), F32),
        jax.ShapeDtypeStruct((bsz, 1, LANES), F32),
        jax.ShapeDtypeStruct((bsz, CONV_W - 1, 2 * ML_W), F32),
    ]
    scratch = [
        pltpu.VMEM((npar, GLA_HEADS, GLA_DV, GLA_DK), F32),
        pltpu.VMEM((npar, tc + 8, 2 * ML_W), F32),
        pltpu.VMEM((npar, tc, 2 * ML_W), BF16),
        pltpu.VMEM((npar, tc, D_MODEL), BF16),
    ]
    return pl.pallas_call(
        functools.partial(_mixer_prompt_kernel, tc=tc, nblk=nblk, npar=npar),
        grid=(bsz // npar, nblk),
        in_specs=in_specs, out_specs=out_specs, out_shape=out_shape,
        scratch_shapes=scratch,
        compiler_params=_params(("arbitrary", "arbitrary")),
        name="mixer_prompt",
    )(x, big, big, big, big, big, big, big, big, big, big, sm, gtc,
      lw["w2p"], lw["b2"], lw["gng"], lw["cw"], lw["cb"], lw["gbr"], lw["gbc"], lw["mng"], lw["wout"])


def _col_bcast(row):
    return jnp.broadcast_to(row, (LANES, LANES)).T


def _mixer_sample_kernel(*refs, nb, has_prev, sps):
    n_in = 18 if has_prev else 16
    (x_ref, big_ref, sm_ref, gla_in, c_in, n_ref, m_ref, conv_ref,
     w2_ref, b2_ref, gng_ref, cw_ref, cb_ref, gbr_ref, mng_ref, wout_ref) = refs[:16]
    x1_ref, gla_out, c_out, n_o, m_o, conv_o = refs[n_in:n_in + 6]
    a_s, qa_s, ka_s, va_s, kw_s, qb_s, vb_s, fc_s, den_s, oa_s, num_s = refs[n_in + 6:]
    gla_ref, c_ref, gla_o, c_o = gla_in.at[0], c_in.at[0], gla_out.at[0], c_out.at[0]
    b = pl.program_id(0)
    seg = lambda idx: slice(idx * D_MODEL, (idx + 1) * D_MODEL)
    lane = lax.broadcasted_iota(jnp.int32, (nb, LANES), 1)

    @pl.when(b == 0)
    def _prep():
        sm = sm_ref[...]
        z = jnp.dot(sm[:, 0:LANES], w2_ref[...], precision=HIGHEST, preferred_element_type=F32) + b2_ref[...]
        a_s[...] = jnp.exp(_log_sigmoid(z) * (1.0 / GLA_TAU))
        qa_s[...] = big_ref[:, 0:GLA_QK].astype(F32) * (GLA_DK ** -0.5)
        ka_s[...] = big_ref[:, GLA_QK:2 * GLA_QK].astype(F32)
        va_s[...] = big_ref[:, seg(1)].astype(F32)
        vb_s[...] = big_ref[:, seg(5)].astype(F32)
        xq = big_ref[:, seg(3)].astype(F32)
        xk = big_ref[:, seg(4)].astype(F32)
        w = 2 * ML_W
        y = cb_ref[...]
        for j in range(CONV_W - 1):
            y = y + cw_ref[j:j + 1, :] * conv_ref[:, j * w:(j + 1) * w]
        y = y + cw_ref[CONV_W - 1:CONV_W, :] * jnp.concatenate([xq, xk], axis=1)
        qk = _silu(y)
        qb = qk[:, 0:ML_W]
        kb = qk[:, ML_W:w] * (ML_DH ** -0.5)
        qb_s[...] = qb
        conv_o[:, 0:2 * w] = conv_ref[:, w:3 * w]
        conv_o[:, 2 * w:2 * w + ML_W] = xq
        conv_o[:, 2 * w + ML_W:3 * w] = xk

        li = sm[:, LANES:2 * LANES] + gbr_ref[:, 0:LANES]
        lf = _log_sigmoid(sm[:, 2 * LANES:3 * LANES] + gbr_ref[:, LANES:2 * LANES])
        m_old = m_ref[...]
        m_new = jnp.maximum(lf + m_old, li)
        f_c = jnp.exp(lf + m_old - m_new)
        w_s = jnp.exp(li - m_new)
        m_o[...] = m_new
        fc_s[...] = f_c
        den = jnp.zeros((nb, LANES), F32)
        for h in range(ML_HEADS):
            vl = slice(h * ML_DH, (h + 1) * ML_DH)
            kw = kb[:, vl] * _lane_col(w_s, h)
            kw_s[:, vl] = kw
            n_new = _lane_col(f_c, h) * n_ref[:, vl] + kw
            n_o[:, vl] = n_new
            den = jnp.where(lane == h, jnp.sum(qb[:, vl] * n_new, axis=1, keepdims=True), den)
        den_s[...] = den
        oa_s[...] = jnp.zeros_like(oa_s)
        num_s[...] = jnp.zeros_like(num_s)

    tile = pl.ds(pl.multiple_of(lax.shift_right_logical(b * sps, 3) * 8, 8), 8)
    lane_row = lax.broadcasted_iota(jnp.int32, (1, LANES), 1)

    for s in range(sps):
        sub = jnp.bitwise_and(b * sps + s, 7)

        def row_get(ref, cols, sub=sub):
            t = ref[tile, cols]
            pick = lax.broadcasted_iota(jnp.int32, t.shape, 0) == sub
            return jnp.sum(jnp.where(pick, t, 0.0), axis=0, keepdims=True)

        def row_set(ref, cols, val, sub=sub):
            t = ref[tile, cols]
            pick = lax.broadcasted_iota(jnp.int32, t.shape, 0) == sub
            ref[tile, cols] = jnp.where(pick, val, t)

        fc_row = row_get(fc_s, slice(0, LANES))
        for h in range(GLA_HEADS):
            sl = slice(h * GLA_DK, (h + 1) * GLA_DK)
            vl = slice(h * GLA_DV, (h + 1) * GLA_DV)
            a_c = _col_bcast(row_get(a_s, sl))
            k_c = _col_bcast(row_get(ka_s, sl))
            q_c = _col_bcast(row_get(qa_s, sl))
            v_row = row_get(va_s, vl)
            s_old = gla_ref[s, h]
            halves = []
            for p in range(GLA_DV // LANES):
                ls = slice(p * LANES, (p + 1) * LANES)
                s_new = s_old[:, ls] * a_c + k_c * v_row[:, ls]
                gla_o[s, h, :, ls] = s_new
                halves.append(jnp.sum(q_c * s_new, axis=0, keepdims=True))
            row_set(oa_s, vl, jnp.concatenate(halves, axis=1))

            f_c = jnp.sum(jnp.where(lane_row == h, fc_row, 0.0), axis=1, keepdims=True)
            vb_row = row_get(vb_s, vl)
            acc = [jnp.zeros((1, LANES), F32) for _ in range(ML_DH // LANES)]
            for r in range(ML_DH // LANES):
                rs = slice(r * LANES, (r + 1) * LANES)
                ks = slice(h * ML_DH + r * LANES, h * ML_DH + (r + 1) * LANES)
                kw_c = _col_bcast(row_get(kw_s, ks))
                qb_c = _col_bcast(row_get(qb_s, ks))
                for p in range(ML_DH // LANES):
                    ls = slice(p * LANES, (p + 1) * LANES)
                    c_new = f_c * c_ref[s, h, rs, ls] + kw_c * vb_row[:, ls]
                    c_o[s, h, rs, ls] = c_new
                    acc[p] = acc[p] + jnp.sum(qb_c * c_new, axis=0, keepdims=True)
            row_set(num_s, vl, jnp.concatenate(acc, axis=1))

    @pl.when(b == nb // sps - 1)
    def _post():
        m_new = m_o[...]
        den = den_s[...]
        for h in range(GLA_HEADS):
            vl = slice(h * GLA_DV, (h + 1) * GLA_DV)
            o_a = _headnorm(oa_s[:, vl], gng_ref[:, vl]) * _silu(big_ref[:, D_MODEL * 2 + h * GLA_DV:
                                                                       D_MODEL * 2 + (h + 1) * GLA_DV].astype(F32))
            dn = jnp.maximum(jnp.abs(_lane_col(den, h)), jnp.exp(-_lane_col(m_new, h)))
            hh = num_s[:, vl] / dn
            o_b = _headnorm(hh, mng_ref[:, vl]) * _sigmoid(
                big_ref[:, 6 * D_MODEL + h * ML_DH:6 * D_MODEL + (h + 1) * ML_DH].astype(F32))
            ga = big_ref[:, 7 * D_MODEL + h * ML_DH:7 * D_MODEL + (h + 1) * ML_DH].astype(F32)
            gb = big_ref[:, 8 * D_MODEL + h * ML_DH:8 * D_MODEL + (h + 1) * ML_DH].astype(F32)
            oa_s[:, vl] = _sigmoid(ga) * o_a + _sigmoid(gb) * o_b
        x1_ref[...] = x_ref[...] + jnp.dot(oa_s[...].astype(BF16), wout_ref[...], preferred_element_type=F32)


def _mixer_sample(x, big, sm, st_gla, st_c, st_n, st_m, st_conv, lw, li, prev_gla, prev_c):
    nb = x.shape[0]
    has_prev = prev_gla is not None
    full = lambda shape: pl.BlockSpec(shape, lambda b: tuple(0 for _ in shape))
    sps = SAMPLES_PER_STEP
    gla_spec = pl.BlockSpec((1, sps, GLA_HEADS, GLA_DK, GLA_DV), lambda b: (li, b, 0, 0, 0))
    c_spec = pl.BlockSpec((1, sps, ML_HEADS, ML_DH, ML_DH), lambda b: (li, b, 0, 0, 0))
    in_specs = [
        full((nb, D_MODEL)), full((nb, BIG_COLS)), full((nb, SMALL_COLS)),
        gla_spec, c_spec,
        full((nb, ML_W)), full((nb, LANES)), full((nb, 3 * 2 * ML_W)),
        full((LANES, GLA_QK)), full((1, GLA_QK)), full((1, D_MODEL)),
        full((CONV_W, 2 * ML_W)), full((1, 2 * ML_W)), full((1, 2 * LANES)), full((1, D_MODEL)),
        full((D_MODEL, D_MODEL)),
    ]
    operands = [x, big, sm, st_gla, st_c, st_n, st_m, st_conv,
                lw["w2p"], lw["b2"], lw["gng"], lw["cw"], lw["cb"], lw["gbr"], lw["mng"], lw["wout"]]
    aliases = {}
    if has_prev:
        in_specs += [pl.BlockSpec(memory_space=pl.ANY), pl.BlockSpec(memory_space=pl.ANY)]
        operands += [prev_gla, prev_c]
        aliases = {16: 1, 17: 2}
    out_specs = [
        full((nb, D_MODEL)),
        gla_spec, c_spec,
        full((nb, ML_W)), full((nb, LANES)), full((nb, 3 * 2 * ML_W)),
    ]
    out_shape = [
        jax.ShapeDtypeStruct((nb, D_MODEL), F32),
        jax.ShapeDtypeStruct(st_gla.shape, F32),
        jax.ShapeDtypeStruct(st_c.shape, F32),
        jax.ShapeDtypeStruct((nb, ML_W), F32),
        jax.ShapeDtypeStruct((nb, LANES), F32),
        jax.ShapeDtypeStruct((nb, 3 * 2 * ML_W), F32),
    ]
    vm = lambda cols: pltpu.VMEM((nb, cols), F32)
    scratch = [vm(GLA_QK), vm(GLA_QK), vm(GLA_QK), vm(D_MODEL), vm(ML_W), vm(ML_W), vm(ML_W),
               vm(LANES), vm(LANES), vm(D_MODEL), vm(ML_W)]
    return pl.pallas_call(
        functools.partial(_mixer_sample_kernel, nb=nb, has_prev=has_prev, sps=sps),
        grid=(nb // sps,),
        in_specs=in_specs, out_specs=out_specs, out_shape=out_shape,
        scratch_shapes=scratch,
        input_output_aliases=aliases,
        compiler_params=_params(("arbitrary",)),
        name="mixer_sample",
    )(*operands)


def _peer_select_kernel(x_ref, g_ref, wq_ref, sk_ref, h2_ref, rank_ref, p1_ref, cnt_ref, p0_ref,
                        sc_scr, *, tq, cw):
    nch = tq // cw
    hb = _rms(x_ref[...], g_ref[...]).astype(BF16)
    h2_ref[...] = hb
    q = jnp.dot(hb, wq_ref[...], preferred_element_type=F32).astype(BF16)
    for hc in range(2 * PEER_HEADS):
        for ch in range(nch):
            qs = q[ch * cw:(ch + 1) * cw, hc * LANES:(hc + 1) * LANES]
            sc_scr[hc, ch] = lax.dot_general(sk_ref[hc % 2], qs, NT_DIMS, preferred_element_type=F32)

    K = PEER_TOPK
    neg = -jnp.inf
    iota_k = lax.broadcasted_iota(jnp.int32, (K, cw), 0)
    iota_kf = iota_k.astype(F32)

    def unit(idx):
        h = idx // nch
        ch = idx % nch
        s0 = sc_scr[2 * h, ch]
        s1 = sc_scr[2 * h + 1, ch]

        sv0 = jnp.zeros((K, cw), F32)
        sv1 = jnp.zeros((K, cw), F32)
        work0, work1 = s0, s1
        for j in range(K):
            marker = -(RANK_MARK + j * RANK_MARK_STEP)
            mx0 = jnp.max(work0, axis=0, keepdims=True)
            work0 = jnp.where(work0 == mx0, marker, work0)
            sv0 = jnp.where(iota_k == j, mx0, sv0)
            yield
            mx1 = jnp.max(work1, axis=0, keepdims=True)
            work1 = jnp.where(work1 == mx1, marker, work1)
            sv1 = jnp.where(iota_k == j, mx1, sv1)
            yield
        rank1 = jnp.where(work1 <= -RANK_MARK, (-RANK_MARK - work1) * (1.0 / RANK_MARK_STEP), float(K))
        rank_ref[h, ch] = rank1.astype(BF16)
        p1_ref[h, ch] = jnp.exp(s1 - sv1[0:1, :]).astype(BF16)
        yield

        top = sv0[0:1, :] + sv1[0:1, :]
        front = sv0 + sv1[0:1, :]
        cnt = jnp.zeros((K, cw), F32)
        zsum = jnp.zeros((1, cw), F32)
        for _ in range(K):
            fm = jnp.max(front, axis=0, keepdims=True)
            j1 = jnp.min(jnp.where(front == fm, iota_kf, float(K)), axis=0, keepdims=True)
            sel = iota_kf == j1
            zsum = zsum + jnp.exp(fm - top)
            nxt = jnp.sum(jnp.where(sel, cnt, 0.0), axis=0, keepdims=True) + 1.0
            cnt = jnp.where(sel, cnt + 1.0, cnt)
            sv1n = jnp.sum(jnp.where(iota_kf == nxt, sv1, 0.0), axis=0, keepdims=True)
            sv1n = jnp.where(nxt >= float(K), neg, sv1n)
            sv0s = jnp.sum(jnp.where(sel, sv0, 0.0), axis=0, keepdims=True)
            front = jnp.where(sel, sv0s + sv1n, front)
            yield
        cntp = jnp.zeros((N_KEYS, cw), F32)
        for j in range(K):
            cntp = jnp.where(work0 == -(RANK_MARK + j * RANK_MARK_STEP), cnt[j:j + 1, :], cntp)
            if j % 4 == 3:
                yield
        p0 = jnp.exp(s0 - sv0[0:1, :]) * (GELU_IN_SCALE / zsum)
        for lt in range(cw // LANES):
            ls = slice(lt * LANES, (lt + 1) * LANES)
            cnt_ref[h, ch, lt] = cntp[:, ls]
            p0_ref[h, ch, lt] = p0[:, ls]

    def body(it, carry):
        units = [unit(2 * it), unit(2 * it + 1)]
        while units:
            for u in list(units):
                if next(u, "done") == "done":
                    units.remove(u)
        return carry

    lax.fori_loop(0, PEER_HEADS * nch // 2, body, 0)


def _peer_select(x1, g2, wq, sk, tq, cw):
    t = x1.shape[0]
    nch = tq // cw
    meta = lambda dt: jax.ShapeDtypeStruct((PEER_HEADS, t // cw, N_KEYS, cw), dt)
    mspec = pl.BlockSpec((PEER_HEADS, nch, N_KEYS, cw), lambda i: (0, i, 0, 0))
    rows = jax.ShapeDtypeStruct((PEER_HEADS, t // cw, cw // LANES, N_KEYS, LANES), F32)
    rspec = pl.BlockSpec((PEER_HEADS, nch, cw // LANES, N_KEYS, LANES), lambda i: (0, i, 0, 0, 0))
    return pl.pallas_call(
        functools.partial(_peer_select_kernel, tq=tq, cw=cw),
        grid=(t // tq,),
        in_specs=[
            pl.BlockSpec((tq, D_MODEL), lambda i: (i, 0)),
            pl.BlockSpec((1, D_MODEL), lambda i: (0, 0)),
            pl.BlockSpec((D_MODEL, PEER_HEADS * PEER_DQ), lambda i: (0, 0)),
            pl.BlockSpec((2, N_KEYS, PEER_DQ // 2), lambda i: (0, 0, 0)),
        ],
        out_specs=[pl.BlockSpec((tq, D_MODEL), lambda i: (i, 0)), mspec, mspec, rspec, rspec],
        out_shape=[jax.ShapeDtypeStruct((t, D_MODEL), BF16), meta(BF16), meta(BF16), rows, rows],
        scratch_shapes=[pltpu.VMEM((2 * PEER_HEADS, nch, N_KEYS, cw), F32)],
        compiler_params=_params(("arbitrary",)),
        name="peer_select",
    )(x1, g2, wq, sk)


def _peer_dense_kernel(x_ref, h2_ref, rank_ref, p1_ref, cnt_ref, p0_ref, u_ref, vt_ref, fg_ref,
                       out_ref, acc_scr, g_scr, ht_scr, *, tb, eb, ec, cw, nblk_e, final_norm):
    ngrp = eb // ec
    j = pl.program_id(1)

    @pl.when(j == 0)
    def _():
        acc_scr[...] = jnp.zeros_like(acc_scr)

    for grp in range(ngrp):
        es = slice(grp * ec, (grp + 1) * ec)
        u_grp = pltpu.bitcast(u_ref[grp * (ec // 2):(grp + 1) * (ec // 2), :], BF16)
        ht_scr[grp] = lax.dot_general(u_grp, h2_ref[...], NT_DIMS, preferred_element_type=F32)
    for grp in range(ngrp):
        es = slice(grp * ec, (grp + 1) * ec)
        for k in range(ec // N_KEYS):
            r = grp * (ec // N_KEYS) + k
            ks = slice(grp * ec + k * N_KEYS, grp * ec + (k + 1) * N_KEYS)
            for ch in range(tb // cw):
                cs = slice(ch * cw, (ch + 1) * cw)
                def row8(ref, h):
                    parts = [ref[h, ch, lt, pl.ds(r, 8, stride=0), :] for lt in range(cw // LANES)]
                    return jnp.concatenate(parts, axis=1).astype(BF16)[None]

                w = jnp.zeros((N_KEYS // 8, 8, cw), BF16)
                for h in range(PEER_HEADS):
                    rk = rank_ref[h, ch].reshape(N_KEYS // 8, 8, cw)
                    p1 = p1_ref[h, ch].reshape(N_KEYS // 8, 8, cw)
                    w = w + jnp.where(rk < row8(cnt_ref, h), p1 * row8(p0_ref, h), jnp.zeros_like(w))
                z = ht_scr[grp, k * N_KEYS:(k + 1) * N_KEYS, cs]
                g_scr[ks, cs] = w.reshape(N_KEYS, cw) * (z * (1.0 + lax.erf(z))).astype(BF16)
        acc_scr[...] += jnp.dot(pltpu.bitcast(vt_ref[:, es], BF16), g_scr[es, :],
                                preferred_element_type=F32)

    @pl.when(j == nblk_e - 1)
    def _():
        y = x_ref[...] + acc_scr[...].T
        if final_norm:
            y = _rms(y, fg_ref[...])
        out_ref[...] = y


def _peer_dense(x1, h2, rank, p1, cnt, p0, u_bf, vt_bf, fg, tb, eb, cw, final_norm):
    t = x1.shape[0]
    nch = tb // cw
    nblk_e = N_EXPERTS // eb
    ec = min(eb, 512)
    mspec = pl.BlockSpec((PEER_HEADS, nch, N_KEYS, cw), lambda i, j: (0, i, 0, 0))
    rspec = pl.BlockSpec((PEER_HEADS, nch, cw // LANES, eb // N_KEYS, LANES), lambda i, j: (0, i, 0, j, 0))
    return pl.pallas_call(
        functools.partial(_peer_dense_kernel, tb=tb, eb=eb, ec=ec, cw=cw, nblk_e=nblk_e,
                          final_norm=final_norm),
        grid=(t // tb, nblk_e),
        in_specs=[
            pl.BlockSpec((tb, D_MODEL), lambda i, j: (i, 0)),
            pl.BlockSpec((tb, D_MODEL), lambda i, j: (i, 0)),
            mspec, mspec, rspec, rspec,
            pl.BlockSpec((eb // 2, D_MODEL), lambda i, j: (j, 0)),
            pl.BlockSpec((D_MODEL // 2, eb), lambda i, j: (0, j)),
            pl.BlockSpec((1, D_MODEL), lambda i, j: (0, 0)),
        ],
        out_specs=pl.BlockSpec((tb, D_MODEL), lambda i, j: (i, 0)),
        out_shape=jax.ShapeDtypeStruct((t, D_MODEL), F32),
        scratch_shapes=[pltpu.VMEM((D_MODEL, tb), F32), pltpu.VMEM((eb, tb), BF16),
                        pltpu.VMEM((eb // ec, ec, tb), F32)],
        compiler_params=_params(("arbitrary", "arbitrary")),
        name="peer_dense",
    )(x1, h2, rank, p1, cnt, p0, u_bf, vt_bf, fg)


def _layer_weights(li, w_in, gla_w2, gla_b2, gla_norm_g, conv_w, conv_b, ml_i_b, ml_f_b, ml_norm_g,
                   w_out, peer_wq, peer_subkeys, peer_u, peer_v):
    w = w_in[li]
    o = 0
    segs = {}
    for name, width in (("aq", GLA_QK), ("ak", GLA_QK), ("av", D_MODEL), ("ag", D_MODEL),
                        ("alr", GLA_GATE_RANK), ("bq", ML_W), ("bk", ML_W), ("bv", ML_W),
                        ("bi", ML_HEADS), ("bf", ML_HEADS), ("bo", ML_W),
                        ("ga", D_MODEL), ("gb", D_MODEL)):
        segs[name] = w[:, o:o + width]
        o += width
    seg_w = WSEG_BLOCKS * D_MODEL
    starts = (0, seg_w + GLA_GATE_RANK, 2 * seg_w + GLA_GATE_RANK + 2 * ML_HEADS)
    w_segs = tuple(w[:, s:s + seg_w].astype(BF16) for s in starts)
    pad = lambda a: jnp.pad(a, ((0, 0), (0, LANES - a.shape[1])))
    w_small = jnp.concatenate([pad(segs["alr"]), pad(segs["bi"]), pad(segs["bf"])], axis=1).astype(BF16)
    w_gt = jnp.concatenate([segs["bi"], segs["bf"]], axis=1).T.astype(BF16)
    gate_b = jnp.concatenate([ml_i_b[li], ml_f_b[li]])
    u_packed, vt_packed = _table_prep(peer_u, peer_v, li)
    return dict(
        w_segs=w_segs, w_small=w_small, w_gt=w_gt,
        w2p=jnp.pad(gla_w2[li], ((0, LANES - GLA_GATE_RANK), (0, 0))),
        b2=gla_b2[li][None, :], gng=gla_norm_g[li][None, :],
        cw=conv_w[li], cb=conv_b[li][None, :],
        gbr=jnp.concatenate([jnp.pad(ml_i_b[li], (0, LANES - ML_HEADS)),
                             jnp.pad(ml_f_b[li], (0, LANES - ML_HEADS))])[None, :],
        gbc=gate_b[:, None], mng=ml_norm_g[li][None, :],
        wout=w_out[li].astype(BF16),
        wq=peer_wq[li].astype(BF16), sk=peer_subkeys[li].astype(BF16),
        u=u_packed, vt=vt_packed,
    )


def _table_prep_kernel(u_ref, v_ref, up_ref, vtp_ref):
    up_ref[...] = pltpu.bitcast((u_ref[0] * GELU_IN_SCALE).astype(BF16), jnp.uint32)
    vtp_ref[...] = pltpu.bitcast(v_ref[0].T.astype(BF16), jnp.uint32)


def _table_prep(u_tabs, v_tabs, li, eb=1024):
    _, ne, d = u_tabs.shape
    return pl.pallas_call(
        _table_prep_kernel,
        grid=(ne // eb,),
        in_specs=[pl.BlockSpec((1, eb, d), lambda i: (li, i, 0)), pl.BlockSpec((1, eb, d), lambda i: (li, i, 0))],
        out_specs=[pl.BlockSpec((eb // 2, d), lambda i: (i, 0)), pl.BlockSpec((d // 2, eb), lambda i: (0, i))],
        out_shape=[jax.ShapeDtypeStruct((ne // 2, d), jnp.uint32),
                   jax.ShapeDtypeStruct((d // 2, ne), jnp.uint32)],
        compiler_params=_params(("arbitrary",)),
        name="table_prep",
    )(u_tabs, v_tabs)


def _pick(n, cands):
    for c in cands:
        if n % c == 0:
            return c
    return n


def _peer(x1, g2, lw, fg, final_norm):
    t = x1.shape[0]
    tq = _pick(t, (512, 256, 128))
    tb = _pick(t, (512, 256, 128))
    cw = _pick(tb, (2 * LANES, LANES))
    h2, rank, p1, cnt, p0 = _peer_select(x1, g2, lw["wq"], lw["sk"], tq, cw)
    return _peer_dense(x1, h2, rank, p1, cnt, p0, lw["u"], lw["vt"], fg, tb, 2048, cw, final_norm)


def kernel(x_prompt, x_sample, state_gla, state_mlstm_c, state_mlstm_n, state_mlstm_m, state_conv,
           norm1_g, w_in, gla_w2, gla_b2, gla_norm_g, conv_w, conv_b, ml_i_b, ml_f_b, ml_norm_g,
           w_out, norm2_g, peer_wq, peer_subkeys, peer_u, peer_v, final_g):
    depth = w_in.shape[0]
    bsz, seq, _ = x_prompt.shape
    nb = x_sample.shape[0]
    assert seq % CHUNK == 0 and x_sample.shape[1] == 1 and nb % LANES == 0
    tp = bsz * seq
    tm = _pick(tp, (2048, 1024, 512, 256, 128))
    tc = _pick(seq, (256, 128, 64))
    npar = _pick(bsz, (2, 1))
    fg = final_g[None, :]

    xp = x_prompt.reshape(tp, D_MODEL)
    xs = x_sample.reshape(nb, D_MODEL)
    p_out = [[] for _ in range(5)]
    s_out = [[] for _ in range(5)]
    s_gla = s_c = None
    for li in range(depth):
        lw = _layer_weights(li, w_in, gla_w2, gla_b2, gla_norm_g, conv_w, conv_b, ml_i_b, ml_f_b,
                            ml_norm_g, w_out, peer_wq, peer_subkeys, peer_u, peer_v)
        g1 = norm1_g[li][None, :]
        g2 = norm2_g[li][None, :]
        last = li == depth - 1

        big, sm, gt = _inproj(xp, g1, lw["w_segs"], lw["w_small"], lw["w_gt"], tm)
        gtc = gt.reshape(8, bsz, seq // CHUNK, CHUNK).transpose(1, 2, 0, 3)
        x1, gla, c, n, m, conv = _mixer_prompt(
            xp.reshape(bsz, seq, D_MODEL), big.reshape(bsz, seq, BIG_COLS),
            sm.reshape(bsz, seq, SMALL_COLS), gtc, lw, bsz, seq, tc, npar)
        xp = _peer(x1.reshape(tp, D_MODEL), g2, lw, fg, last)
        for lst, val in zip(p_out, (gla, c, n, m[:, 0, :ML_HEADS], conv)):
            lst.append(val)

        big, sm, _ = _inproj(xs, g1, lw["w_segs"], lw["w_small"], lw["w_gt"], nb)
        x1, s_gla, s_c, n, m, conv = _mixer_sample(
            xs, big, sm, state_gla, state_mlstm_c,
            state_mlstm_n[li].reshape(nb, ML_W),
            jnp.pad(state_mlstm_m[li], ((0, 0), (0, LANES - ML_HEADS))),
            state_conv[li].reshape(nb, (CONV_W - 1) * 2 * ML_W), lw, li, s_gla, s_c)
        xs = _peer(x1, g2, lw, fg, last)
        for lst, val in zip(s_out[2:], (n.reshape(nb, ML_HEADS, ML_DH), m[:, :ML_HEADS],
                                        conv.reshape(nb, CONV_W - 1, 2 * ML_W))):
            lst.append(val)

    y_prompt = xp.reshape(bsz, seq, D_MODEL)
    y_sample = xs.reshape(nb, 1, D_MODEL)
    return (y_prompt, y_sample, *[jnp.stack(v) for v in p_out],
            s_gla, s_c, *[jnp.stack(v) for v in s_out[2:]])
```

```python
import functools

import jax
import jax.numpy as jnp
from jax import lax
from jax.experimental import pallas as pl
from jax.experimental.pallas import tpu as pltpu

F32 = jnp.float32
BF16 = jnp.bfloat16
HIGHEST = lax.Precision.HIGHEST

D_MODEL = 1024
GLA_HEADS = 4
GLA_DK = 128
GLA_DV = 256
GLA_GATE_RANK = 16
GLA_TAU = 16.0
GLA_QK = GLA_HEADS * GLA_DK
ML_HEADS = 4
ML_DH = 256
ML_W = ML_HEADS * ML_DH
CONV_W = 4
CHUNK = 64
PEER_HEADS = 8
PEER_DQ = 256
N_KEYS = 128
N_EXPERTS = N_KEYS * N_KEYS
PEER_TOPK = 16
EPS = 1e-6
GELU_IN_SCALE = 0.7071067811865476
RANK_MARK = 2.0 ** 100
RANK_MARK_STEP = 2.0 ** 96

LANES = 128
BIG_COLS = 9 * D_MODEL
WSEG_BLOCKS = 3
SAMPLES_PER_STEP = 2
HEADS_PER_ROUND = 4
SMALL_COLS = 3 * LANES
VMEM_LIMIT_BYTES = 56 * 1024 * 1024

NT_DIMS = (((1,), (1,)), ((), ()))
TN_DIMS = (((0,), (0,)), ((), ()))


def _params(sem, flags=None):
    return pltpu.CompilerParams(dimension_semantics=sem, vmem_limit_bytes=VMEM_LIMIT_BYTES, flags=flags)


def _log_sigmoid(x):
    return jnp.minimum(x, 0.0) - jnp.log1p(jnp.exp(-jnp.abs(x)))


def _sigmoid(x):
    return 1.0 / (1.0 + jnp.exp(-x))


def _silu(x):
    return x * _sigmoid(x)


def _rms(x, g):
    return x * lax.rsqrt(jnp.mean(x * x, axis=-1, keepdims=True) + EPS) * g


def _lane_col(a, j):
    lane = lax.broadcasted_iota(jnp.int32, a.shape, 1)
    return jnp.sum(jnp.where(lane == j, a, 0.0), axis=1, keepdims=True)


def _inproj_kernel(x_ref, g_ref, wa_ref, wb_ref, wc_ref, wsm_ref, wgt_ref, big_ref, sm_ref, gt_ref, h_scr):
    j = pl.program_id(1)

    @pl.when(j == 0)
    def _():
        hb = _rms(x_ref[...], g_ref[...]).astype(BF16)
        h_scr[...] = hb
        sm_ref[...] = jnp.dot(hb, wsm_ref[...], preferred_element_type=F32)
        gt_ref[...] = lax.dot_general(wgt_ref[...], hb, NT_DIMS, preferred_element_type=F32)

    for s, w_ref in enumerate((wa_ref, wb_ref, wc_ref)):
        @pl.when((j >= s * WSEG_BLOCKS) & (j < (s + 1) * WSEG_BLOCKS))
        def _(w_ref=w_ref):
            big_ref[...] = jnp.dot(h_scr[...], w_ref[...], preferred_element_type=F32).astype(BF16)


def _inproj(x, g, w_segs, w_small, w_gt, tm):
    t = x.shape[0]
    tn = D_MODEL

    def wspec(s):
        return pl.BlockSpec((D_MODEL, tn), lambda i, j: (0, jnp.clip(j - s * WSEG_BLOCKS, 0, WSEG_BLOCKS - 1)))

    return pl.pallas_call(
        _inproj_kernel,
        grid=(t // tm, BIG_COLS // tn),
        in_specs=[
            pl.BlockSpec((tm, D_MODEL), lambda i, j: (i, 0)),
            pl.BlockSpec((1, D_MODEL), lambda i, j: (0, 0)),
            wspec(0), wspec(1), wspec(2),
            pl.BlockSpec((D_MODEL, SMALL_COLS), lambda i, j: (0, 0)),
            pl.BlockSpec((8, D_MODEL), lambda i, j: (0, 0)),
        ],
        out_specs=[
            pl.BlockSpec((tm, tn), lambda i, j: (i, j)),
            pl.BlockSpec((tm, SMALL_COLS), lambda i, j: (i, 0)),
            pl.BlockSpec((8, tm), lambda i, j: (0, i)),
        ],
        out_shape=[
            jax.ShapeDtypeStruct((t, BIG_COLS), BF16),
            jax.ShapeDtypeStruct((t, SMALL_COLS), F32),
            jax.ShapeDtypeStruct((8, t), F32),
        ],
        scratch_shapes=[pltpu.VMEM((tm, D_MODEL), BF16)],
        compiler_params=_params(("arbitrary", "arbitrary")),
        name="inproj",
    )(x, g, *w_segs, w_small, w_gt)


def _headnorm(o, g_row):
    return o * lax.rsqrt(jnp.mean(o * o, axis=-1, keepdims=True) + EPS) * g_row


def _mixer_prompt_kernel(x_ref, aq_ref, ak_ref, av_ref, ag_ref, bq_ref, bk_ref, bv_ref, bo_ref,
                         ga_ref, gb_ref, sm_ref, gt_ref, w2_ref, b2_ref, gng_ref, cw_ref, cb_ref,
                         gbr_ref, gbc_ref, mng_ref, wout_ref,
                         x1_ref, gla_ref, c_ref, n_ref, m_ref, conv_ref,
                         st_scr, xp_scr, qk_scr, mix_scr, *, tc, nblk, npar):
    i = pl.program_id(1)
    L = CHUNK

    @pl.when(i == 0)
    def _init():
        st_scr[...] = jnp.zeros_like(st_scr)
        c_ref[...] = jnp.zeros_like(c_ref)
        n_ref[...] = jnp.zeros_like(n_ref)
        m_ref[...] = jnp.zeros_like(m_ref)
        xp_scr[:, 0:8, :] = jnp.zeros((npar, 8, 2 * ML_W), F32)

    for p in range(npar):
        xp_scr[p, 8:8 + tc, 0:ML_W] = bq_ref[p].astype(F32)
        xp_scr[p, 8:8 + tc, ML_W:2 * ML_W] = bk_ref[p].astype(F32)
        y = cb_ref[...]
        for j in range(CONV_W):
            y = y + cw_ref[j:j + 1, :] * xp_scr[p, 5 + j:5 + j + tc, :]
        qk = _silu(y)
        qk_scr[p, :, 0:ML_W] = qk[:, 0:ML_W].astype(BF16)
        qk_scr[p, :, ML_W:2 * ML_W] = (qk[:, ML_W:2 * ML_W] * (ML_DH ** -0.5)).astype(BF16)
        xp_scr[p, 5:8, :] = xp_scr[p, tc + 5:tc + 8, :]

    @pl.when(i == nblk - 1)
    def _():
        conv_ref[...] = xp_scr[:, 5:8, :]

    row = lax.broadcasted_iota(jnp.int32, (L, L), 0)
    col = lax.broadcasted_iota(jnp.int32, (L, L), 1)
    causal = col <= row
    tril = causal.astype(F32)
    triu = (row <= col).astype(F32)
    lane_row = lax.broadcasted_iota(jnp.int32, (1, LANES), 1)

    def gates(p, c, rows):
        sm = sm_ref[p, rows, :]
        z = jnp.dot(sm[:, 0:LANES], w2_ref[...], precision=HIGHEST, preferred_element_type=F32) + b2_ref[...]
        log_a = _log_sigmoid(z) * (1.0 / GLA_TAU)
        bc = jnp.dot(tril, log_a, precision=HIGHEST, preferred_element_type=F32)
        li_c_all = sm[:, LANES:2 * LANES] + gbr_ref[:, 0:LANES]
        lf_c_all = _log_sigmoid(sm[:, 2 * LANES:3 * LANES] + gbr_ref[:, LANES:2 * LANES])
        bcum_c_all = jnp.dot(tril, lf_c_all, precision=HIGHEST, preferred_element_type=F32)
        gtb = gt_ref[p, c] + gbc_ref[...]
        bcum_r_all = jnp.dot(_log_sigmoid(gtb), triu, precision=HIGHEST, preferred_element_type=F32)
        return bc, li_c_all, bcum_c_all, gtb, bcum_r_all

    def head(p, h, rows, shared):
        bc, li_c_all, bcum_c_all, gtb, bcum_r_all = shared
        sl = slice(h * GLA_DK, (h + 1) * GLA_DK)
        vl = slice(h * GLA_DV, (h + 1) * GLA_DV)
        b = bc[:, sl]
        mid = b[L // 2 - 1:L // 2, :]
        bl = b[L - 1:L, :]
        q = aq_ref[p, rows, sl].astype(F32) * (GLA_DK ** -0.5)
        k = ak_ref[p, rows, sl].astype(F32)
        v = av_ref[p, rows, vl]
        q_in = (q * jnp.exp(b)).astype(BF16)
        q_at = (q * jnp.exp(b - mid)).astype(BF16)
        yield
        k_at = (k * jnp.exp(mid - b)).astype(BF16)
        k_out = (k * jnp.exp(bl - b)).astype(BF16)
        yield
        att = lax.dot_general(q_at, k_at, NT_DIMS, preferred_element_type=F32)
        att = jnp.where(causal, att, 0.0)
        st = st_scr[p, h]
        yield
        o = lax.dot_general(q_in, st.astype(BF16), NT_DIMS, preferred_element_type=F32)
        yield
        o = o + jnp.dot(att.astype(BF16), v, preferred_element_type=F32)
        yield
        st_scr[p, h] = st * jnp.exp(bl) + lax.dot_general(v, k_out, TN_DIMS, preferred_element_type=F32)
        yield
        o_a = _headnorm(o, gng_ref[:, vl]) * _silu(ag_ref[p, rows, vl].astype(F32))
        yield

        li_c = _lane_col(li_c_all, h)
        b_c = _lane_col(bcum_c_all, h)
        li_r = gtb[h:h + 1, :]
        b_r = bcum_r_all[ML_HEADS + h:ML_HEADS + h + 1, :]
        m_prev = jnp.sum(jnp.where(lane_row == h, m_ref[p], 0.0), axis=1, keepdims=True)
        a_int = b_c + m_prev
        dm = jnp.where(causal, b_c - b_r + li_r, -jnp.inf)
        m_t = jnp.maximum(a_int, jnp.max(dm, axis=1, keepdims=True))
        yield
        w_int = jnp.exp(a_int - m_t)
        dexp = jnp.exp(dm - m_t)
        qb = qk_scr[p, rows, vl]
        kb = qk_scr[p, rows, ML_W + h * ML_DH:ML_W + (h + 1) * ML_DH]
        vb = bv_ref[p, rows, vl]
        s = lax.dot_general(qb, kb, NT_DIMS, preferred_element_type=F32) * dexp
        yield
        cst = c_ref[p, h]
        num = w_int * jnp.dot(qb, cst.astype(BF16), preferred_element_type=F32)
        yield
        num = num + jnp.dot(s.astype(BF16), vb, preferred_element_type=F32)
        nrow = n_ref[p, h:h + 1, :]
        den = (w_int * jnp.sum(qb.astype(F32) * nrow, axis=1, keepdims=True)
               + jnp.sum(s, axis=1, keepdims=True))
        yield
        hh = num / jnp.maximum(jnp.abs(den), jnp.exp(-m_t))
        b_last = b_c[L - 1:L, :]
        g_c = b_last - b_c + li_c
        m_new = jnp.maximum(b_last + m_prev, jnp.max(g_c, axis=0, keepdims=True))
        f_c = jnp.exp(b_last + m_prev - m_new)
        w_s = jnp.exp(g_c - m_new)
        kw = kb.astype(F32) * w_s
        yield
        c_ref[p, h] = f_c * cst + lax.dot_general(kw.astype(BF16), vb, TN_DIMS, preferred_element_type=F32)
        n_ref[p, h:h + 1, :] = f_c * nrow + jnp.sum(kw, axis=0, keepdims=True)
        m_ref[p] = jnp.where(lane_row == h, m_new, m_ref[p])
        yield
        o_b = _headnorm(hh, mng_ref[:, vl]) * _sigmoid(bo_ref[p, rows, vl].astype(F32))
        yield
        mix = (_sigmoid(ga_ref[p, rows, vl].astype(F32)) * o_a
               + _sigmoid(gb_ref[p, rows, vl].astype(F32)) * o_b)
        mix_scr[p, rows, vl] = mix.astype(BF16)

    def chunk(c, carry):
        rows = pl.ds(pl.multiple_of(c * L, L), L)
        shared = [gates(p, c, rows) for p in range(npar)]
        for h0 in range(0, GLA_HEADS, HEADS_PER_ROUND):
            units = [head(p, h, rows, shared[p]) for h in range(h0, h0 + HEADS_PER_ROUND) for p in range(npar)]
            while units:
                for u in list(units):
                    if next(u, "done") == "done":
                        units.remove(u)
        return carry

    lax.fori_loop(0, tc // L, chunk, 0)

    for p in range(npar):
        x1_ref[p] = x_ref[p] + jnp.dot(mix_scr[p], wout_ref[...], preferred_element_type=F32)

    @pl.when(i == nblk - 1)
    def _fin():
        for p in range(npar):
            for h in range(GLA_HEADS):
                gla_ref[p, h] = st_scr[p, h].T


def _mixer_prompt(x, big, sm, gtc, lw, bsz, seq, tc, npar):
    nblk = seq // tc

    def seg(width, idx):
        return pl.BlockSpec((npar, tc, width), lambda b, i: (b, i, idx))

    const = lambda shape: pl.BlockSpec(shape, lambda b, i: tuple(0 for _ in shape))
    in_specs = [
        seg(D_MODEL, 0),
        seg(GLA_QK, 0), seg(GLA_QK, 1),
        seg(D_MODEL, 1), seg(D_MODEL, 2),
        seg(D_MODEL, 3), seg(D_MODEL, 4), seg(D_MODEL, 5),
        seg(D_MODEL, 6), seg(D_MODEL, 7), seg(D_MODEL, 8),
        seg(SMALL_COLS, 0),
        pl.BlockSpec((npar, tc // CHUNK, 8, CHUNK), lambda b, i: (b, i, 0, 0)),
        const((LANES, GLA_QK)), const((1, GLA_QK)), const((1, D_MODEL)),
        const((CONV_W, 2 * ML_W)), const((1, 2 * ML_W)),
        const((1, 2 * LANES)), const((8, 1)), const((1, D_MODEL)),
        const((D_MODEL, D_MODEL)),
    ]
    out_specs = [
        seg(D_MODEL, 0),
        pl.BlockSpec((npar, GLA_HEADS, GLA_DK, GLA_DV), lambda b, i: (b, 0, 0, 0)),
        pl.BlockSpec((npar, ML_HEADS, ML_DH, ML_DH), lambda b, i: (b, 0, 0, 0)),
        pl.BlockSpec((npar, ML_HEADS, ML_DH), lambda b, i: (b, 0, 0)),
        pl.BlockSpec((npar, 1, LANES), lambda b, i: (b, 0, 0)),
        pl.BlockSpec((npar, CONV_W - 1, 2 * ML_W), lambda b, i: (b, 0, 0)),
    ]
    out_shape = [
        jax.ShapeDtypeStruct((bsz, seq, D_MODEL), F32),
        jax.ShapeDtypeStruct((bsz, GLA_HEADS, GLA_DK, GLA_DV), F32),
        jax.ShapeDtypeStruct((bsz, ML_HEADS, ML_DH, ML_DH), F32),
        jax.ShapeDtypeStruct((bsz, ML_HEADS, ML_DH), F32),
        jax.ShapeDtypeStruct((bsz, 1, LANES), F32),
        jax.ShapeDtypeStruct((bsz, CONV_W - 1, 2 * ML_W), F32),
    ]
    scratch = [
        pltpu.VMEM((npar, GLA_HEADS, GLA_DV, GLA_DK), F32),
        pltpu.VMEM((npar, tc + 8, 2 * ML_W), F32),
        pltpu.VMEM((npar, tc, 2 * ML_W), BF16),
        pltpu.VMEM((npar, tc, D_MODEL), BF16),
    ]
    return pl.pallas_call(
        functools.partial(_mixer_prompt_kernel, tc=tc, nblk=nblk, npar=npar),
        grid=(bsz // npar, nblk),
        in_specs=in_specs, out_specs=out_specs, out_shape=out_shape,
        scratch_shapes=scratch,
        compiler_params=_params(("arbitrary", "arbitrary")),
        name="mixer_prompt",
    )(x, big, big, big, big, big, big, big, big, big, big, sm, gtc,
      lw["w2p"], lw["b2"], lw["gng"], lw["cw"], lw["cb"], lw["gbr"], lw["gbc"], lw["mng"], lw["wout"])


def _col_bcast(row):
    return jnp.broadcast_to(row, (LANES, LANES)).T


def _mixer_sample_kernel(*refs, nb, has_prev, sps):
    n_in = 18 if has_prev else 16
    (x_ref, big_ref, sm_ref, gla_in, c_in, n_ref, m_ref, conv_ref,
     w2_ref, b2_ref, gng_ref, cw_ref, cb_ref, gbr_ref, mng_ref, wout_ref) = refs[:16]
    x1_ref, gla_out, c_out, n_o, m_o, conv_o = refs[n_in:n_in + 6]
    a_s, qa_s, ka_s, va_s, kw_s, qb_s, vb_s, fc_s, den_s, oa_s, num_s = refs[n_in + 6:]
    gla_ref, c_ref, gla_o, c_o = gla_in.at[0], c_in.at[0], gla_out.at[0], c_out.at[0]
    b = pl.program_id(0)
    seg = lambda idx: slice(idx * D_MODEL, (idx + 1) * D_MODEL)
    lane = lax.broadcasted_iota(jnp.int32, (nb, LANES), 1)

    @pl.when(b == 0)
    def _prep():
        sm = sm_ref[...]
        z = jnp.dot(sm[:, 0:LANES], w2_ref[...], precision=HIGHEST, preferred_element_type=F32) + b2_ref[...]
        a_s[...] = jnp.exp(_log_sigmoid(z) * (1.0 / GLA_TAU))
        qa_s[...] = big_ref[:, 0:GLA_QK].astype(F32) * (GLA_DK ** -0.5)
        ka_s[...] = big_ref[:, GLA_QK:2 * GLA_QK].astype(F32)
        va_s[...] = big_ref[:, seg(1)].astype(F32)
        vb_s[...] = big_ref[:, seg(5)].astype(F32)
        xq = big_ref[:, seg(3)].astype(F32)
        xk = big_ref[:, seg(4)].astype(F32)
        w = 2 * ML_W
        y = cb_ref[...]
        for j in range(CONV_W - 1):
            y = y + cw_ref[j:j + 1, :] * conv_ref[:, j * w:(j + 1) * w]
        y = y + cw_ref[CONV_W - 1:CONV_W, :] * jnp.concatenate([xq, xk], axis=1)
        qk = _silu(y)
        qb = qk[:, 0:ML_W]
        kb = qk[:, ML_W:w] * (ML_DH ** -0.5)
        qb_s[...] = qb
        conv_o[:, 0:2 * w] = conv_ref[:, w:3 * w]
        conv_o[:, 2 * w:2 * w + ML_W] = xq
        conv_o[:, 2 * w + ML_W:3 * w] = xk

        li = sm[:, LANES:2 * LANES] + gbr_ref[:, 0:LANES]
        lf = _log_sigmoid(sm[:, 2 * LANES:3 * LANES] + gbr_ref[:, LANES:2 * LANES])
        m_old = m_ref[...]
        m_new = jnp.maximum(lf + m_old, li)
        f_c = jnp.exp(lf + m_old - m_new)
        w_s = jnp.exp(li - m_new)
        m_o[...] = m_new
        fc_s[...] = f_c
        den = jnp.zeros((nb, LANES), F32)
        for h in range(ML_HEADS):
            vl = slice(h * ML_DH, (h + 1) * ML_DH)
            kw = kb[:, vl] * _lane_col(w_s, h)
            kw_s[:, vl] = kw
            n_new = _lane_col(f_c, h) * n_ref[:, vl] + kw
            n_o[:, vl] = n_new
            den = jnp.where(lane == h, jnp.sum(qb[:, vl] * n_new, axis=1, keepdims=True), den)
        den_s[...] = den
        oa_s[...] = jnp.zeros_like(oa_s)
        num_s[...] = jnp.zeros_like(num_s)

    tile = pl.ds(pl.multiple_of(lax.shift_right_logical(b * sps, 3) * 8, 8), 8)
    lane_row = lax.broadcasted_iota(jnp.int32, (1, LANES), 1)

    for s in range(sps):
        sub = jnp.bitwise_and(b * sps + s, 7)

        def row_get(ref, cols, sub=sub):
            t = ref[tile, cols]
            pick = lax.broadcasted_iota(jnp.int32, t.shape, 0) == sub
            return jnp.sum(jnp.where(pick, t, 0.0), axis=0, keepdims=True)

        def row_set(ref, cols, val, sub=sub):
            t = ref[tile, cols]
            pick = lax.broadcasted_iota(jnp.int32, t.shape, 0) == sub
            ref[tile, cols] = jnp.where(pick, val, t)

        fc_row = row_get(fc_s, slice(0, LANES))
        for h in range(GLA_HEADS):
            sl = slice(h * GLA_DK, (h + 1) * GLA_DK)
            vl = slice(h * GLA_DV, (h + 1) * GLA_DV)
            a_c = _col_bcast(row_get(a_s, sl))
            k_c = _col_bcast(row_get(ka_s, sl))
            q_c = _col_bcast(row_get(qa_s, sl))
            v_row = row_get(va_s, vl)
            s_old = gla_ref[s, h]
            halves = []
            for p in range(GLA_DV // LANES):
                ls = slice(p * LANES, (p + 1) * LANES)
                s_new = s_old[:, ls] * a_c + k_c * v_row[:, ls]
                gla_o[s, h, :, ls] = s_new
                halves.append(jnp.sum(q_c * s_new, axis=0, keepdims=True))
            row_set(oa_s, vl, jnp.concatenate(halves, axis=1))

            f_c = jnp.sum(jnp.where(lane_row == h, fc_row, 0.0), axis=1, keepdims=True)
            vb_row = row_get(vb_s, vl)
            acc = [jnp.zeros((1, LANES), F32) for _ in range(ML_DH // LANES)]
            for r in range(ML_DH // LANES):
                rs = slice(r * LANES, (r + 1) * LANES)
                ks = slice(h * ML_DH + r * LANES, h * ML_DH + (r + 1) * LANES)
                kw_c = _col_bcast(row_get(kw_s, ks))
                qb_c = _col_bcast(row_get(qb_s, ks))
                for p in range(ML_DH // LANES):
                    ls = slice(p * LANES, (p + 1) * LANES)
                    c_new = f_c * c_ref[s, h, rs, ls] + kw_c * vb_row[:, ls]
                    c_o[s, h, rs, ls] = c_new
                    acc[p] = acc[p] + jnp.sum(qb_c * c_new, axis=0, keepdims=True)
            row_set(num_s, vl, jnp.concatenate(acc, axis=1))

    @pl.when(b == nb // sps - 1)
    def _post():
        m_new = m_o[...]
        den = den_s[...]
        for h in range(GLA_HEADS):
            vl = slice(h * GLA_DV, (h + 1) * GLA_DV)
            o_a = _headnorm(oa_s[:, vl], gng_ref[:, vl]) * _silu(big_ref[:, D_MODEL * 2 + h * GLA_DV:
                                                                       D_MODEL * 2 + (h + 1) * GLA_DV].astype(F32))
            dn = jnp.maximum(jnp.abs(_lane_col(den, h)), jnp.exp(-_lane_col(m_new, h)))
            hh = num_s[:, vl] / dn
            o_b = _headnorm(hh, mng_ref[:, vl]) * _sigmoid(
                big_ref[:, 6 * D_MODEL + h * ML_DH:6 * D_MODEL + (h + 1) * ML_DH].astype(F32))
            ga = big_ref[:, 7 * D_MODEL + h * ML_DH:7 * D_MODEL + (h + 1) * ML_DH].astype(F32)
            gb = big_ref[:, 8 * D_MODEL + h * ML_DH:8 * D_MODEL + (h + 1) * ML_DH].astype(F32)
            oa_s[:, vl] = _sigmoid(ga) * o_a + _sigmoid(gb) * o_b
        x1_ref[...] = x_ref[...] + jnp.dot(oa_s[...].astype(BF16), wout_ref[...], preferred_element_type=F32)


def _mixer_sample(x, big, sm, st_gla, st_c, st_n, st_m, st_conv, lw, li, prev_gla, prev_c):
    nb = x.shape[0]
    has_prev = prev_gla is not None
    full = lambda shape: pl.BlockSpec(shape, lambda b: tuple(0 for _ in shape))
    sps = SAMPLES_PER_STEP
    gla_spec = pl.BlockSpec((1, sps, GLA_HEADS, GLA_DK, GLA_DV), lambda b: (li, b, 0, 0, 0))
    c_spec = pl.BlockSpec((1, sps, ML_HEADS, ML_DH, ML_DH), lambda b: (li, b, 0, 0, 0))
    in_specs = [
        full((nb, D_MODEL)), full((nb, BIG_COLS)), full((nb, SMALL_COLS)),
        gla_spec, c_spec,
        full((nb, ML_W)), full((nb, LANES)), full((nb, 3 * 2 * ML_W)),
        full((LANES, GLA_QK)), full((1, GLA_QK)), full((1, D_MODEL)),
        full((CONV_W, 2 * ML_W)), full((1, 2 * ML_W)), full((1, 2 * LANES)), full((1, D_MODEL)),
        full((D_MODEL, D_MODEL)),
    ]
    operands = [x, big, sm, st_gla, st_c, st_n, st_m, st_conv,
                lw["w2p"], lw["b2"], lw["gng"], lw["cw"], lw["cb"], lw["gbr"], lw["mng"], lw["wout"]]
    aliases = {}
    if has_prev:
        in_specs += [pl.BlockSpec(memory_space=pl.ANY), pl.BlockSpec(memory_space=pl.ANY)]
        operands += [prev_gla, prev_c]
        aliases = {16: 1, 17: 2}
    out_specs = [
        full((nb, D_MODEL)),
        gla_spec, c_spec,
        full((nb, ML_W)), full((nb, LANES)), full((nb, 3 * 2 * ML_W)),
    ]
    out_shape = [
        jax.ShapeDtypeStruct((nb, D_MODEL), F32),
        jax.ShapeDtypeStruct(st_gla.shape, F32),
        jax.ShapeDtypeStruct(st_c.shape, F32),
        jax.ShapeDtypeStruct((nb, ML_W), F32),
        jax.ShapeDtypeStruct((nb, LANES), F32),
        jax.ShapeDtypeStruct((nb, 3 * 2 * ML_W), F32),
    ]
    vm = lambda cols: pltpu.VMEM((nb, cols), F32)
    scratch = [vm(GLA_QK), vm(GLA_QK), vm(GLA_QK), vm(D_MODEL), vm(ML_W), vm(ML_W), vm(ML_W),
               vm(LANES), vm(LANES), vm(D_MODEL), vm(ML_W)]
    return pl.pallas_call(
        functools.partial(_mixer_sample_kernel, nb=nb, has_prev=has_prev, sps=sps),
        grid=(nb // sps,),
        in_specs=in_specs, out_specs=out_specs, out_shape=out_shape,
        scratch_shapes=scratch,
        input_output_aliases=aliases,
        compiler_params=_params(("arbitrary",)),
        name="mixer_sample",
    )(*operands)


def _peer_select_kernel(x_ref, g_ref, wq_ref, sk_ref, h2_ref, rank_ref, p1_ref, cnt_ref, p0_ref,
                        sc_scr, *, tq, cw):
    nch = tq // cw
    hb = _rms(x_ref[...], g_ref[...]).astype(BF16)
    h2_ref[...] = hb
    q = jnp.dot(hb, wq_ref[...], preferred_element_type=F32).astype(BF16)
    for hc in range(2 * PEER_HEADS):
        for ch in range(nch):
            qs = q[ch * cw:(ch + 1) * cw, hc * LANES:(hc + 1) * LANES]
            sc_scr[hc, ch] = lax.dot_general(sk_ref[hc % 2], qs, NT_DIMS, preferred_element_type=F32)

    K = PEER_TOPK
    neg = -jnp.inf
    iota_k = lax.broadcasted_iota(jnp.int32, (K, cw), 0)
    iota_kf = iota_k.astype(F32)

    def unit(idx):
        h = idx // nch
        ch = idx % nch
        s0 = sc_scr[2 * h, ch]
        s1 = sc_scr[2 * h + 1, ch]

        sv0 = jnp.zeros((K, cw), F32)
        sv1 = jnp.zeros((K, cw), F32)
        work0, work1 = s0, s1
        for j in range(K):
            marker = -(RANK_MARK + j * RANK_MARK_STEP)
            mx0 = jnp.max(work0, axis=0, keepdims=True)
            work0 = jnp.where(work0 == mx0, marker, work0)
            sv0 = jnp.where(iota_k == j, mx0, sv0)
            yield
            mx1 = jnp.max(work1, axis=0, keepdims=True)
            work1 = jnp.where(work1 == mx1, marker, work1)
            sv1 = jnp.where(iota_k == j, mx1, sv1)
            yield
        rank1 = jnp.where(work1 <= -RANK_MARK, (-RANK_MARK - work1) * (1.0 / RANK_MARK_STEP), float(K))
        rank_ref[h, ch] = rank1.astype(BF16)
        p1_ref[h, ch] = jnp.exp(s1 - sv1[0:1, :]).astype(BF16)
        yield

        top = sv0[0:1, :] + sv1[0:1, :]
        front = sv0 + sv1[0:1, :]
        cnt = jnp.zeros((K, cw), F32)
        zsum = jnp.zeros((1, cw), F32)
        for _ in range(K):
            fm = jnp.max(front, axis=0, keepdims=True)
            j1 = jnp.min(jnp.where(front == fm, iota_kf, float(K)), axis=0, keepdims=True)
            sel = iota_kf == j1
            zsum = zsum + jnp.exp(fm - top)
            nxt = jnp.sum(jnp.where(sel, cnt, 0.0), axis=0, keepdims=True) + 1.0
            cnt = jnp.where(sel, cnt + 1.0, cnt)
            sv1n = jnp.sum(jnp.where(iota_kf == nxt, sv1, 0.0), axis=0, keepdims=True)
            sv1n = jnp.where(nxt >= float(K), neg, sv1n)
            sv0s = jnp.sum(jnp.where(sel, sv0, 0.0), axis=0, keepdims=True)
            front = jnp.where(sel, sv0s + sv1n, front)
            yield
        cntp = jnp.zeros((N_KEYS, cw), F32)
        for j in range(K):
            cntp = jnp.where(work0 == -(RANK_MARK + j * RANK_MARK_STEP), cnt[j:j + 1, :], cntp)
            if j % 4 == 3:
                yield
        cnt_ref[h, ch] = cntp
        p0_ref[h, ch] = jnp.exp(s0 - sv0[0:1, :]) * (GELU_IN_SCALE / zsum)

    def body(it, carry):
        units = [unit(2 * it), unit(2 * it + 1)]
        while units:
            for u in list(units):
                if next(u, "done") == "done":
                    units.remove(u)
        return carry

    lax.fori_loop(0, PEER_HEADS * nch // 2, body, 0)


def _peer_select(x1, g2, wq, sk, tq, cw):
    t = x1.shape[0]
    nch = tq // cw
    meta = lambda dt: jax.ShapeDtypeStruct((PEER_HEADS, t // cw, N_KEYS, cw), dt)
    mspec = pl.BlockSpec((PEER_HEADS, nch, N_KEYS, cw), lambda i: (0, i, 0, 0))
    return pl.pallas_call(
        functools.partial(_peer_select_kernel, tq=tq, cw=cw),
        grid=(t // tq,),
        in_specs=[
            pl.BlockSpec((tq, D_MODEL), lambda i: (i, 0)),
            pl.BlockSpec((1, D_MODEL), lambda i: (0, 0)),
            pl.BlockSpec((D_MODEL, PEER_HEADS * PEER_DQ), lambda i: (0, 0)),
            pl.BlockSpec((2, N_KEYS, PEER_DQ // 2), lambda i: (0, 0, 0)),
        ],
        out_specs=[pl.BlockSpec((tq, D_MODEL), lambda i: (i, 0)), mspec, mspec, mspec, mspec],
        out_shape=[jax.ShapeDtypeStruct((t, D_MODEL), BF16), meta(BF16), meta(BF16), meta(F32), meta(F32)],
        scratch_shapes=[pltpu.VMEM((2 * PEER_HEADS, nch, N_KEYS, cw), F32)],
        compiler_params=_params(("arbitrary",)),
        name="peer_select",
    )(x1, g2, wq, sk)


def _peer_dense_kernel(x_ref, h2_ref, rank_ref, p1_ref, cnt_ref, p0_ref, u_ref, vt_ref, fg_ref,
                       out_ref, acc_scr, g_scr, ht_scr, *, tb, eb, ec, cw, nblk_e, final_norm):
    ngrp = eb // ec
    j = pl.program_id(1)

    @pl.when(j == 0)
    def _():
        acc_scr[...] = jnp.zeros_like(acc_scr)

    for grp in range(ngrp):
        es = slice(grp * ec, (grp + 1) * ec)
        u_grp = pltpu.bitcast(u_ref[grp * (ec // 2):(grp + 1) * (ec // 2), :], BF16)
        ht_scr[grp] = lax.dot_general(u_grp, h2_ref[...], NT_DIMS, preferred_element_type=F32)
    for grp in range(ngrp):
        es = slice(grp * ec, (grp + 1) * ec)
        for k in range(ec // N_KEYS):
            r = grp * (ec // N_KEYS) + k
            ks = slice(grp * ec + k * N_KEYS, grp * ec + (k + 1) * N_KEYS)
            for ch in range(tb // cw):
                cs = slice(ch * cw, (ch + 1) * cw)
                def row8(ref, h):
                    parts = [jnp.broadcast_to(ref[h, ch, r:r + 1, lt * LANES:(lt + 1) * LANES], (8, LANES))
                             for lt in range(cw // LANES)]
                    return jnp.concatenate(parts, axis=1).astype(BF16)[None]

                w = jnp.zeros((N_KEYS // 8, 8, cw), BF16)
                for h in range(PEER_HEADS):
                    rk = rank_ref[h, ch].reshape(N_KEYS // 8, 8, cw)
                    p1 = p1_ref[h, ch].reshape(N_KEYS // 8, 8, cw)
                    w = w + jnp.where(rk < row8(cnt_ref, h), p1 * row8(p0_ref, h), jnp.zeros_like(w))
                z = ht_scr[grp, k * N_KEYS:(k + 1) * N_KEYS, cs]
                g_scr[ks, cs] = w.reshape(N_KEYS, cw) * (z * (1.0 + lax.erf(z))).astype(BF16)
        acc_scr[...] += jnp.dot(pltpu.bitcast(vt_ref[:, es], BF16), g_scr[es, :],
                                preferred_element_type=F32)

    @pl.when(j == nblk_e - 1)
    def _():
        y = x_ref[...] + acc_scr[...].T
        if final_norm:
            y = _rms(y, fg_ref[...])
        out_ref[...] = y


def _peer_dense(x1, h2, rank, p1, cnt, p0, u_bf, vt_bf, fg, tb, eb, cw, final_norm):
    t = x1.shape[0]
    nch = tb // cw
    nblk_e = N_EXPERTS // eb
    ec = min(eb, 512)
    mspec = pl.BlockSpec((PEER_HEADS, nch, N_KEYS, cw), lambda i, j: (0, i, 0, 0))
    rspec = pl.BlockSpec((PEER_HEADS, nch, eb // N_KEYS, cw), lambda i, j: (0, i, j, 0))
    return pl.pallas_call(
        functools.partial(_peer_dense_kernel, tb=tb, eb=eb, ec=ec, cw=cw, nblk_e=nblk_e,
                          final_norm=final_norm),
        grid=(t // tb, nblk_e),
        in_specs=[
            pl.BlockSpec((tb, D_MODEL), lambda i, j: (i, 0)),
            pl.BlockSpec((tb, D_MODEL), lambda i, j: (i, 0)),
            mspec, mspec, rspec, rspec,
            pl.BlockSpec((eb // 2, D_MODEL), lambda i, j: (j, 0)),
            pl.BlockSpec((D_MODEL // 2, eb), lambda i, j: (0, j)),
            pl.BlockSpec((1, D_MODEL), lambda i, j: (0, 0)),
        ],
        out_specs=pl.BlockSpec((tb, D_MODEL), lambda i, j: (i, 0)),
        out_shape=jax.ShapeDtypeStruct((t, D_MODEL), F32),
        scratch_shapes=[pltpu.VMEM((D_MODEL, tb), F32), pltpu.VMEM((eb, tb), BF16),
                        pltpu.VMEM((eb // ec, ec, tb), F32)],
        compiler_params=_params(("arbitrary", "arbitrary")),
        name="peer_dense",
    )(x1, h2, rank, p1, cnt, p0, u_bf, vt_bf, fg)


def _layer_weights(li, w_in, gla_w2, gla_b2, gla_norm_g, conv_w, conv_b, ml_i_b, ml_f_b, ml_norm_g,
                   w_out, peer_wq, peer_subkeys, peer_u, peer_v):
    w = w_in[li]
    o = 0
    segs = {}
    for name, width in (("aq", GLA_QK), ("ak", GLA_QK), ("av", D_MODEL), ("ag", D_MODEL),
                        ("alr", GLA_GATE_RANK), ("bq", ML_W), ("bk", ML_W), ("bv", ML_W),
                        ("bi", ML_HEADS), ("bf", ML_HEADS), ("bo", ML_W),
                        ("ga", D_MODEL), ("gb", D_MODEL)):
        segs[name] = w[:, o:o + width]
        o += width
    seg_w = WSEG_BLOCKS * D_MODEL
    starts = (0, seg_w + GLA_GATE_RANK, 2 * seg_w + GLA_GATE_RANK + 2 * ML_HEADS)
    w_segs = tuple(w[:, s:s + seg_w].astype(BF16) for s in starts)
    pad = lambda a: jnp.pad(a, ((0, 0), (0, LANES - a.shape[1])))
    w_small = jnp.concatenate([pad(segs["alr"]), pad(segs["bi"]), pad(segs["bf"])], axis=1).astype(BF16)
    w_gt = jnp.concatenate([segs["bi"], segs["bf"]], axis=1).T.astype(BF16)
    gate_b = jnp.concatenate([ml_i_b[li], ml_f_b[li]])
    u_packed, vt_packed = _table_prep(peer_u, peer_v, li)
    return dict(
        w_segs=w_segs, w_small=w_small, w_gt=w_gt,
        w2p=jnp.pad(gla_w2[li], ((0, LANES - GLA_GATE_RANK), (0, 0))),
        b2=gla_b2[li][None, :], gng=gla_norm_g[li][None, :],
        cw=conv_w[li], cb=conv_b[li][None, :],
        gbr=jnp.concatenate([jnp.pad(ml_i_b[li], (0, LANES - ML_HEADS)),
                             jnp.pad(ml_f_b[li], (0, LANES - ML_HEADS))])[None, :],
        gbc=gate_b[:, None], mng=ml_norm_g[li][None, :],
        wout=w_out[li].astype(BF16),
        wq=peer_wq[li].astype(BF16), sk=peer_subkeys[li].astype(BF16),
        u=u_packed, vt=vt_packed,
    )


def _table_prep_kernel(u_ref, v_ref, up_ref, vtp_ref):
    up_ref[...] = pltpu.bitcast((u_ref[0] * GELU_IN_SCALE).astype(BF16), jnp.uint32)
    vtp_ref[...] = pltpu.bitcast(v_ref[0].T.astype(BF16), jnp.uint32)


def _table_prep(u_tabs, v_tabs, li, eb=1024):
    _, ne, d = u_tabs.shape
    return pl.pallas_call(
        _table_prep_kernel,
        grid=(ne // eb,),
        in_specs=[pl.BlockSpec((1, eb, d), lambda i: (li, i, 0)), pl.BlockSpec((1, eb, d), lambda i: (li, i, 0))],
        out_specs=[pl.BlockSpec((eb // 2, d), lambda i: (i, 0)), pl.BlockSpec((d // 2, eb), lambda i: (0, i))],
        out_shape=[jax.ShapeDtypeStruct((ne // 2, d), jnp.uint32),
                   jax.ShapeDtypeStruct((d // 2, ne), jnp.uint32)],
        compiler_params=_params(("arbitrary",)),
        name="table_prep",
    )(u_tabs, v_tabs)


def _pick(n, cands):
    for c in cands:
        if n % c == 0:
            return c
    return n


def _peer(x1, g2, lw, fg, final_norm):
    t = x1.shape[0]
    tq = _pick(t, (512, 256, 128))
    tb = _pick(t, (512, 256, 128))
    cw = _pick(tb, (2 * LANES, LANES))
    h2, rank, p1, cnt, p0 = _peer_select(x1, g2, lw["wq"], lw["sk"], tq, cw)
    return _peer_dense(x1, h2, rank, p1, cnt, p0, lw["u"], lw["vt"], fg, tb, 2048, cw, final_norm)


def kernel(x_prompt, x_sample, state_gla, state_mlstm_c, state_mlstm_n, state_mlstm_m, state_conv,
           norm1_g, w_in, gla_w2, gla_b2, gla_norm_g, conv_w, conv_b, ml_i_b, ml_f_b, ml_norm_g,
           w_out, norm2_g, peer_wq, peer_subkeys, peer_u, peer_v, final_g):
    depth = w_in.shape[0]
    bsz, seq, _ = x_prompt.shape
    nb = x_sample.shape[0]
    assert seq % CHUNK == 0 and x_sample.shape[1] == 1 and nb % LANES == 0
    tp = bsz * seq
    tm = _pick(tp, (2048, 1024, 512, 256, 128))
    tc = _pick(seq, (256, 128, 64))
    npar = _pick(bsz, (2, 1))
    fg = final_g[None, :]

    xp = x_prompt.reshape(tp, D_MODEL)
    xs = x_sample.reshape(nb, D_MODEL)
    p_out = [[] for _ in range(5)]
    s_out = [[] for _ in range(5)]
    s_gla = s_c = None
    for li in range(depth):
        lw = _layer_weights(li, w_in, gla_w2, gla_b2, gla_norm_g, conv_w, conv_b, ml_i_b, ml_f_b,
                            ml_norm_g, w_out, peer_wq, peer_subkeys, peer_u, peer_v)
        g1 = norm1_g[li][None, :]
        g2 = norm2_g[li][None, :]
        last = li == depth - 1

        big, sm, gt = _inproj(xp, g1, lw["w_segs"], lw["w_small"], lw["w_gt"], tm)
        gtc = gt.reshape(8, bsz, seq // CHUNK, CHUNK).transpose(1, 2, 0, 3)
        x1, gla, c, n, m, conv = _mixer_prompt(
            xp.reshape(bsz, seq, D_MODEL), big.reshape(bsz, seq, BIG_COLS),
            sm.reshape(bsz, seq, SMALL_COLS), gtc, lw, bsz, seq, tc, npar)
        xp = _peer(x1.reshape(tp, D_MODEL), g2, lw, fg, last)
        for lst, val in zip(p_out, (gla, c, n, m[:, 0, :ML_HEADS], conv)):
            lst.append(val)

        big, sm, _ = _inproj(xs, g1, lw["w_segs"], lw["w_small"], lw["w_gt"], nb)
        x1, s_gla, s_c, n, m, conv = _mixer_sample(
            xs, big, sm, state_gla, state_mlstm_c,
            state_mlstm_n[li].reshape(nb, ML_W),
            jnp.pad(state_mlstm_m[li], ((0, 0), (0, LANES - ML_HEADS))),
            state_conv[li].reshape(nb, (CONV_W - 1) * 2 * ML_W), lw, li, s_gla, s_c)
        xs = _peer(x1, g2, lw, fg, last)
        for lst, val in zip(s_out[2:], (n.reshape(nb, ML_HEADS, ML_DH), m[:, :ML_HEADS],
                                        conv.reshape(nb, CONV_W - 1, 2 * ML_W))):
            lst.append(val)

    y_prompt = xp.reshape(bsz, seq, D_MODEL)
    y_sample = xs.reshape(nb, 1, D_MODEL)
    return (y_prompt, y_sample, *[jnp.stack(v) for v in p_out],
            s_gla, s_c, *[jnp.stack(v) for v in s_out[2:]])
```

```python
import functools

import jax
import jax.numpy as jnp
from jax import lax
from jax.experimental import pallas as pl
from jax.experimental.pallas import tpu as pltpu

F32 = jnp.float32
BF16 = jnp.bfloat16
HIGHEST = lax.Precision.HIGHEST

D_MODEL = 1024
GLA_HEADS = 4
GLA_DK = 128
GLA_DV = 256
GLA_GATE_RANK = 16
GLA_TAU = 16.0
GLA_QK = GLA_HEADS * GLA_DK
ML_HEADS = 4
ML_DH = 256
ML_W = ML_HEADS * ML_DH
CONV_W = 4
CHUNK = 64
PEER_HEADS = 8
PEER_DQ = 256
N_KEYS = 128
N_EXPERTS = N_KEYS * N_KEYS
PEER_TOPK = 16
EPS = 1e-6
GELU_IN_SCALE = 0.7071067811865476
RANK_MARK = 2.0 ** 100
RANK_MARK_STEP = 2.0 ** 96

LANES = 128
BIG_COLS = 9 * D_MODEL
WSEG_BLOCKS = 3
SAMPLES_PER_STEP = 4
HEADS_PER_ROUND = 4
SMALL_COLS = 3 * LANES
VMEM_LIMIT_BYTES = 56 * 1024 * 1024

NT_DIMS = (((1,), (1,)), ((), ()))
TN_DIMS = (((0,), (0,)), ((), ()))


def _params(sem, flags=None):
    return pltpu.CompilerParams(dimension_semantics=sem, vmem_limit_bytes=VMEM_LIMIT_BYTES, flags=flags)


def _log_sigmoid(x):
    return jnp.minimum(x, 0.0) - jnp.log1p(jnp.exp(-jnp.abs(x)))


def _sigmoid(x):
    return 1.0 / (1.0 + jnp.exp(-x))


def _silu(x):
    return x * _sigmoid(x)


def _rms(x, g):
    return x * lax.rsqrt(jnp.mean(x * x, axis=-1, keepdims=True) + EPS) * g


def _lane_col(a, j):
    lane = lax.broadcasted_iota(jnp.int32, a.shape, 1)
    return jnp.sum(jnp.where(lane == j, a, 0.0), axis=1, keepdims=True)


def _inproj_kernel(x_ref, g_ref, wa_ref, wb_ref, wc_ref, wsm_ref, wgt_ref, big_ref, sm_ref, gt_ref, h_scr):
    j = pl.program_id(1)

    @pl.when(j == 0)
    def _():
        hb = _rms(x_ref[...], g_ref[...]).astype(BF16)
        h_scr[...] = hb
        sm_ref[...] = jnp.dot(hb, wsm_ref[...], preferred_element_type=F32)
        gt_ref[...] = lax.dot_general(wgt_ref[...], hb, NT_DIMS, preferred_element_type=F32)

    for s, w_ref in enumerate((wa_ref, wb_ref, wc_ref)):
        @pl.when((j >= s * WSEG_BLOCKS) & (j < (s + 1) * WSEG_BLOCKS))
        def _(w_ref=w_ref):
            big_ref[...] = jnp.dot(h_scr[...], w_ref[...], preferred_element_type=F32).astype(BF16)


def _inproj(x, g, w_segs, w_small, w_gt, tm):
    t = x.shape[0]
    tn = D_MODEL

    def wspec(s):
        return pl.BlockSpec((D_MODEL, tn), lambda i, j: (0, jnp.clip(j - s * WSEG_BLOCKS, 0, WSEG_BLOCKS - 1)))

    return pl.pallas_call(
        _inproj_kernel,
        grid=(t // tm, BIG_COLS // tn),
        in_specs=[
            pl.BlockSpec((tm, D_MODEL), lambda i, j: (i, 0)),
            pl.BlockSpec((1, D_MODEL), lambda i, j: (0, 0)),
            wspec(0), wspec(1), wspec(2),
            pl.BlockSpec((D_MODEL, SMALL_COLS), lambda i, j: (0, 0)),
            pl.BlockSpec((8, D_MODEL), lambda i, j: (0, 0)),
        ],
        out_specs=[
            pl.BlockSpec((tm, tn), lambda i, j: (i, j)),
            pl.BlockSpec((tm, SMALL_COLS), lambda i, j: (i, 0)),
            pl.BlockSpec((8, tm), lambda i, j: (0, i)),
        ],
        out_shape=[
            jax.ShapeDtypeStruct((t, BIG_COLS), BF16),
            jax.ShapeDtypeStruct((t, SMALL_COLS), F32),
            jax.ShapeDtypeStruct((8, t), F32),
        ],
        scratch_shapes=[pltpu.VMEM((tm, D_MODEL), BF16)],
        compiler_params=_params(("arbitrary", "arbitrary")),
        name="inproj",
    )(x, g, *w_segs, w_small, w_gt)


def _headnorm(o, g_row):
    return o * lax.rsqrt(jnp.mean(o * o, axis=-1, keepdims=True) + EPS) * g_row


def _mixer_prompt_kernel(x_ref, aq_ref, ak_ref, av_ref, ag_ref, bq_ref, bk_ref, bv_ref, bo_ref,
                         ga_ref, gb_ref, sm_ref, gt_ref, w2_ref, b2_ref, gng_ref, cw_ref, cb_ref,
                         gbr_ref, gbc_ref, mng_ref, wout_ref,
                         x1_ref, gla_ref, c_ref, n_ref, m_ref, conv_ref,
                         st_scr, xp_scr, qk_scr, mix_scr, *, tc, nblk, npar):
    i = pl.program_id(1)
    L = CHUNK

    @pl.when(i == 0)
    def _init():
        st_scr[...] = jnp.zeros_like(st_scr)
        c_ref[...] = jnp.zeros_like(c_ref)
        n_ref[...] = jnp.zeros_like(n_ref)
        m_ref[...] = jnp.zeros_like(m_ref)
        xp_scr[:, 0:8, :] = jnp.zeros((npar, 8, 2 * ML_W), F32)

    for p in range(npar):
        xp_scr[p, 8:8 + tc, 0:ML_W] = bq_ref[p].astype(F32)
        xp_scr[p, 8:8 + tc, ML_W:2 * ML_W] = bk_ref[p].astype(F32)
        y = cb_ref[...]
        for j in range(CONV_W):
            y = y + cw_ref[j:j + 1, :] * xp_scr[p, 5 + j:5 + j + tc, :]
        qk = _silu(y)
        qk_scr[p, :, 0:ML_W] = qk[:, 0:ML_W].astype(BF16)
        qk_scr[p, :, ML_W:2 * ML_W] = (qk[:, ML_W:2 * ML_W] * (ML_DH ** -0.5)).astype(BF16)
        xp_scr[p, 5:8, :] = xp_scr[p, tc + 5:tc + 8, :]

    @pl.when(i == nblk - 1)
    def _():
        conv_ref[...] = xp_scr[:, 5:8, :]

    row = lax.broadcasted_iota(jnp.int32, (L, L), 0)
    col = lax.broadcasted_iota(jnp.int32, (L, L), 1)
    causal = col <= row
    tril = causal.astype(F32)
    triu = (row <= col).astype(F32)
    lane_row = lax.broadcasted_iota(jnp.int32, (1, LANES), 1)

    def gates(p, c, rows):
        sm = sm_ref[p, rows, :]
        z = jnp.dot(sm[:, 0:LANES], w2_ref[...], precision=HIGHEST, preferred_element_type=F32) + b2_ref[...]
        log_a = _log_sigmoid(z) * (1.0 / GLA_TAU)
        bc = jnp.dot(tril, log_a, precision=HIGHEST, preferred_element_type=F32)
        li_c_all = sm[:, LANES:2 * LANES] + gbr_ref[:, 0:LANES]
        lf_c_all = _log_sigmoid(sm[:, 2 * LANES:3 * LANES] + gbr_ref[:, LANES:2 * LANES])
        bcum_c_all = jnp.dot(tril, lf_c_all, precision=HIGHEST, preferred_element_type=F32)
        gtb = gt_ref[p, c] + gbc_ref[...]
        bcum_r_all = jnp.dot(_log_sigmoid(gtb), triu, precision=HIGHEST, preferred_element_type=F32)
        return bc, li_c_all, bcum_c_all, gtb, bcum_r_all

    def head(p, h, rows, shared):
        bc, li_c_all, bcum_c_all, gtb, bcum_r_all = shared
        sl = slice(h * GLA_DK, (h + 1) * GLA_DK)
        vl = slice(h * GLA_DV, (h + 1) * GLA_DV)
        b = bc[:, sl]
        mid = b[L // 2 - 1:L // 2, :]
        bl = b[L - 1:L, :]
        q = aq_ref[p, rows, sl].astype(F32) * (GLA_DK ** -0.5)
        k = ak_ref[p, rows, sl].astype(F32)
        v = av_ref[p, rows, vl]
        q_in = (q * jnp.exp(b)).astype(BF16)
        q_at = (q * jnp.exp(b - mid)).astype(BF16)
        yield
        k_at = (k * jnp.exp(mid - b)).astype(BF16)
        k_out = (k * jnp.exp(bl - b)).astype(BF16)
        yield
        att = lax.dot_general(q_at, k_at, NT_DIMS, preferred_element_type=F32)
        att = jnp.where(causal, att, 0.0)
        st = st_scr[p, h]
        yield
        o = lax.dot_general(q_in, st.astype(BF16), NT_DIMS, preferred_element_type=F32)
        yield
        o = o + jnp.dot(att.astype(BF16), v, preferred_element_type=F32)
        yield
        st_scr[p, h] = st * jnp.exp(bl) + lax.dot_general(v, k_out, TN_DIMS, preferred_element_type=F32)
        yield
        o_a = _headnorm(o, gng_ref[:, vl]) * _silu(ag_ref[p, rows, vl].astype(F32))
        yield

        li_c = _lane_col(li_c_all, h)
        b_c = _lane_col(bcum_c_all, h)
        li_r = gtb[h:h + 1, :]
        b_r = bcum_r_all[ML_HEADS + h:ML_HEADS + h + 1, :]
        m_prev = jnp.sum(jnp.where(lane_row == h, m_ref[p], 0.0), axis=1, keepdims=True)
        a_int = b_c + m_prev
        dm = jnp.where(causal, b_c - b_r + li_r, -jnp.inf)
        m_t = jnp.maximum(a_int, jnp.max(dm, axis=1, keepdims=True))
        yield
        w_int = jnp.exp(a_int - m_t)
        dexp = jnp.exp(dm - m_t)
        qb = qk_scr[p, rows, vl]
        kb = qk_scr[p, rows, ML_W + h * ML_DH:ML_W + (h + 1) * ML_DH]
        vb = bv_ref[p, rows, vl]
        s = lax.dot_general(qb, kb, NT_DIMS, preferred_element_type=F32) * dexp
        yield
        cst = c_ref[p, h]
        num = w_int * jnp.dot(qb, cst.astype(BF16), preferred_element_type=F32)
        yield
        num = num + jnp.dot(s.astype(BF16), vb, preferred_element_type=F32)
        nrow = n_ref[p, h:h + 1, :]
        den = (w_int * jnp.sum(qb.astype(F32) * nrow, axis=1, keepdims=True)
               + jnp.sum(s, axis=1, keepdims=True))
        yield
        hh = num / jnp.maximum(jnp.abs(den), jnp.exp(-m_t))
        b_last = b_c[L - 1:L, :]
        g_c = b_last - b_c + li_c
        m_new = jnp.maximum(b_last + m_prev, jnp.max(g_c, axis=0, keepdims=True))
        f_c = jnp.exp(b_last + m_prev - m_new)
        w_s = jnp.exp(g_c - m_new)
        kw = kb.astype(F32) * w_s
        yield
        c_ref[p, h] = f_c * cst + lax.dot_general(kw.astype(BF16), vb, TN_DIMS, preferred_element_type=F32)
        n_ref[p, h:h + 1, :] = f_c * nrow + jnp.sum(kw, axis=0, keepdims=True)
        m_ref[p] = jnp.where(lane_row == h, m_new, m_ref[p])
        yield
        o_b = _headnorm(hh, mng_ref[:, vl]) * _sigmoid(bo_ref[p, rows, vl].astype(F32))
        yield
        mix = (_sigmoid(ga_ref[p, rows, vl].astype(F32)) * o_a
               + _sigmoid(gb_ref[p, rows, vl].astype(F32)) * o_b)
        mix_scr[p, rows, vl] = mix.astype(BF16)

    def chunk(c, carry):
        rows = pl.ds(pl.multiple_of(c * L, L), L)
        shared = [gates(p, c, rows) for p in range(npar)]
        for h0 in range(0, GLA_HEADS, HEADS_PER_ROUND):
            units = [head(p, h, rows, shared[p]) for h in range(h0, h0 + HEADS_PER_ROUND) for p in range(npar)]
            while units:
                for u in list(units):
                    if next(u, "done") == "done":
                        units.remove(u)
        return carry

    lax.fori_loop(0, tc // L, chunk, 0)

    for p in range(npar):
        x1_ref[p] = x_ref[p] + jnp.dot(mix_scr[p], wout_ref[...], preferred_element_type=F32)

    @pl.when(i == nblk - 1)
    def _fin():
        for p in range(npar):
            for h in range(GLA_HEADS):
                gla_ref[p, h] = st_scr[p, h].T


def _mixer_prompt(x, big, sm, gtc, lw, bsz, seq, tc, npar):
    nblk = seq // tc

    def seg(width, idx):
        return pl.BlockSpec((npar, tc, width), lambda b, i: (b, i, idx))

    const = lambda shape: pl.BlockSpec(shape, lambda b, i: tuple(0 for _ in shape))
    in_specs = [
        seg(D_MODEL, 0),
        seg(GLA_QK, 0), seg(GLA_QK, 1),
        seg(D_MODEL, 1), seg(D_MODEL, 2),
        seg(D_MODEL, 3), seg(D_MODEL, 4), seg(D_MODEL, 5),
        seg(D_MODEL, 6), seg(D_MODEL, 7), seg(D_MODEL, 8),
        seg(SMALL_COLS, 0),
        pl.BlockSpec((npar, tc // CHUNK, 8, CHUNK), lambda b, i: (b, i, 0, 0)),
        const((LANES, GLA_QK)), const((1, GLA_QK)), const((1, D_MODEL)),
        const((CONV_W, 2 * ML_W)), const((1, 2 * ML_W)),
        const((1, 2 * LANES)), const((8, 1)), const((1, D_MODEL)),
        const((D_MODEL, D_MODEL)),
    ]
    out_specs = [
        seg(D_MODEL, 0),
        pl.BlockSpec((npar, GLA_HEADS, GLA_DK, GLA_DV), lambda b, i: (b, 0, 0, 0)),
        pl.BlockSpec((npar, ML_HEADS, ML_DH, ML_DH), lambda b, i: (b, 0, 0, 0)),
        pl.BlockSpec((npar, ML_HEADS, ML_DH), lambda b, i: (b, 0, 0)),
        pl.BlockSpec((npar, 1, LANES), lambda b, i: (b, 0, 0)),
        pl.BlockSpec((npar, CONV_W - 1, 2 * ML_W), lambda b, i: (b, 0, 0)),
    ]
    out_shape = [
        jax.ShapeDtypeStruct((bsz, seq, D_MODEL), F32),
        jax.ShapeDtypeStruct((bsz, GLA_HEADS, GLA_DK, GLA_DV), F32),
        jax.ShapeDtypeStruct((bsz, ML_HEADS, ML_DH, ML_DH), F32),
        jax.ShapeDtypeStruct((bsz, ML_HEADS, ML_DH), F32),
        jax.ShapeDtypeStruct((bsz, 1, LANES), F32),
        jax.ShapeDtypeStruct((bsz, CONV_W - 1, 2 * ML_W), F32),
    ]
    scratch = [
        pltpu.VMEM((npar, GLA_HEADS, GLA_DV, GLA_DK), F32),
        pltpu.VMEM((npar, tc + 8, 2 * ML_W), F32),
        pltpu.VMEM((npar, tc, 2 * ML_W), BF16),
        pltpu.VMEM((npar, tc, D_MODEL), BF16),
    ]
    return pl.pallas_call(
        functools.partial(_mixer_prompt_kernel, tc=tc, nblk=nblk, npar=npar),
        grid=(bsz // npar, nblk),
        in_specs=in_specs, out_specs=out_specs, out_shape=out_shape,
        scratch_shapes=scratch,
        compiler_params=_params(("arbitrary", "arbitrary")),
        name="mixer_prompt",
    )(x, big, big, big, big, big, big, big, big, big, big, sm, gtc,
      lw["w2p"], lw["b2"], lw["gng"], lw["cw"], lw["cb"], lw["gbr"], lw["gbc"], lw["mng"], lw["wout"])


def _col_bcast(row):
    return jnp.broadcast_to(row, (LANES, LANES)).T


def _mixer_sample_kernel(*refs, nb, has_prev, sps):
    n_in = 18 if has_prev else 16
    (x_ref, big_ref, sm_ref, gla_in, c_in, n_ref, m_ref, conv_ref,
     w2_ref, b2_ref, gng_ref, cw_ref, cb_ref, gbr_ref, mng_ref, wout_ref) = refs[:16]
    x1_ref, gla_out, c_out, n_o, m_o, conv_o = refs[n_in:n_in + 6]
    a_s, qa_s, ka_s, va_s, kw_s, qb_s, vb_s, fc_s, den_s, oa_s, num_s = refs[n_in + 6:]
    gla_ref, c_ref, gla_o, c_o = gla_in.at[0], c_in.at[0], gla_out.at[0], c_out.at[0]
    b = pl.program_id(0)
    seg = lambda idx: slice(idx * D_MODEL, (idx + 1) * D_MODEL)
    lane = lax.broadcasted_iota(jnp.int32, (nb, LANES), 1)

    @pl.when(b == 0)
    def _prep():
        sm = sm_ref[...]
        z = jnp.dot(sm[:, 0:LANES], w2_ref[...], precision=HIGHEST, preferred_element_type=F32) + b2_ref[...]
        a_s[...] = jnp.exp(_log_sigmoid(z) * (1.0 / GLA_TAU))
        qa_s[...] = big_ref[:, 0:GLA_QK].astype(F32) * (GLA_DK ** -0.5)
        ka_s[...] = big_ref[:, GLA_QK:2 * GLA_QK].astype(F32)
        va_s[...] = big_ref[:, seg(1)].astype(F32)
        vb_s[...] = big_ref[:, seg(5)].astype(F32)
        xq = big_ref[:, seg(3)].astype(F32)
        xk = big_ref[:, seg(4)].astype(F32)
        w = 2 * ML_W
        y = cb_ref[...]
        for j in range(CONV_W - 1):
            y = y + cw_ref[j:j + 1, :] * conv_ref[:, j * w:(j + 1) * w]
        y = y + cw_ref[CONV_W - 1:CONV_W, :] * jnp.concatenate([xq, xk], axis=1)
        qk = _silu(y)
        qb = qk[:, 0:ML_W]
        kb = qk[:, ML_W:w] * (ML_DH ** -0.5)
        qb_s[...] = qb
        conv_o[:, 0:2 * w] = conv_ref[:, w:3 * w]
        conv_o[:, 2 * w:2 * w + ML_W] = xq
        conv_o[:, 2 * w + ML_W:3 * w] = xk

        li = sm[:, LANES:2 * LANES] + gbr_ref[:, 0:LANES]
        lf = _log_sigmoid(sm[:, 2 * LANES:3 * LANES] + gbr_ref[:, LANES:2 * LANES])
        m_old = m_ref[...]
        m_new = jnp.maximum(lf + m_old, li)
        f_c = jnp.exp(lf + m_old - m_new)
        w_s = jnp.exp(li - m_new)
        m_o[...] = m_new
        fc_s[...] = f_c
        den = jnp.zeros((nb, LANES), F32)
        for h in range(ML_HEADS):
            vl = slice(h * ML_DH, (h + 1) * ML_DH)
            kw = kb[:, vl] * _lane_col(w_s, h)
            kw_s[:, vl] = kw
            n_new = _lane_col(f_c, h) * n_ref[:, vl] + kw
            n_o[:, vl] = n_new
            den = jnp.where(lane == h, jnp.sum(qb[:, vl] * n_new, axis=1, keepdims=True), den)
        den_s[...] = den
        oa_s[...] = jnp.zeros_like(oa_s)
        num_s[...] = jnp.zeros_like(num_s)

    tile = pl.ds(pl.multiple_of(lax.shift_right_logical(b * sps, 3) * 8, 8), 8)
    lane_row = lax.broadcasted_iota(jnp.int32, (1, LANES), 1)

    for s in range(sps):
        sub = jnp.bitwise_and(b * sps + s, 7)

        def row_get(ref, cols, sub=sub):
            t = ref[tile, cols]
            pick = lax.broadcasted_iota(jnp.int32, t.shape, 0) == sub
            return jnp.sum(jnp.where(pick, t, 0.0), axis=0, keepdims=True)

        def row_set(ref, cols, val, sub=sub):
            t = ref[tile, cols]
            pick = lax.broadcasted_iota(jnp.int32, t.shape, 0) == sub
            ref[tile, cols] = jnp.where(pick, val, t)

        fc_row = row_get(fc_s, slice(0, LANES))
        for h in range(GLA_HEADS):
            sl = slice(h * GLA_DK, (h + 1) * GLA_DK)
            vl = slice(h * GLA_DV, (h + 1) * GLA_DV)
            a_c = _col_bcast(row_get(a_s, sl))
            k_c = _col_bcast(row_get(ka_s, sl))
            q_c = _col_bcast(row_get(qa_s, sl))
            v_row = row_get(va_s, vl)
            s_old = gla_ref[s, h]
            halves = []
            for p in range(GLA_DV // LANES):
                ls = slice(p * LANES, (p + 1) * LANES)
                s_new = s_old[:, ls] * a_c + k_c * v_row[:, ls]
                gla_o[s, h, :, ls] = s_new
                halves.append(jnp.sum(q_c * s_new, axis=0, keepdims=True))
            row_set(oa_s, vl, jnp.concatenate(halves, axis=1))

            f_c = jnp.sum(jnp.where(lane_row == h, fc_row, 0.0), axis=1, keepdims=True)
            vb_row = row_get(vb_s, vl)
            acc = [jnp.zeros((1, LANES), F32) for _ in range(ML_DH // LANES)]
            for r in range(ML_DH // LANES):
                rs = slice(r * LANES, (r + 1) * LANES)
                ks = slice(h * ML_DH + r * LANES, h * ML_DH + (r + 1) * LANES)
                kw_c = _col_bcast(row_get(kw_s, ks))
                qb_c = _col_bcast(row_get(qb_s, ks))
                for p in range(ML_DH // LANES):
                    ls = slice(p * LANES, (p + 1) * LANES)
                    c_new = f_c * c_ref[s, h, rs, ls] + kw_c * vb_row[:, ls]
                    c_o[s, h, rs, ls] = c_new
                    acc[p] = acc[p] + jnp.sum(qb_c * c_new, axis=0, keepdims=True)
            row_set(num_s, vl, jnp.concatenate(acc, axis=1))

    @pl.when(b == nb // sps - 1)
    def _post():
        m_new = m_o[...]
        den = den_s[...]
        for h in range(GLA_HEADS):
            vl = slice(h * GLA_DV, (h + 1) * GLA_DV)
            o_a = _headnorm(oa_s[:, vl], gng_ref[:, vl]) * _silu(big_ref[:, D_MODEL * 2 + h * GLA_DV:
                                                                       D_MODEL * 2 + (h + 1) * GLA_DV].astype(F32))
            dn = jnp.maximum(jnp.abs(_lane_col(den, h)), jnp.exp(-_lane_col(m_new, h)))
            hh = num_s[:, vl] / dn
            o_b = _headnorm(hh, mng_ref[:, vl]) * _sigmoid(
                big_ref[:, 6 * D_MODEL + h * ML_DH:6 * D_MODEL + (h + 1) * ML_DH].astype(F32))
            ga = big_ref[:, 7 * D_MODEL + h * ML_DH:7 * D_MODEL + (h + 1) * ML_DH].astype(F32)
            gb = big_ref[:, 8 * D_MODEL + h * ML_DH:8 * D_MODEL + (h + 1) * ML_DH].astype(F32)
            oa_s[:, vl] = _sigmoid(ga) * o_a + _sigmoid(gb) * o_b
        x1_ref[...] = x_ref[...] + jnp.dot(oa_s[...].astype(BF16), wout_ref[...], preferred_element_type=F32)


def _mixer_sample(x, big, sm, st_gla, st_c, st_n, st_m, st_conv, lw, li, prev_gla, prev_c):
    nb = x.shape[0]
    has_prev = prev_gla is not None
    full = lambda shape: pl.BlockSpec(shape, lambda b: tuple(0 for _ in shape))
    sps = SAMPLES_PER_STEP
    gla_spec = pl.BlockSpec((1, sps, GLA_HEADS, GLA_DK, GLA_DV), lambda b: (li, b, 0, 0, 0))
    c_spec = pl.BlockSpec((1, sps, ML_HEADS, ML_DH, ML_DH), lambda b: (li, b, 0, 0, 0))
    in_specs = [
        full((nb, D_MODEL)), full((nb, BIG_COLS)), full((nb, SMALL_COLS)),
        gla_spec, c_spec,
        full((nb, ML_W)), full((nb, LANES)), full((nb, 3 * 2 * ML_W)),
        full((LANES, GLA_QK)), full((1, GLA_QK)), full((1, D_MODEL)),
        full((CONV_W, 2 * ML_W)), full((1, 2 * ML_W)), full((1, 2 * LANES)), full((1, D_MODEL)),
        full((D_MODEL, D_MODEL)),
    ]
    operands = [x, big, sm, st_gla, st_c, st_n, st_m, st_conv,
                lw["w2p"], lw["b2"], lw["gng"], lw["cw"], lw["cb"], lw["gbr"], lw["mng"], lw["wout"]]
    aliases = {}
    if has_prev:
        in_specs += [pl.BlockSpec(memory_space=pl.ANY), pl.BlockSpec(memory_space=pl.ANY)]
        operands += [prev_gla, prev_c]
        aliases = {16: 1, 17: 2}
    out_specs = [
        full((nb, D_MODEL)),
        gla_spec, c_spec,
        full((nb, ML_W)), full((nb, LANES)), full((nb, 3 * 2 * ML_W)),
    ]
    out_shape = [
        jax.ShapeDtypeStruct((nb, D_MODEL), F32),
        jax.ShapeDtypeStruct(st_gla.shape, F32),
        jax.ShapeDtypeStruct(st_c.shape, F32),
        jax.ShapeDtypeStruct((nb, ML_W), F32),
        jax.ShapeDtypeStruct((nb, LANES), F32),
        jax.ShapeDtypeStruct((nb, 3 * 2 * ML_W), F32),
    ]
    vm = lambda cols: pltpu.VMEM((nb, cols), F32)
    scratch = [vm(GLA_QK), vm(GLA_QK), vm(GLA_QK), vm(D_MODEL), vm(ML_W), vm(ML_W), vm(ML_W),
               vm(LANES), vm(LANES), vm(D_MODEL), vm(ML_W)]
    return pl.pallas_call(
        functools.partial(_mixer_sample_kernel, nb=nb, has_prev=has_prev, sps=sps),
        grid=(nb // sps,),
        in_specs=in_specs, out_specs=out_specs, out_shape=out_shape,
        scratch_shapes=scratch,
        input_output_aliases=aliases,
        compiler_params=_params(("arbitrary",)),
        name="mixer_sample",
    )(*operands)


def _peer_select_kernel(x_ref, g_ref, wq_ref, sk_ref, h2_ref, rank_ref, p1_ref, cnt_ref, p0_ref,
                        sc_scr, *, tq, cw):
    nch = tq // cw
    hb = _rms(x_ref[...], g_ref[...]).astype(BF16)
    h2_ref[...] = hb
    q = jnp.dot(hb, wq_ref[...], preferred_element_type=F32).astype(BF16)
    for hc in range(2 * PEER_HEADS):
        for ch in range(nch):
            qs = q[ch * cw:(ch + 1) * cw, hc * LANES:(hc + 1) * LANES]
            sc_scr[hc, ch] = lax.dot_general(sk_ref[hc % 2], qs, NT_DIMS, preferred_element_type=F32)

    K = PEER_TOPK
    neg = -jnp.inf
    iota_k = lax.broadcasted_iota(jnp.int32, (K, cw), 0)
    iota_kf = iota_k.astype(F32)

    def unit(idx):
        h = idx // nch
        ch = idx % nch
        s0 = sc_scr[2 * h, ch]
        s1 = sc_scr[2 * h + 1, ch]

        sv0 = jnp.zeros((K, cw), F32)
        sv1 = jnp.zeros((K, cw), F32)
        work0, work1 = s0, s1
        for j in range(K):
            marker = -(RANK_MARK + j * RANK_MARK_STEP)
            mx0 = jnp.max(work0, axis=0, keepdims=True)
            work0 = jnp.where(work0 == mx0, marker, work0)
            sv0 = jnp.where(iota_k == j, mx0, sv0)
            yield
            mx1 = jnp.max(work1, axis=0, keepdims=True)
            work1 = jnp.where(work1 == mx1, marker, work1)
            sv1 = jnp.where(iota_k == j, mx1, sv1)
            yield
        rank1 = jnp.where(work1 <= -RANK_MARK, (-RANK_MARK - work1) * (1.0 / RANK_MARK_STEP), float(K))
        rank_ref[h, ch] = rank1.astype(BF16)
        p1_ref[h, ch] = jnp.exp(s1 - sv1[0:1, :]).astype(BF16)
        yield

        top = sv0[0:1, :] + sv1[0:1, :]
        front = sv0 + sv1[0:1, :]
        cnt = jnp.zeros((K, cw), F32)
        zsum = jnp.zeros((1, cw), F32)
        for _ in range(K):
            fm = jnp.max(front, axis=0, keepdims=True)
            j1 = jnp.min(jnp.where(front == fm, iota_kf, float(K)), axis=0, keepdims=True)
            sel = iota_kf == j1
            zsum = zsum + jnp.exp(fm - top)
            nxt = jnp.sum(jnp.where(sel, cnt, 0.0), axis=0, keepdims=True) + 1.0
            cnt = jnp.where(sel, cnt + 1.0, cnt)
            sv1n = jnp.sum(jnp.where(iota_kf == nxt, sv1, 0.0), axis=0, keepdims=True)
            sv1n = jnp.where(nxt >= float(K), neg, sv1n)
            sv0s = jnp.sum(jnp.where(sel, sv0, 0.0), axis=0, keepdims=True)
            front = jnp.where(sel, sv0s + sv1n, front)
            yield
        cntp = jnp.zeros((N_KEYS, cw), F32)
        for j in range(K):
            cntp = jnp.where(work0 == -(RANK_MARK + j * RANK_MARK_STEP), cnt[j:j + 1, :], cntp)
            if j % 4 == 3:
                yield
        cnt_ref[h, ch] = cntp
        p0_ref[h, ch] = jnp.exp(s0 - sv0[0:1, :]) * (GELU_IN_SCALE / zsum)

    def body(it, carry):
        units = [unit(2 * it), unit(2 * it + 1)]
        while units:
            for u in list(units):
                if next(u, "done") == "done":
                    units.remove(u)
        return carry

    lax.fori_loop(0, PEER_HEADS * nch // 2, body, 0)


def _peer_select(x1, g2, wq, sk, tq, cw):
    t = x1.shape[0]
    nch = tq // cw
    meta = lambda dt: jax.ShapeDtypeStruct((PEER_HEADS, t // cw, N_KEYS, cw), dt)
    mspec = pl.BlockSpec((PEER_HEADS, nch, N_KEYS, cw), lambda i: (0, i, 0, 0))
    return pl.pallas_call(
        functools.partial(_peer_select_kernel, tq=tq, cw=cw),
        grid=(t // tq,),
        in_specs=[
            pl.BlockSpec((tq, D_MODEL), lambda i: (i, 0)),
            pl.BlockSpec((1, D_MODEL), lambda i: (0, 0)),
            pl.BlockSpec((D_MODEL, PEER_HEADS * PEER_DQ), lambda i: (0, 0)),
            pl.BlockSpec((2, N_KEYS, PEER_DQ // 2), lambda i: (0, 0, 0)),
        ],
        out_specs=[pl.BlockSpec((tq, D_MODEL), lambda i: (i, 0)), mspec, mspec, mspec, mspec],
        out_shape=[jax.ShapeDtypeStruct((t, D_MODEL), BF16), meta(BF16), meta(BF16), meta(F32), meta(F32)],
        scratch_shapes=[pltpu.VMEM((2 * PEER_HEADS, nch, N_KEYS, cw), F32)],
        compiler_params=_params(("arbitrary",)),
        name="peer_select",
    )(x1, g2, wq, sk)


def _peer_dense_kernel(x_ref, h2_ref, rank_ref, p1_ref, cnt_ref, p0_ref, u_ref, vt_ref, fg_ref,
                       out_ref, acc_scr, g_scr, ht_scr, *, tb, eb, ec, cw, nblk_e, final_norm):
    ngrp = eb // ec
    j = pl.program_id(1)

    @pl.when(j == 0)
    def _():
        acc_scr[...] = jnp.zeros_like(acc_scr)

    for grp in range(ngrp):
        es = slice(grp * ec, (grp + 1) * ec)
        u_grp = pltpu.bitcast(u_ref[grp * (ec // 2):(grp + 1) * (ec // 2), :], BF16)
        ht_scr[grp] = lax.dot_general(u_grp, h2_ref[...], NT_DIMS, preferred_element_type=F32)
    for grp in range(ngrp):
        es = slice(grp * ec, (grp + 1) * ec)
        for k in range(ec // N_KEYS):
            r = grp * (ec // N_KEYS) + k
            ks = slice(grp * ec + k * N_KEYS, grp * ec + (k + 1) * N_KEYS)
            for ch in range(tb // cw):
                cs = slice(ch * cw, (ch + 1) * cw)
                def row8(ref, h):
                    parts = [jnp.broadcast_to(ref[h, ch, r:r + 1, lt * LANES:(lt + 1) * LANES], (8, LANES))
                             for lt in range(cw // LANES)]
                    return jnp.concatenate(parts, axis=1).astype(BF16)[None]

                w = jnp.zeros((N_KEYS // 8, 8, cw), BF16)
                for h in range(PEER_HEADS):
                    rk = rank_ref[h, ch].reshape(N_KEYS // 8, 8, cw)
                    p1 = p1_ref[h, ch].reshape(N_KEYS // 8, 8, cw)
                    w = w + jnp.where(rk < row8(cnt_ref, h), p1 * row8(p0_ref, h), jnp.zeros_like(w))
                z = ht_scr[grp, k * N_KEYS:(k + 1) * N_KEYS, cs]
                g_scr[ks, cs] = w.reshape(N_KEYS, cw) * (z * (1.0 + lax.erf(z))).astype(BF16)
        acc_scr[...] += jnp.dot(pltpu.bitcast(vt_ref[:, es], BF16), g_scr[es, :],
                                preferred_element_type=F32)

    @pl.when(j == nblk_e - 1)
    def _():
        y = x_ref[...] + acc_scr[...].T
        if final_norm:
            y = _rms(y, fg_ref[...])
        out_ref[...] = y


def _peer_dense(x1, h2, rank, p1, cnt, p0, u_bf, vt_bf, fg, tb, eb, cw, final_norm):
    t = x1.shape[0]
    nch = tb // cw
    nblk_e = N_EXPERTS // eb
    ec = min(eb, 512)
    mspec = pl.BlockSpec((PEER_HEADS, nch, N_KEYS, cw), lambda i, j: (0, i, 0, 0))
    rspec = pl.BlockSpec((PEER_HEADS, nch, eb // N_KEYS, cw), lambda i, j: (0, i, j, 0))
    return pl.pallas_call(
        functools.partial(_peer_dense_kernel, tb=tb, eb=eb, ec=ec, cw=cw, nblk_e=nblk_e,
                          final_norm=final_norm),
        grid=(t // tb, nblk_e),
        in_specs=[
            pl.BlockSpec((tb, D_MODEL), lambda i, j: (i, 0)),
            pl.BlockSpec((tb, D_MODEL), lambda i, j: (i, 0)),
            mspec, mspec, rspec, rspec,
            pl.BlockSpec((eb // 2, D_MODEL), lambda i, j: (j, 0)),
            pl.BlockSpec((D_MODEL // 2, eb), lambda i, j: (0, j)),
            pl.BlockSpec((1, D_MODEL), lambda i, j: (0, 0)),
        ],
        out_specs=pl.BlockSpec((tb, D_MODEL), lambda i, j: (i, 0)),
        out_shape=jax.ShapeDtypeStruct((t, D_MODEL), F32),
        scratch_shapes=[pltpu.VMEM((D_MODEL, tb), F32), pltpu.VMEM((eb, tb), BF16),
                        pltpu.VMEM((eb // ec, ec, tb), F32)],
        compiler_params=_params(("arbitrary", "arbitrary")),
        name="peer_dense",
    )(x1, h2, rank, p1, cnt, p0, u_bf, vt_bf, fg)


def _layer_weights(li, w_in, gla_w2, gla_b2, gla_norm_g, conv_w, conv_b, ml_i_b, ml_f_b, ml_norm_g,
                   w_out, peer_wq, peer_subkeys, peer_u, peer_v):
    w = w_in[li]
    o = 0
    segs = {}
    for name, width in (("aq", GLA_QK), ("ak", GLA_QK), ("av", D_MODEL), ("ag", D_MODEL),
                        ("alr", GLA_GATE_RANK), ("bq", ML_W), ("bk", ML_W), ("bv", ML_W),
                        ("bi", ML_HEADS), ("bf", ML_HEADS), ("bo", ML_W),
                        ("ga", D_MODEL), ("gb", D_MODEL)):
        segs[name] = w[:, o:o + width]
        o += width
    seg_w = WSEG_BLOCKS * D_MODEL
    starts = (0, seg_w + GLA_GATE_RANK, 2 * seg_w + GLA_GATE_RANK + 2 * ML_HEADS)
    w_segs = tuple(w[:, s:s + seg_w].astype(BF16) for s in starts)
    pad = lambda a: jnp.pad(a, ((0, 0), (0, LANES - a.shape[1])))
    w_small = jnp.concatenate([pad(segs["alr"]), pad(segs["bi"]), pad(segs["bf"])], axis=1).astype(BF16)
    w_gt = jnp.concatenate([segs["bi"], segs["bf"]], axis=1).T.astype(BF16)
    gate_b = jnp.concatenate([ml_i_b[li], ml_f_b[li]])
    u_packed, vt_packed = _table_prep(peer_u, peer_v, li)
    return dict(
        w_segs=w_segs, w_small=w_small, w_gt=w_gt,
        w2p=jnp.pad(gla_w2[li], ((0, LANES - GLA_GATE_RANK), (0, 0))),
        b2=gla_b2[li][None, :], gng=gla_norm_g[li][None, :],
        cw=conv_w[li], cb=conv_b[li][None, :],
        gbr=jnp.concatenate([jnp.pad(ml_i_b[li], (0, LANES - ML_HEADS)),
                             jnp.pad(ml_f_b[li], (0, LANES - ML_HEADS))])[None, :],
        gbc=gate_b[:, None], mng=ml_norm_g[li][None, :],
        wout=w_out[li].astype(BF16),
        wq=peer_wq[li].astype(BF16), sk=peer_subkeys[li].astype(BF16),
        u=u_packed, vt=vt_packed,
    )


def _table_prep_kernel(u_ref, v_ref, up_ref, vtp_ref):
    up_ref[...] = pltpu.bitcast((u_ref[0] * GELU_IN_SCALE).astype(BF16), jnp.uint32)
    vtp_ref[...] = pltpu.bitcast(v_ref[0].T.astype(BF16), jnp.uint32)


def _table_prep(u_tabs, v_tabs, li, eb=1024):
    _, ne, d = u_tabs.shape
    return pl.pallas_call(
        _table_prep_kernel,
        grid=(ne // eb,),
        in_specs=[pl.BlockSpec((1, eb, d), lambda i: (li, i, 0)), pl.BlockSpec((1, eb, d), lambda i: (li, i, 0))],
        out_specs=[pl.BlockSpec((eb // 2, d), lambda i: (i, 0)), pl.BlockSpec((d // 2, eb), lambda i: (0, i))],
        out_shape=[jax.ShapeDtypeStruct((ne // 2, d), jnp.uint32),
                   jax.ShapeDtypeStruct((d // 2, ne), jnp.uint32)],
        compiler_params=_params(("arbitrary",)),
        name="table_prep",
    )(u_tabs, v_tabs)


def _pick(n, cands):
    for c in cands:
        if n % c == 0:
            return c
    return n


def _peer(x1, g2, lw, fg, final_norm):
    t = x1.shape[0]
    tq = _pick(t, (512, 256, 128))
    tb = _pick(t, (512, 256, 128))
    cw = _pick(tb, (2 * LANES, LANES))
    h2, rank, p1, cnt, p0 = _peer_select(x1, g2, lw["wq"], lw["sk"], tq, cw)
    return _peer_dense(x1, h2, rank, p1, cnt, p0, lw["u"], lw["vt"], fg, tb, 2048, cw, final_norm)


def kernel(x_prompt, x_sample, state_gla, state_mlstm_c, state_mlstm_n, state_mlstm_m, state_conv,
           norm1_g, w_in, gla_w2, gla_b2, gla_norm_g, conv_w, conv_b, ml_i_b, ml_f_b, ml_norm_g,
           w_out, norm2_g, peer_wq, peer_subkeys, peer_u, peer_v, final_g):
    depth = w_in.shape[0]
    bsz, seq, _ = x_prompt.shape
    nb = x_sample.shape[0]
    assert seq % CHUNK == 0 and x_sample.shape[1] == 1 and nb % LANES == 0
    tp = bsz * seq
    tm = _pick(tp, (2048, 1024, 512, 256, 128))
    tc = _pick(seq, (256, 128, 64))
    npar = _pick(bsz, (2, 1))
    fg = final_g[None, :]

    xp = x_prompt.reshape(tp, D_MODEL)
    xs = x_sample.reshape(nb, D_MODEL)
    p_out = [[] for _ in range(5)]
    s_out = [[] for _ in range(5)]
    s_gla = s_c = None
    for li in range(depth):
        lw = _layer_weights(li, w_in, gla_w2, gla_b2, gla_norm_g, conv_w, conv_b, ml_i_b, ml_f_b,
                            ml_norm_g, w_out, peer_wq, peer_subkeys, peer_u, peer_v)
        g1 = norm1_g[li][None, :]
        g2 = norm2_g[li][None, :]
        last = li == depth - 1

        big, sm, gt = _inproj(xp, g1, lw["w_segs"], lw["w_small"], lw["w_gt"], tm)
        gtc = gt.reshape(8, bsz, seq // CHUNK, CHUNK).transpose(1, 2, 0, 3)
        x1, gla, c, n, m, conv = _mixer_prompt(
            xp.reshape(bsz, seq, D_MODEL), big.reshape(bsz, seq, BIG_COLS),
            sm.reshape(bsz, seq, SMALL_COLS), gtc, lw, bsz, seq, tc, npar)
        xp = _peer(x1.reshape(tp, D_MODEL), g2, lw, fg, last)
        for lst, val in zip(p_out, (gla, c, n, m[:, 0, :ML_HEADS], conv)):
            lst.append(val)

        big, sm, _ = _inproj(xs, g1, lw["w_segs"], lw["w_small"], lw["w_gt"], nb)
        x1, s_gla, s_c, n, m, conv = _mixer_sample(
            xs, big, sm, state_gla, state_mlstm_c,
            state_mlstm_n[li].reshape(nb, ML_W),
            jnp.pad(state_mlstm_m[li], ((0, 0), (0, LANES - ML_HEADS))),
            state_conv[li].reshape(nb, (CONV_W - 1) * 2 * ML_W), lw, li, s_gla, s_c)
        xs = _peer(x1, g2, lw, fg, last)
        for lst, val in zip(s_out[2:], (n.reshape(nb, ML_HEADS, ML_DH), m[:, :ML_HEADS],
                                        conv.reshape(nb, CONV_W - 1, 2 * ML_W))):
            lst.append(val)

    y_prompt = xp.reshape(bsz, seq, D_MODEL)
    y_sample = xs.reshape(nb, 1, D_MODEL)
    return (y_prompt, y_sample, *[jnp.stack(v) for v in p_out],
            s_gla, s_c, *[jnp.stack(v) for v in s_out[2:]])
```

```python
import functools

import jax
import jax.numpy as jnp
from jax import lax
from jax.experimental import pallas as pl
from jax.experimental.pallas import tpu as pltpu

F32 = jnp.float32
BF16 = jnp.bfloat16
HIGHEST = lax.Precision.HIGHEST

D_MODEL = 1024
GLA_HEADS = 4
GLA_DK = 128
GLA_DV = 256
GLA_GATE_RANK = 16
GLA_TAU = 16.0
GLA_QK = GLA_HEADS * GLA_DK
ML_HEADS = 4
ML_DH = 256
ML_W = ML_HEADS * ML_DH
CONV_W = 4
CHUNK = 64
PEER_HEADS = 8
PEER_DQ = 256
N_KEYS = 128
N_EXPERTS = N_KEYS * N_KEYS
PEER_TOPK = 16
EPS = 1e-6
GELU_IN_SCALE = 0.7071067811865476
RANK_MARK = 2.0 ** 100
RANK_MARK_STEP = 2.0 ** 96

LANES = 128
BIG_COLS = 9 * D_MODEL
WSEG_BLOCKS = 3
SAMPLES_PER_STEP = 4
HEADS_PER_ROUND = 4
SMALL_COLS = 3 * LANES
VMEM_LIMIT_BYTES = 56 * 1024 * 1024

NT_DIMS = (((1,), (1,)), ((), ()))
TN_DIMS = (((0,), (0,)), ((), ()))


def _params(sem, flags=None):
    return pltpu.CompilerParams(dimension_semantics=sem, vmem_limit_bytes=VMEM_LIMIT_BYTES, flags=flags)


def _log_sigmoid(x):
    return jnp.minimum(x, 0.0) - jnp.log1p(jnp.exp(-jnp.abs(x)))


def _sigmoid(x):
    return 1.0 / (1.0 + jnp.exp(-x))


def _silu(x):
    return x * _sigmoid(x)


def _rms(x, g):
    return x * lax.rsqrt(jnp.mean(x * x, axis=-1, keepdims=True) + EPS) * g


def _lane_col(a, j):
    lane = lax.broadcasted_iota(jnp.int32, a.shape, 1)
    return jnp.sum(jnp.where(lane == j, a, 0.0), axis=1, keepdims=True)


def _inproj_kernel(x_ref, g_ref, wa_ref, wb_ref, wc_ref, wsm_ref, wgt_ref, big_ref, sm_ref, gt_ref, h_scr):
    j = pl.program_id(1)

    @pl.when(j == 0)
    def _():
        hb = _rms(x_ref[...], g_ref[...]).astype(BF16)
        h_scr[...] = hb
        sm_ref[...] = jnp.dot(hb, wsm_ref[...], preferred_element_type=F32)
        gt_ref[...] = lax.dot_general(wgt_ref[...], hb, NT_DIMS, preferred_element_type=F32)

    for s, w_ref in enumerate((wa_ref, wb_ref, wc_ref)):
        @pl.when((j >= s * WSEG_BLOCKS) & (j < (s + 1) * WSEG_BLOCKS))
        def _(w_ref=w_ref):
            big_ref[...] = jnp.dot(h_scr[...], w_ref[...], preferred_element_type=F32).astype(BF16)


def _inproj(x, g, w_segs, w_small, w_gt, tm):
    t = x.shape[0]
    tn = D_MODEL

    def wspec(s):
        return pl.BlockSpec((D_MODEL, tn), lambda i, j: (0, jnp.clip(j - s * WSEG_BLOCKS, 0, WSEG_BLOCKS - 1)))

    return pl.pallas_call(
        _inproj_kernel,
        grid=(t // tm, BIG_COLS // tn),
        in_specs=[
            pl.BlockSpec((tm, D_MODEL), lambda i, j: (i, 0)),
            pl.BlockSpec((1, D_MODEL), lambda i, j: (0, 0)),
            wspec(0), wspec(1), wspec(2),
            pl.BlockSpec((D_MODEL, SMALL_COLS), lambda i, j: (0, 0)),
            pl.BlockSpec((8, D_MODEL), lambda i, j: (0, 0)),
        ],
        out_specs=[
            pl.BlockSpec((tm, tn), lambda i, j: (i, j)),
            pl.BlockSpec((tm, SMALL_COLS), lambda i, j: (i, 0)),
            pl.BlockSpec((8, tm), lambda i, j: (0, i)),
        ],
        out_shape=[
            jax.ShapeDtypeStruct((t, BIG_COLS), BF16),
            jax.ShapeDtypeStruct((t, SMALL_COLS), F32),
            jax.ShapeDtypeStruct((8, t), F32),
        ],
        scratch_shapes=[pltpu.VMEM((tm, D_MODEL), BF16)],
        compiler_params=_params(("arbitrary", "arbitrary")),
        name="inproj",
    )(x, g, *w_segs, w_small, w_gt)


def _headnorm(o, g_row):
    return o * lax.rsqrt(jnp.mean(o * o, axis=-1, keepdims=True) + EPS) * g_row


def _mixer_prompt_kernel(x_ref, aq_ref, ak_ref, av_ref, ag_ref, bq_ref, bk_ref, bv_ref, bo_ref,
                         ga_ref, gb_ref, sm_ref, gt_ref, w2_ref, b2_ref, gng_ref, cw_ref, cb_ref,
                         gbr_ref, gbc_ref, mng_ref, wout_ref,
                         x1_ref, gla_ref, c_ref, n_ref, m_ref, conv_ref,
                         st_scr, xp_scr, qk_scr, mix_scr, *, tc, nblk, npar):
    i = pl.program_id(1)
    L = CHUNK

    @pl.when(i == 0)
    def _init():
        st_scr[...] = jnp.zeros_like(st_scr)
        c_ref[...] = jnp.zeros_like(c_ref)
        n_ref[...] = jnp.zeros_like(n_ref)
        m_ref[...] = jnp.zeros_like(m_ref)
        xp_scr[:, 0:8, :] = jnp.zeros((npar, 8, 2 * ML_W), F32)

    for p in range(npar):
        xp_scr[p, 8:8 + tc, 0:ML_W] = bq_ref[p].astype(F32)
        xp_scr[p, 8:8 + tc, ML_W:2 * ML_W] = bk_ref[p].astype(F32)
        y = cb_ref[...]
        for j in range(CONV_W):
            y = y + cw_ref[j:j + 1, :] * xp_scr[p, 5 + j:5 + j + tc, :]
        qk = _silu(y)
        qk_scr[p, :, 0:ML_W] = qk[:, 0:ML_W].astype(BF16)
        qk_scr[p, :, ML_W:2 * ML_W] = (qk[:, ML_W:2 * ML_W] * (ML_DH ** -0.5)).astype(BF16)
        xp_scr[p, 5:8, :] = xp_scr[p, tc + 5:tc + 8, :]

    @pl.when(i == nblk - 1)
    def _():
        conv_ref[...] = xp_scr[:, 5:8, :]

    row = lax.broadcasted_iota(jnp.int32, (L, L), 0)
    col = lax.broadcasted_iota(jnp.int32, (L, L), 1)
    causal = col <= row
    tril = causal.astype(F32)
    triu = (row <= col).astype(F32)
    lane_row = lax.broadcasted_iota(jnp.int32, (1, LANES), 1)

    def gates(p, c, rows):
        sm = sm_ref[p, rows, :]
        z = jnp.dot(sm[:, 0:LANES], w2_ref[...], precision=HIGHEST, preferred_element_type=F32) + b2_ref[...]
        log_a = _log_sigmoid(z) * (1.0 / GLA_TAU)
        bc = jnp.dot(tril, log_a, precision=HIGHEST, preferred_element_type=F32)
        li_c_all = sm[:, LANES:2 * LANES] + gbr_ref[:, 0:LANES]
        lf_c_all = _log_sigmoid(sm[:, 2 * LANES:3 * LANES] + gbr_ref[:, LANES:2 * LANES])
        bcum_c_all = jnp.dot(tril, lf_c_all, precision=HIGHEST, preferred_element_type=F32)
        gtb = gt_ref[p, c] + gbc_ref[...]
        bcum_r_all = jnp.dot(_log_sigmoid(gtb), triu, precision=HIGHEST, preferred_element_type=F32)
        return bc, li_c_all, bcum_c_all, gtb, bcum_r_all

    def head(p, h, rows, shared):
        bc, li_c_all, bcum_c_all, gtb, bcum_r_all = shared
        sl = slice(h * GLA_DK, (h + 1) * GLA_DK)
        vl = slice(h * GLA_DV, (h + 1) * GLA_DV)
        b = bc[:, sl]
        mid = b[L // 2 - 1:L // 2, :]
        bl = b[L - 1:L, :]
        q = aq_ref[p, rows, sl].astype(F32) * (GLA_DK ** -0.5)
        k = ak_ref[p, rows, sl].astype(F32)
        v = av_ref[p, rows, vl]
        q_in = (q * jnp.exp(b)).astype(BF16)
        q_at = (q * jnp.exp(b - mid)).astype(BF16)
        yield
        k_at = (k * jnp.exp(mid - b)).astype(BF16)
        k_out = (k * jnp.exp(bl - b)).astype(BF16)
        yield
        att = lax.dot_general(q_at, k_at, NT_DIMS, preferred_element_type=F32)
        att = jnp.where(causal, att, 0.0)
        st = st_scr[p, h]
        yield
        o = lax.dot_general(q_in, st.astype(BF16), NT_DIMS, preferred_element_type=F32)
        yield
        o = o + jnp.dot(att.astype(BF16), v, preferred_element_type=F32)
        yield
        st_scr[p, h] = st * jnp.exp(bl) + lax.dot_general(v, k_out, TN_DIMS, preferred_element_type=F32)
        yield
        o_a = _headnorm(o, gng_ref[:, vl]) * _silu(ag_ref[p, rows, vl].astype(F32))
        yield

        li_c = _lane_col(li_c_all, h)
        b_c = _lane_col(bcum_c_all, h)
        li_r = gtb[h:h + 1, :]
        b_r = bcum_r_all[ML_HEADS + h:ML_HEADS + h + 1, :]
        m_prev = jnp.sum(jnp.where(lane_row == h, m_ref[p], 0.0), axis=1, keepdims=True)
        a_int = b_c + m_prev
        dm = jnp.where(causal, b_c - b_r + li_r, -jnp.inf)
        m_t = jnp.maximum(a_int, jnp.max(dm, axis=1, keepdims=True))
        yield
        w_int = jnp.exp(a_int - m_t)
        dexp = jnp.exp(dm - m_t)
        qb = qk_scr[p, rows, vl]
        kb = qk_scr[p, rows, ML_W + h * ML_DH:ML_W + (h + 1) * ML_DH]
        vb = bv_ref[p, rows, vl]
        s = lax.dot_general(qb, kb, NT_DIMS, preferred_element_type=F32) * dexp
        yield
        cst = c_ref[p, h]
        num = w_int * jnp.dot(qb, cst.astype(BF16), preferred_element_type=F32)
        yield
        num = num + jnp.dot(s.astype(BF16), vb, preferred_element_type=F32)
        nrow = n_ref[p, h:h + 1, :]
        den = (w_int * jnp.sum(qb.astype(F32) * nrow, axis=1, keepdims=True)
               + jnp.sum(s, axis=1, keepdims=True))
        yield
        hh = num / jnp.maximum(jnp.abs(den), jnp.exp(-m_t))
        b_last = b_c[L - 1:L, :]
        g_c = b_last - b_c + li_c
        m_new = jnp.maximum(b_last + m_prev, jnp.max(g_c, axis=0, keepdims=True))
        f_c = jnp.exp(b_last + m_prev - m_new)
        w_s = jnp.exp(g_c - m_new)
        kw = kb.astype(F32) * w_s
        yield
        c_ref[p, h] = f_c * cst + lax.dot_general(kw.astype(BF16), vb, TN_DIMS, preferred_element_type=F32)
        n_ref[p, h:h + 1, :] = f_c * nrow + jnp.sum(kw, axis=0, keepdims=True)
        m_ref[p] = jnp.where(lane_row == h, m_new, m_ref[p])
        yield
        o_b = _headnorm(hh, mng_ref[:, vl]) * _sigmoid(bo_ref[p, rows, vl].astype(F32))
        yield
        mix = (_sigmoid(ga_ref[p, rows, vl].astype(F32)) * o_a
               + _sigmoid(gb_ref[p, rows, vl].astype(F32)) * o_b)
        mix_scr[p, rows, vl] = mix.astype(BF16)

    def chunk(c, carry):
        rows = pl.ds(pl.multiple_of(c * L, L), L)
        shared = [gates(p, c, rows) for p in range(npar)]
        for h0 in range(0, GLA_HEADS, HEADS_PER_ROUND):
            units = [head(p, h, rows, shared[p]) for h in range(h0, h0 + HEADS_PER_ROUND) for p in range(npar)]
            while units:
                for u in list(units):
                    if next(u, "done") == "done":
                        units.remove(u)
        return carry

    lax.fori_loop(0, tc // L, chunk, 0)

    for p in range(npar):
        x1_ref[p] = x_ref[p] + jnp.dot(mix_scr[p], wout_ref[...], preferred_element_type=F32)

    @pl.when(i == nblk - 1)
    def _fin():
        for p in range(npar):
            for h in range(GLA_HEADS):
                gla_ref[p, h] = st_scr[p, h].T


def _mixer_prompt(x, big, sm, gtc, lw, bsz, seq, tc, npar):
    nblk = seq // tc

    def seg(width, idx):
        return pl.BlockSpec((npar, tc, width), lambda b, i: (b, i, idx))

    const = lambda shape: pl.BlockSpec(shape, lambda b, i: tuple(0 for _ in shape))
    in_specs = [
        seg(D_MODEL, 0),
        seg(GLA_QK, 0), seg(GLA_QK, 1),
        seg(D_MODEL, 1), seg(D_MODEL, 2),
        seg(D_MODEL, 3), seg(D_MODEL, 4), seg(D_MODEL, 5),
        seg(D_MODEL, 6), seg(D_MODEL, 7), seg(D_MODEL, 8),
        seg(SMALL_COLS, 0),
        pl.BlockSpec((npar, tc // CHUNK, 8, CHUNK), lambda b, i: (b, i, 0, 0)),
        const((LANES, GLA_QK)), const((1, GLA_QK)), const((1, D_MODEL)),
        const((CONV_W, 2 * ML_W)), const((1, 2 * ML_W)),
        const((1, 2 * LANES)), const((8, 1)), const((1, D_MODEL)),
        const((D_MODEL, D_MODEL)),
    ]
    out_specs = [
        seg(D_MODEL, 0),
        pl.BlockSpec((npar, GLA_HEADS, GLA_DK, GLA_DV), lambda b, i: (b, 0, 0, 0)),
        pl.BlockSpec((npar, ML_HEADS, ML_DH, ML_DH), lambda b, i: (b, 0, 0, 0)),
        pl.BlockSpec((npar, ML_HEADS, ML_DH), lambda b, i: (b, 0, 0)),
        pl.BlockSpec((npar, 1, LANES), lambda b, i: (b, 0, 0)),
        pl.BlockSpec((npar, CONV_W - 1, 2 * ML_W), lambda b, i: (b, 0, 0)),
    ]
    out_shape = [
        jax.ShapeDtypeStruct((bsz, seq, D_MODEL), F32),
        jax.ShapeDtypeStruct((bsz, GLA_HEADS, GLA_DK, GLA_DV), F32),
        jax.ShapeDtypeStruct((bsz, ML_HEADS, ML_DH, ML_DH), F32),
        jax.ShapeDtypeStruct((bsz, ML_HEADS, ML_DH), F32),
        jax.ShapeDtypeStruct((bsz, 1, LANES), F32),
        jax.ShapeDtypeStruct((bsz, CONV_W - 1, 2 * ML_W), F32),
    ]
    scratch = [
        pltpu.VMEM((npar, GLA_HEADS, GLA_DV, GLA_DK), F32),
        pltpu.VMEM((npar, tc + 8, 2 * ML_W), F32),
        pltpu.VMEM((npar, tc, 2 * ML_W), BF16),
        pltpu.VMEM((npar, tc, D_MODEL), BF16),
    ]
    return pl.pallas_call(
        functools.partial(_mixer_prompt_kernel, tc=tc, nblk=nblk, npar=npar),
        grid=(bsz // npar, nblk),
        in_specs=in_specs, out_specs=out_specs, out_shape=out_shape,
        scratch_shapes=scratch,
        compiler_params=_params(("arbitrary", "arbitrary")),
        name="mixer_prompt",
    )(x, big, big, big, big, big, big, big, big, big, big, sm, gtc,
      lw["w2p"], lw["b2"], lw["gng"], lw["cw"], lw["cb"], lw["gbr"], lw["gbc"], lw["mng"], lw["wout"])


def _col_bcast(row):
    return jnp.broadcast_to(row, (LANES, LANES)).T


def _mixer_sample_kernel(*refs, nb, has_prev, sps):
    n_in = 18 if has_prev else 16
    (x_ref, big_ref, sm_ref, gla_in, c_in, n_ref, m_ref, conv_ref,
     w2_ref, b2_ref, gng_ref, cw_ref, cb_ref, gbr_ref, mng_ref, wout_ref) = refs[:16]
    x1_ref, gla_out, c_out, n_o, m_o, conv_o = refs[n_in:n_in + 6]
    a_s, qa_s, ka_s, va_s, kw_s, qb_s, vb_s, fc_s, den_s, oa_s, num_s = refs[n_in + 6:]
    gla_ref, c_ref, gla_o, c_o = gla_in.at[0], c_in.at[0], gla_out.at[0], c_out.at[0]
    b = pl.program_id(0)
    seg = lambda idx: slice(idx * D_MODEL, (idx + 1) * D_MODEL)
    lane = lax.broadcasted_iota(jnp.int32, (nb, LANES), 1)

    @pl.when(b == 0)
    def _prep():
        sm = sm_ref[...]
        z = jnp.dot(sm[:, 0:LANES], w2_ref[...], precision=HIGHEST, preferred_element_type=F32) + b2_ref[...]
        a_s[...] = jnp.exp(_log_sigmoid(z) * (1.0 / GLA_TAU))
        qa_s[...] = big_ref[:, 0:GLA_QK].astype(F32) * (GLA_DK ** -0.5)
        ka_s[...] = big_ref[:, GLA_QK:2 * GLA_QK].astype(F32)
        va_s[...] = big_ref[:, seg(1)].astype(F32)
        vb_s[...] = big_ref[:, seg(5)].astype(F32)
        xq = big_ref[:, seg(3)].astype(F32)
        xk = big_ref[:, seg(4)].astype(F32)
        w = 2 * ML_W
        y = cb_ref[...]
        for j in range(CONV_W - 1):
            y = y + cw_ref[j:j + 1, :] * conv_ref[:, j * w:(j + 1) * w]
        y = y + cw_ref[CONV_W - 1:CONV_W, :] * jnp.concatenate([xq, xk], axis=1)
        qk = _silu(y)
        qb = qk[:, 0:ML_W]
        kb = qk[:, ML_W:w] * (ML_DH ** -0.5)
        qb_s[...] = qb
        conv_o[:, 0:2 * w] = conv_ref[:, w:3 * w]
        conv_o[:, 2 * w:2 * w + ML_W] = xq
        conv_o[:, 2 * w + ML_W:3 * w] = xk

        li = sm[:, LANES:2 * LANES] + gbr_ref[:, 0:LANES]
        lf = _log_sigmoid(sm[:, 2 * LANES:3 * LANES] + gbr_ref[:, LANES:2 * LANES])
        m_old = m_ref[...]
        m_new = jnp.maximum(lf + m_old, li)
        f_c = jnp.exp(lf + m_old - m_new)
        w_s = jnp.exp(li - m_new)
        m_o[...] = m_new
        fc_s[...] = f_c
        den = jnp.zeros((nb, LANES), F32)
        for h in range(ML_HEADS):
            vl = slice(h * ML_DH, (h + 1) * ML_DH)
            kw = kb[:, vl] * _lane_col(w_s, h)
            kw_s[:, vl] = kw
            n_new = _lane_col(f_c, h) * n_ref[:, vl] + kw
            n_o[:, vl] = n_new
            den = jnp.where(lane == h, jnp.sum(qb[:, vl] * n_new, axis=1, keepdims=True), den)
        den_s[...] = den
        oa_s[...] = jnp.zeros_like(oa_s)
        num_s[...] = jnp.zeros_like(num_s)

    tile = pl.ds(pl.multiple_of(lax.shift_right_logical(b * sps, 3) * 8, 8), 8)
    lane_row = lax.broadcasted_iota(jnp.int32, (1, LANES), 1)

    for s in range(sps):
        sub = jnp.bitwise_and(b * sps + s, 7)

        def row_get(ref, cols, sub=sub):
            t = ref[tile, cols]
            pick = lax.broadcasted_iota(jnp.int32, t.shape, 0) == sub
            return jnp.sum(jnp.where(pick, t, 0.0), axis=0, keepdims=True)

        def row_set(ref, cols, val, sub=sub):
            t = ref[tile, cols]
            pick = lax.broadcasted_iota(jnp.int32, t.shape, 0) == sub
            ref[tile, cols] = jnp.where(pick, val, t)

        fc_row = row_get(fc_s, slice(0, LANES))
        for h in range(GLA_HEADS):
            sl = slice(h * GLA_DK, (h + 1) * GLA_DK)
            vl = slice(h * GLA_DV, (h + 1) * GLA_DV)
            a_c = _col_bcast(row_get(a_s, sl))
            k_c = _col_bcast(row_get(ka_s, sl))
            q_c = _col_bcast(row_get(qa_s, sl))
            v_row = row_get(va_s, vl)
            s_old = gla_ref[s, h]
            halves = []
            for p in range(GLA_DV // LANES):
                ls = slice(p * LANES, (p + 1) * LANES)
                s_new = s_old[:, ls] * a_c + k_c * v_row[:, ls]
                gla_o[s, h, :, ls] = s_new
                halves.append(jnp.sum(q_c * s_new, axis=0, keepdims=True))
            row_set(oa_s, vl, jnp.concatenate(halves, axis=1))

            f_c = jnp.sum(jnp.where(lane_row == h, fc_row, 0.0), axis=1, keepdims=True)
            vb_row = row_get(vb_s, vl)
            acc = [jnp.zeros((1, LANES), F32) for _ in range(ML_DH // LANES)]
            for r in range(ML_DH // LANES):
                rs = slice(r * LANES, (r + 1) * LANES)
                ks = slice(h * ML_DH + r * LANES, h * ML_DH + (r + 1) * LANES)
                kw_c = _col_bcast(row_get(kw_s, ks))
                qb_c = _col_bcast(row_get(qb_s, ks))
                for p in range(ML_DH // LANES):
                    ls = slice(p * LANES, (p + 1) * LANES)
                    c_new = f_c * c_ref[s, h, rs, ls] + kw_c * vb_row[:, ls]
                    c_o[s, h, rs, ls] = c_new
                    acc[p] = acc[p] + jnp.sum(qb_c * c_new, axis=0, keepdims=True)
            row_set(num_s, vl, jnp.concatenate(acc, axis=1))

    @pl.when(b == nb // sps - 1)
    def _post():
        m_new = m_o[...]
        den = den_s[...]
        for h in range(GLA_HEADS):
            vl = slice(h * GLA_DV, (h + 1) * GLA_DV)
            o_a = _headnorm(oa_s[:, vl], gng_ref[:, vl]) * _silu(big_ref[:, D_MODEL * 2 + h * GLA_DV:
                                                                       D_MODEL * 2 + (h + 1) * GLA_DV].astype(F32))
            dn = jnp.maximum(jnp.abs(_lane_col(den, h)), jnp.exp(-_lane_col(m_new, h)))
            hh = num_s[:, vl] / dn
            o_b = _headnorm(hh, mng_ref[:, vl]) * _sigmoid(
                big_ref[:, 6 * D_MODEL + h * ML_DH:6 * D_MODEL + (h + 1) * ML_DH].astype(F32))
            ga = big_ref[:, 7 * D_MODEL + h * ML_DH:7 * D_MODEL + (h + 1) * ML_DH].astype(F32)
            gb = big_ref[:, 8 * D_MODEL + h * ML_DH:8 * D_MODEL + (h + 1) * ML_DH].astype(F32)
            oa_s[:, vl] = _sigmoid(ga) * o_a + _sigmoid(gb) * o_b
        x1_ref[...] = x_ref[...] + jnp.dot(oa_s[...].astype(BF16), wout_ref[...], preferred_element_type=F32)


def _mixer_sample(x, big, sm, st_gla, st_c, st_n, st_m, st_conv, lw, li, prev_gla, prev_c):
    nb = x.shape[0]
    has_prev = prev_gla is not None
    full = lambda shape: pl.BlockSpec(shape, lambda b: tuple(0 for _ in shape))
    sps = SAMPLES_PER_STEP
    gla_spec = pl.BlockSpec((1, sps, GLA_HEADS, GLA_DK, GLA_DV), lambda b: (li, b, 0, 0, 0))
    c_spec = pl.BlockSpec((1, sps, ML_HEADS, ML_DH, ML_DH), lambda b: (li, b, 0, 0, 0))
    in_specs = [
        full((nb, D_MODEL)), full((nb, BIG_COLS)), full((nb, SMALL_COLS)),
        gla_spec, c_spec,
        full((nb, ML_W)), full((nb, LANES)), full((nb, 3 * 2 * ML_W)),
        full((LANES, GLA_QK)), full((1, GLA_QK)), full((1, D_MODEL)),
        full((CONV_W, 2 * ML_W)), full((1, 2 * ML_W)), full((1, 2 * LANES)), full((1, D_MODEL)),
        full((D_MODEL, D_MODEL)),
    ]
    operands = [x, big, sm, st_gla, st_c, st_n, st_m, st_conv,
                lw["w2p"], lw["b2"], lw["gng"], lw["cw"], lw["cb"], lw["gbr"], lw["mng"], lw["wout"]]
    aliases = {}
    if has_prev:
        in_specs += [pl.BlockSpec(memory_space=pl.ANY), pl.BlockSpec(memory_space=pl.ANY)]
        operands += [prev_gla, prev_c]
        aliases = {16: 1, 17: 2}
    out_specs = [
        full((nb, D_MODEL)),
        gla_spec, c_spec,
        full((nb, ML_W)), full((nb, LANES)), full((nb, 3 * 2 * ML_W)),
    ]
    out_shape = [
        jax.ShapeDtypeStruct((nb, D_MODEL), F32),
        jax.ShapeDtypeStruct(st_gla.shape, F32),
        jax.ShapeDtypeStruct(st_c.shape, F32),
        jax.ShapeDtypeStruct((nb, ML_W), F32),
        jax.ShapeDtypeStruct((nb, LANES), F32),
        jax.ShapeDtypeStruct((nb, 3 * 2 * ML_W), F32),
    ]
    vm = lambda cols: pltpu.VMEM((nb, cols), F32)
    scratch = [vm(GLA_QK), vm(GLA_QK), vm(GLA_QK), vm(D_MODEL), vm(ML_W), vm(ML_W), vm(ML_W),
               vm(LANES), vm(LANES), vm(D_MODEL), vm(ML_W)]
    return pl.pallas_call(
        functools.partial(_mixer_sample_kernel, nb=nb, has_prev=has_prev, sps=sps),
        grid=(nb // sps,),
        in_specs=in_specs, out_specs=out_specs, out_shape=out_shape,
        scratch_shapes=scratch,
        input_output_aliases=aliases,
        compiler_params=_params(("arbitrary",)),
        name="mixer_sample",
    )(*operands)


def _peer_select_kernel(x_ref, g_ref, wq_ref, sk_ref, h2_ref, rank_ref, p1_ref, cnt_ref, p0_ref,
                        sc_scr, *, tq, cw):
    nch = tq // cw
    hf = _rms(x_ref[...], g_ref[...])
    hb = hf.astype(BF16)
    h2_ref[...] = hf.T.astype(BF16)
    q = jnp.dot(hb, wq_ref[...], preferred_element_type=F32).astype(BF16)
    for hc in range(2 * PEER_HEADS):
        for ch in range(nch):
            qs = q[ch * cw:(ch + 1) * cw, hc * LANES:(hc + 1) * LANES]
            sc_scr[hc, ch] = lax.dot_general(sk_ref[hc % 2], qs, NT_DIMS, preferred_element_type=F32)

    K = PEER_TOPK
    neg = -jnp.inf
    iota_k = lax.broadcasted_iota(jnp.int32, (K, cw), 0)
    iota_kf = iota_k.astype(F32)

    def unit(idx):
        h = idx // nch
        ch = idx % nch
        s0 = sc_scr[2 * h, ch]
        s1 = sc_scr[2 * h + 1, ch]

        sv0 = jnp.zeros((K, cw), F32)
        sv1 = jnp.zeros((K, cw), F32)
        work0, work1 = s0, s1
        for j in range(K):
            marker = -(RANK_MARK + j * RANK_MARK_STEP)
            mx0 = jnp.max(work0, axis=0, keepdims=True)
            work0 = jnp.where(work0 == mx0, marker, work0)
            sv0 = jnp.where(iota_k == j, mx0, sv0)
            yield
            mx1 = jnp.max(work1, axis=0, keepdims=True)
            work1 = jnp.where(work1 == mx1, marker, work1)
            sv1 = jnp.where(iota_k == j, mx1, sv1)
            yield
        rank1 = jnp.where(work1 <= -RANK_MARK, (-RANK_MARK - work1) * (1.0 / RANK_MARK_STEP), float(K))
        rank_ref[h, ch] = rank1.astype(BF16)
        p1_ref[h, ch] = jnp.exp(s1 - sv1[0:1, :]).astype(BF16)
        yield

        top = sv0[0:1, :] + sv1[0:1, :]
        front = sv0 + sv1[0:1, :]
        cnt = jnp.zeros((K, cw), F32)
        zsum = jnp.zeros((1, cw), F32)
        for _ in range(K):
            fm = jnp.max(front, axis=0, keepdims=True)
            j1 = jnp.min(jnp.where(front == fm, iota_kf, float(K)), axis=0, keepdims=True)
            sel = iota_kf == j1
            zsum = zsum + jnp.exp(fm - top)
            nxt = jnp.sum(jnp.where(sel, cnt, 0.0), axis=0, keepdims=True) + 1.0
            cnt = jnp.where(sel, cnt + 1.0, cnt)
            sv1n = jnp.sum(jnp.where(iota_kf == nxt, sv1, 0.0), axis=0, keepdims=True)
            sv1n = jnp.where(nxt >= float(K), neg, sv1n)
            sv0s = jnp.sum(jnp.where(sel, sv0, 0.0), axis=0, keepdims=True)
            front = jnp.where(sel, sv0s + sv1n, front)
            yield
        cntp = jnp.zeros((N_KEYS, cw), F32)
        for j in range(K):
            cntp = jnp.where(work0 == -(RANK_MARK + j * RANK_MARK_STEP), cnt[j:j + 1, :], cntp)
            if j % 4 == 3:
                yield
        cnt_ref[h, ch] = cntp
        p0_ref[h, ch] = jnp.exp(s0 - sv0[0:1, :]) * (GELU_IN_SCALE / zsum)

    def body(it, carry):
        units = [unit(2 * it), unit(2 * it + 1)]
        while units:
            for u in list(units):
                if next(u, "done") == "done":
                    units.remove(u)
        return carry

    lax.fori_loop(0, PEER_HEADS * nch // 2, body, 0)


def _peer_select(x1, g2, wq, sk, tq, cw):
    t = x1.shape[0]
    nch = tq // cw
    meta = lambda dt: jax.ShapeDtypeStruct((PEER_HEADS, t // cw, N_KEYS, cw), dt)
    mspec = pl.BlockSpec((PEER_HEADS, nch, N_KEYS, cw), lambda i: (0, i, 0, 0))
    return pl.pallas_call(
        functools.partial(_peer_select_kernel, tq=tq, cw=cw),
        grid=(t // tq,),
        in_specs=[
            pl.BlockSpec((tq, D_MODEL), lambda i: (i, 0)),
            pl.BlockSpec((1, D_MODEL), lambda i: (0, 0)),
            pl.BlockSpec((D_MODEL, PEER_HEADS * PEER_DQ), lambda i: (0, 0)),
            pl.BlockSpec((2, N_KEYS, PEER_DQ // 2), lambda i: (0, 0, 0)),
        ],
        out_specs=[pl.BlockSpec((D_MODEL, tq), lambda i: (0, i)), mspec, mspec, mspec, mspec],
        out_shape=[jax.ShapeDtypeStruct((D_MODEL, t), BF16), meta(BF16), meta(BF16), meta(F32), meta(F32)],
        scratch_shapes=[pltpu.VMEM((2 * PEER_HEADS, nch, N_KEYS, cw), F32)],
        compiler_params=_params(("arbitrary",)),
        name="peer_select",
    )(x1, g2, wq, sk)


def _peer_dense_kernel(x_ref, h2_ref, rank_ref, p1_ref, cnt_ref, p0_ref, u_ref, vt_ref, fg_ref,
                       out_ref, acc_scr, g_scr, ht_scr, *, tb, eb, ec, cw, nblk_e, final_norm):
    ngrp = eb // ec
    j = pl.program_id(1)

    @pl.when(j == 0)
    def _():
        acc_scr[...] = jnp.zeros_like(acc_scr)

    for grp in range(ngrp):
        es = slice(grp * ec, (grp + 1) * ec)
        u_grp = pltpu.bitcast(u_ref[grp * (ec // 2):(grp + 1) * (ec // 2), :], BF16)
        ht_scr[grp] = jnp.dot(u_grp, h2_ref[...], preferred_element_type=F32)
    for grp in range(ngrp):
        es = slice(grp * ec, (grp + 1) * ec)
        for k in range(ec // N_KEYS):
            r = grp * (ec // N_KEYS) + k
            ks = slice(grp * ec + k * N_KEYS, grp * ec + (k + 1) * N_KEYS)
            for ch in range(tb // cw):
                cs = slice(ch * cw, (ch + 1) * cw)
                def row8(ref, h):
                    parts = [jnp.broadcast_to(ref[h, ch, r:r + 1, lt * LANES:(lt + 1) * LANES], (8, LANES))
                             for lt in range(cw // LANES)]
                    return jnp.concatenate(parts, axis=1).astype(BF16)[None]

                w = jnp.zeros((N_KEYS // 8, 8, cw), BF16)
                for h in range(PEER_HEADS):
                    rk = rank_ref[h, ch].reshape(N_KEYS // 8, 8, cw)
                    p1 = p1_ref[h, ch].reshape(N_KEYS // 8, 8, cw)
                    w = w + jnp.where(rk < row8(cnt_ref, h), p1 * row8(p0_ref, h), jnp.zeros_like(w))
                z = ht_scr[grp, k * N_KEYS:(k + 1) * N_KEYS, cs]
                g_scr[ks, cs] = w.reshape(N_KEYS, cw) * (z * (1.0 + lax.erf(z))).astype(BF16)
        acc_scr[...] += jnp.dot(pltpu.bitcast(vt_ref[:, es], BF16), g_scr[es, :],
                                preferred_element_type=F32)

    @pl.when(j == nblk_e - 1)
    def _():
        y = x_ref[...] + acc_scr[...].T
        if final_norm:
            y = _rms(y, fg_ref[...])
        out_ref[...] = y


def _peer_dense(x1, h2, rank, p1, cnt, p0, u_bf, vt_bf, fg, tb, eb, cw, final_norm):
    t = x1.shape[0]
    nch = tb // cw
    nblk_e = N_EXPERTS // eb
    ec = min(eb, 512)
    mspec = pl.BlockSpec((PEER_HEADS, nch, N_KEYS, cw), lambda i, j: (0, i, 0, 0))
    rspec = pl.BlockSpec((PEER_HEADS, nch, eb // N_KEYS, cw), lambda i, j: (0, i, j, 0))
    return pl.pallas_call(
        functools.partial(_peer_dense_kernel, tb=tb, eb=eb, ec=ec, cw=cw, nblk_e=nblk_e,
                          final_norm=final_norm),
        grid=(t // tb, nblk_e),
        in_specs=[
            pl.BlockSpec((tb, D_MODEL), lambda i, j: (i, 0)),
            pl.BlockSpec((D_MODEL, tb), lambda i, j: (0, i)),
            mspec, mspec, rspec, rspec,
            pl.BlockSpec((eb // 2, D_MODEL), lambda i, j: (j, 0)),
            pl.BlockSpec((D_MODEL // 2, eb), lambda i, j: (0, j)),
            pl.BlockSpec((1, D_MODEL), lambda i, j: (0, 0)),
        ],
        out_specs=pl.BlockSpec((tb, D_MODEL), lambda i, j: (i, 0)),
        out_shape=jax.ShapeDtypeStruct((t, D_MODEL), F32),
        scratch_shapes=[pltpu.VMEM((D_MODEL, tb), F32), pltpu.VMEM((eb, tb), BF16),
                        pltpu.VMEM((eb // ec, ec, tb), F32)],
        compiler_params=_params(("arbitrary", "arbitrary")),
        name="peer_dense",
    )(x1, h2, rank, p1, cnt, p0, u_bf, vt_bf, fg)


def _layer_weights(li, w_in, gla_w2, gla_b2, gla_norm_g, conv_w, conv_b, ml_i_b, ml_f_b, ml_norm_g,
                   w_out, peer_wq, peer_subkeys, peer_u, peer_v):
    w = w_in[li]
    o = 0
    segs = {}
    for name, width in (("aq", GLA_QK), ("ak", GLA_QK), ("av", D_MODEL), ("ag", D_MODEL),
                        ("alr", GLA_GATE_RANK), ("bq", ML_W), ("bk", ML_W), ("bv", ML_W),
                        ("bi", ML_HEADS), ("bf", ML_HEADS), ("bo", ML_W),
                        ("ga", D_MODEL), ("gb", D_MODEL)):
        segs[name] = w[:, o:o + width]
        o += width
    seg_w = WSEG_BLOCKS * D_MODEL
    starts = (0, seg_w + GLA_GATE_RANK, 2 * seg_w + GLA_GATE_RANK + 2 * ML_HEADS)
    w_segs = tuple(w[:, s:s + seg_w].astype(BF16) for s in starts)
    pad = lambda a: jnp.pad(a, ((0, 0), (0, LANES - a.shape[1])))
    w_small = jnp.concatenate([pad(segs["alr"]), pad(segs["bi"]), pad(segs["bf"])], axis=1).astype(BF16)
    w_gt = jnp.concatenate([segs["bi"], segs["bf"]], axis=1).T.astype(BF16)
    gate_b = jnp.concatenate([ml_i_b[li], ml_f_b[li]])
    u_packed, vt_packed = _table_prep(peer_u, peer_v, li)
    return dict(
        w_segs=w_segs, w_small=w_small, w_gt=w_gt,
        w2p=jnp.pad(gla_w2[li], ((0, LANES - GLA_GATE_RANK), (0, 0))),
        b2=gla_b2[li][None, :], gng=gla_norm_g[li][None, :],
        cw=conv_w[li], cb=conv_b[li][None, :],
        gbr=jnp.concatenate([jnp.pad(ml_i_b[li], (0, LANES - ML_HEADS)),
                             jnp.pad(ml_f_b[li], (0, LANES - ML_HEADS))])[None, :],
        gbc=gate_b[:, None], mng=ml_norm_g[li][None, :],
        wout=w_out[li].astype(BF16),
        wq=peer_wq[li].astype(BF16), sk=peer_subkeys[li].astype(BF16),
        u=u_packed, vt=vt_packed,
    )


def _table_prep_kernel(u_ref, v_ref, up_ref, vtp_ref):
    up_ref[...] = pltpu.bitcast((u_ref[0] * GELU_IN_SCALE).astype(BF16), jnp.uint32)
    vtp_ref[...] = pltpu.bitcast(v_ref[0].T.astype(BF16), jnp.uint32)


def _table_prep(u_tabs, v_tabs, li, eb=1024):
    _, ne, d = u_tabs.shape
    return pl.pallas_call(
        _table_prep_kernel,
        grid=(ne // eb,),
        in_specs=[pl.BlockSpec((1, eb, d), lambda i: (li, i, 0)), pl.BlockSpec((1, eb, d), lambda i: (li, i, 0))],
        out_specs=[pl.BlockSpec((eb // 2, d), lambda i: (i, 0)), pl.BlockSpec((d // 2, eb), lambda i: (0, i))],
        out_shape=[jax.ShapeDtypeStruct((ne // 2, d), jnp.uint32),
                   jax.ShapeDtypeStruct((d // 2, ne), jnp.uint32)],
        compiler_params=_params(("arbitrary",)),
        name="table_prep",
    )(u_tabs, v_tabs)


def _pick(n, cands):
    for c in cands:
        if n % c == 0:
            return c
    return n


def _peer(x1, g2, lw, fg, final_norm):
    t = x1.shape[0]
    tq = _pick(t, (512, 256, 128))
    tb = _pick(t, (512, 256, 128))
    cw = _pick(tb, (2 * LANES, LANES))
    h2, rank, p1, cnt, p0 = _peer_select(x1, g2, lw["wq"], lw["sk"], tq, cw)
    return _peer_dense(x1, h2, rank, p1, cnt, p0, lw["u"], lw["vt"], fg, tb, 2048, cw, final_norm)


def kernel(x_prompt, x_sample, state_gla, state_mlstm_c, state_mlstm_n, state_mlstm_m, state_conv,
           norm1_g, w_in, gla_w2, gla_b2, gla_norm_g, conv_w, conv_b, ml_i_b, ml_f_b, ml_norm_g,
           w_out, norm2_g, peer_wq, peer_subkeys, peer_u, peer_v, final_g):
    depth = w_in.shape[0]
    bsz, seq, _ = x_prompt.shape
    nb = x_sample.shape[0]
    assert seq % CHUNK == 0 and x_sample.shape[1] == 1 and nb % LANES == 0
    tp = bsz * seq
    tm = _pick(tp, (2048, 1024, 512, 256, 128))
    tc = _pick(seq, (256, 128, 64))
    npar = _pick(bsz, (2, 1))
    fg = final_g[None, :]

    xp = x_prompt.reshape(tp, D_MODEL)
    xs = x_sample.reshape(nb, D_MODEL)
    p_out = [[] for _ in range(5)]
    s_out = [[] for _ in range(5)]
    s_gla = s_c = None
    for li in range(depth):
        lw = _layer_weights(li, w_in, gla_w2, gla_b2, gla_norm_g, conv_w, conv_b, ml_i_b, ml_f_b,
                            ml_norm_g, w_out, peer_wq, peer_subkeys, peer_u, peer_v)
        g1 = norm1_g[li][None, :]
        g2 = norm2_g[li][None, :]
        last = li == depth - 1

        big, sm, gt = _inproj(xp, g1, lw["w_segs"], lw["w_small"], lw["w_gt"], tm)
        gtc = gt.reshape(8, bsz, seq // CHUNK, CHUNK).transpose(1, 2, 0, 3)
        x1, gla, c, n, m, conv = _mixer_prompt(
            xp.reshape(bsz, seq, D_MODEL), big.reshape(bsz, seq, BIG_COLS),
            sm.reshape(bsz, seq, SMALL_COLS), gtc, lw, bsz, seq, tc, npar)
        xp = _peer(x1.reshape(tp, D_MODEL), g2, lw, fg, last)
        for lst, val in zip(p_out, (gla, c, n, m[:, 0, :ML_HEADS], conv)):
            lst.append(val)

        big, sm, _ = _inproj(xs, g1, lw["w_segs"], lw["w_small"], lw["w_gt"], nb)
        x1, s_gla, s_c, n, m, conv = _mixer_sample(
            xs, big, sm, state_gla, state_mlstm_c,
            state_mlstm_n[li].reshape(nb, ML_W),
            jnp.pad(state_mlstm_m[li], ((0, 0), (0, LANES - ML_HEADS))),
            state_conv[li].reshape(nb, (CONV_W - 1) * 2 * ML_W), lw, li, s_gla, s_c)
        xs = _peer(x1, g2, lw, fg, last)
        for lst, val in zip(s_out[2:], (n.reshape(nb, ML_HEADS, ML_DH), m[:, :ML_HEADS],
                                        conv.reshape(nb, CONV_W - 1, 2 * ML_W))):
            lst.append(val)

    y_prompt = xp.reshape(bsz, seq, D_MODEL)
    y_sample = xs.reshape(nb, 1, D_MODEL)
    return (y_prompt, y_sample, *[jnp.stack(v) for v in p_out],
            s_gla, s_c, *[jnp.stack(v) for v in s_out[2:]])
```
